```python
import math
import jax
import jax.numpy as jnp
from jax import lax
import numpy as np

D_MODEL = 1024
BATCH = 4
SEQ = 8192
DEPTH = 2
DEC_BATCH = 32
DEC_SEQ = 32
PAST_LEN = 1024

CHUNK = 64
N_PAST_CHUNKS = 8
ATT_HEADS = 8
ATT_HEAD_DIM = 64
ATT_WIDTH = ATT_HEADS * ATT_HEAD_DIM
MAX_REL = 256
REL_SIZE = MAX_REL + CHUNK
M_HEADS = 4
M_HEAD_DIM = 128
M_WIDTH = M_HEADS * M_HEAD_DIM
CONV_W = 4
D_FF = -(-8 * D_MODEL // (3 * 256)) * 256
IN_WIDTH = 3 * ATT_WIDTH + 4 * M_WIDTH + 2 * M_HEADS + 2 * D_MODEL
EPS = 1e-6
NEG = -1e30

kernel_name = 'hybrid_stream_bandattn_mlstm_step'


def _split_points():
    widths = [ATT_WIDTH] * 3 + [M_WIDTH] * 4 + [M_HEADS] * 2 + [D_MODEL] * 2
    return [int(p) for p in np.cumsum(widths)[:-1]]


def rmsnorm(x, g):
    xf = x.astype(jnp.float32)
    y = xf * lax.rsqrt(jnp.mean(xf * xf, axis=-1, keepdims=True) + EPS)
    return (y * g.astype(jnp.float32)).astype(x.dtype)


def head_rmsnorm(h, gain):
    B, T = h.shape[0], h.shape[1]
    y = h * lax.rsqrt(jnp.mean(h * h, axis=-1, keepdims=True) + EPS)
    return y.reshape(B, T, M_WIDTH) * gain.astype(jnp.float32)


def rel_bias_block(rel_bias, n_q, n_k, n_past):
    dist = jnp.arange(n_q)[:, None] + n_past - jnp.arange(n_k)[None, :]
    idx = jnp.clip(dist, -(CHUNK - 1), MAX_REL) + (CHUNK - 1)
    return rel_bias[:, idx]


def band_core(q, k, v, valid, bias):
    s = jnp.einsum('nqhd,nkhd->nhqk', q, k).astype(jnp.float32) * (ATT_HEAD_DIM ** -0.5)
    s = s + bias[None].astype(jnp.float32)
    s = jnp.where(valid[:, None, None, :], s, NEG)
    p = jax.nn.softmax(s, axis=-1).astype(v.dtype)
    return jnp.einsum('nhqk,nkhd->nqhd', p, v)


def prompt_band_attention(q, k, v, rel_bias):
    B, S, H, Dh = q.shape
    nc = S // CHUNK
    P = N_PAST_CHUNKS * CHUNK
    L = P + CHUNK
    idx = jnp.arange(nc)[:, None] * CHUNK + jnp.arange(L)[None, :]
    valid = idx >= P
    bias = rel_bias_block(rel_bias, CHUNK, L, P)

    def one(args):
        qs, ks, vs = args
        kp = jnp.pad(ks, ((P, 0), (0, 0), (0, 0)))[idx]
        vp = jnp.pad(vs, ((P, 0), (0, 0), (0, 0)))[idx]
        out = band_core(qs.reshape(nc, CHUNK, H, Dh), kp, vp, valid, bias)
        return out.reshape(S, H, Dh)

    return lax.map(one, (q, k, v))


def sample_band_attention(q, k, v, k_cache, v_cache, rel_bias):
    T = q.shape[1]
    pc = k_cache.shape[1]
    kb = jnp.concatenate([k_cache.astype(k.dtype), k], axis=1)[:, None]
    vb = jnp.concatenate([v_cache.astype(v.dtype), v], axis=1)[:, None]
    valid = jnp.ones((1, pc + T), dtype=bool)
    bias = rel_bias_block(rel_bias, T, pc + T, pc)
    out = jax.vmap(band_core, in_axes=(0, 0, 0, None, None))(q[:, None], kb, vb, valid, bias)
    return out[:, 0]


def mlstm_chunkwise(q, k, v, log_i, log_f, C0, n0, m0, block):
    B, T, H, Dk = q.shape
    Dv = v.shape[-1]
    nb = T // block

    def blocks(t):
        return t.astype(jnp.float32).reshape(B, nb, block, H, -1).transpose(1, 0, 3, 2, 4)

    def gblocks(t):
        return t.reshape(B, nb, block, H).transpose(1, 0, 3, 2)

    tril = jnp.tril(jnp.ones((block, block), dtype=bool))

    def step(carry, xs):
        C, n, m = carry
        qc, kc, vc, li, lf = xs
        b = jnp.cumsum(lf, axis=-1)
        a = b + m[..., None]
        D = b[..., :, None] - b[..., None, :] + li[..., None, :]
        D = jnp.where(tril, D, NEG)
        m_t = jnp.maximum(a, jnp.max(D, axis=-1))
        w_inter = jnp.exp(a - m_t)
        W = jnp.exp(D - m_t[..., None])
        S = jnp.einsum('bhtd,bhsd->bhts', qc, kc) * W
        num = jnp.einsum('bhts,bhsv->bhtv', S, vc) + w_inter[..., None] * jnp.einsum('bhtd,bhdv->bhtv', qc, C)
        den = jnp.sum(S, axis=-1) + w_inter * jnp.einsum('bhtd,bhd->bht', qc, n)
        h = num / jnp.maximum(jnp.abs(den), jnp.exp(-m_t))[..., None]
        m_new = m_t[..., -1]
        g_state = jnp.exp(b[..., -1] + m - m_new)
        w_s = jnp.exp(b[..., -1:] - b + li - m_new[..., None])
        C_new = g_state[..., None, None] * C + jnp.einsum('bhs,bhsd,bhsv->bhdv', w_s, kc, vc)
        n_new = g_state[..., None] * n + jnp.einsum('bhs,bhsd->bhd', w_s, kc)
        return (C_new, n_new, m_new), h

    carry0 = (C0.astype(jnp.float32), n0.astype(jnp.float32), m0.astype(jnp.float32))
    xs = (blocks(q), blocks(k), blocks(v), gblocks(log_i), gblocks(log_f))
    (C1, n1, m1), hs = lax.scan(step, carry0, xs)
    h = hs.transpose(1, 0, 3, 2, 4).reshape(B, T, H, Dv)
    return h, (C1, n1, m1)


def trunk_layer(x, c, k_cache, v_cache, conv_left, C0, n0, m0,
                w_ada, b_ada, g_mix, w_in, b_if, conv_w, conv_b, rel_bias, mh_gain,
                w_br_att, w_br_mlstm, w_out, g_ffn, w_gate_up, w_down):
    B, T, _ = x.shape
    dt = x.dtype
    mod = jax.nn.silu(c) @ w_ada + b_ada
    sh1, sc1, gt1, sh2, sc2, gt2 = jnp.split(mod[:, None, :], 6, axis=-1)

    h = rmsnorm(x, g_mix) * (1 + sc1) + sh1
    u = h @ w_in
    qa, ka, va, qm, km, vm, om, ip, fp, ga, gm = jnp.split(u, _split_points(), axis=-1)

    qa = qa.reshape(B, T, ATT_HEADS, ATT_HEAD_DIM)
    ka = ka.reshape(B, T, ATT_HEADS, ATT_HEAD_DIM)
    va = va.reshape(B, T, ATT_HEADS, ATT_HEAD_DIM)
    if k_cache is None:
        ya = prompt_band_attention(qa, ka, va, rel_bias)
        keep = min(N_PAST_CHUNKS * CHUNK, T)
        new_k, new_v = ka[:, T - keep:], va[:, T - keep:]
    else:
        ya = sample_band_attention(qa, ka, va, k_cache, v_cache, rel_bias)
        new_k, new_v = ka, va

    qk_in = jnp.concatenate([qm, km], axis=-1)
    xpad = jnp.concatenate([conv_left.astype(dt), qk_in], axis=1)
    conv = sum(xpad[:, j:j + T] * conv_w[j] for j in range(CONV_W)) + conv_b
    new_conv = xpad[:, -(CONV_W - 1):]
    conv = jax.nn.silu(conv)
    q_m, k_m = jnp.split(conv, 2, axis=-1)
    q_m = q_m.reshape(B, T, M_HEADS, M_HEAD_DIM)
    k_m = k_m.reshape(B, T, M_HEADS, M_HEAD_DIM) * (M_HEAD_DIM ** -0.5)
    v_m = vm.reshape(B, T, M_HEADS, M_HEAD_DIM)
    log_i = (ip + b_if[:M_HEADS]).astype(jnp.float32)
    log_f = jax.nn.log_sigmoid((fp + b_if[M_HEADS:]).astype(jnp.float32))
    block = CHUNK if T % CHUNK == 0 else T
    hm, (C1, n1, m1) = mlstm_chunkwise(q_m, k_m, v_m, log_i, log_f, C0, n0, m0, block)
    ym = (head_rmsnorm(hm, mh_gain) * jax.nn.sigmoid(om.astype(jnp.float32))).astype(dt)

    merged = (jax.nn.sigmoid(ga) * (ya.reshape(B, T, ATT_WIDTH) @ w_br_att)
              + jax.nn.sigmoid(gm) * (ym @ w_br_mlstm))
    x = x + gt1 * (merged @ w_out)

    h2 = rmsnorm(x, g_ffn) * (1 + sc2) + sh2
    g, up = jnp.split(h2 @ w_gate_up, 2, axis=-1)
    x = x + gt2 * ((jax.nn.silu(g) * up) @ w_down)
    return x, (new_k, new_v, new_conv, C1.astype(dt), n1.astype(dt), m1.astype(dt))


def setup_inputs(seed: int = 0) -> dict:
    key = jax.random.key(seed)
    ks = jax.random.split(key, 32)
    f32 = jnp.float32
    D = D_MODEL

    def nrm(k, shape, scale):
        return jax.random.normal(k, shape, f32) * scale

    pc = min(N_PAST_CHUNKS * CHUNK, PAST_LEN)
    b_if = jnp.concatenate([
        nrm(ks[14], (DEPTH, M_HEADS), 0.1),
        jnp.linspace(3.0, 6.0, M_HEADS, dtype=f32)[None, :] + nrm(ks[15], (DEPTH, M_HEADS), 0.1),
    ], axis=-1)
    return {
        'x_prompt': nrm(ks[0], (BATCH, SEQ, D), 1.0),
        'x_sample': nrm(ks[1], (DEC_BATCH, DEC_SEQ, D), 1.0),
        'cache_k': nrm(ks[2], (DEPTH, DEC_BATCH, pc, ATT_HEADS, ATT_HEAD_DIM), 1.0),
        'cache_v': nrm(ks[3], (DEPTH, DEC_BATCH, pc, ATT_HEADS, ATT_HEAD_DIM), 1.0),
        'state_conv': nrm(ks[4], (DEPTH, DEC_BATCH, CONV_W - 1, 2 * M_WIDTH), 1.0),
        'state_C': nrm(ks[5], (DEPTH, DEC_BATCH, M_HEADS, M_HEAD_DIM, M_HEAD_DIM), 1.0),
        'state_n': nrm(ks[6], (DEPTH, DEC_BATCH, M_HEADS, M_HEAD_DIM), 1.0),
        'state_m': nrm(ks[7], (DEPTH, DEC_BATCH, M_HEADS), 1.0),
        'c_prompt': nrm(ks[8], (BATCH, D), 1.0),
        'c_sample': nrm(ks[9], (DEC_BATCH, D), 1.0),
        'w_ada': nrm(ks[10], (DEPTH, D, 6 * D), 0.5 * D ** -0.5),
        'b_ada': nrm(ks[11], (DEPTH, 6 * D), 0.02),
        'g_mix': 1.0 + nrm(ks[12], (DEPTH, D), 0.02),
        'w_in': nrm(ks[13], (DEPTH, D, IN_WIDTH), D ** -0.5),
        'b_if': b_if,
        'conv_w': nrm(ks[16], (DEPTH, CONV_W, 2 * M_WIDTH), CONV_W ** -0.5),
        'conv_b': nrm(ks[17], (DEPTH, 2 * M_WIDTH), 0.02),
        'rel_bias': nrm(ks[18], (DEPTH, ATT_HEADS, REL_SIZE), 0.5),
        'mh_gain': 1.0 + nrm(ks[19], (DEPTH, M_WIDTH), 0.02),
        'w_br_att': nrm(ks[20], (DEPTH, ATT_WIDTH, D), ATT_WIDTH ** -0.5),
        'w_br_mlstm': nrm(ks[21], (DEPTH, M_WIDTH, D), M_WIDTH ** -0.5),
        'w_out': nrm(ks[22], (DEPTH, D, D), D ** -0.5),
        'g_ffn': 1.0 + nrm(ks[23], (DEPTH, D), 0.02),
        'w_gate_up': nrm(ks[24], (DEPTH, D, 2 * D_FF), D ** -0.5),
        'w_down': nrm(ks[25], (DEPTH, D_FF, D), D_FF ** -0.5),
        'g_final': 1.0 + nrm(ks[26], (D,), 0.02),
    }


def reference(x_prompt, x_sample, cache_k, cache_v, state_conv, state_C, state_n, state_m,
              c_prompt, c_sample, w_ada, b_ada, g_mix, w_in, b_if, conv_w, conv_b, rel_bias,
              mh_gain, w_br_att, w_br_mlstm, w_out, g_ffn, w_gate_up, w_down, g_final):
    bp = x_prompt.shape[0]
    xp, xs = x_prompt, x_sample
    outs_p, outs_s = [], []
    for l in range(DEPTH):
        wl = (w_ada[l], b_ada[l], g_mix[l], w_in[l], b_if[l], conv_w[l], conv_b[l], rel_bias[l],
              mh_gain[l], w_br_att[l], w_br_mlstm[l], w_out[l], g_ffn[l], w_gate_up[l], w_down[l])
        zc = jnp.zeros((bp, CONV_W - 1, 2 * M_WIDTH), xp.dtype)
        zC = jnp.zeros((bp, M_HEADS, M_HEAD_DIM, M_HEAD_DIM), jnp.float32)
        zn = jnp.zeros((bp, M_HEADS, M_HEAD_DIM), jnp.float32)
        zm = jnp.zeros((bp, M_HEADS), jnp.float32)
        xp, st_p = trunk_layer(xp, c_prompt, None, None, zc, zC, zn, zm, *wl)
        xs, st_s = trunk_layer(xs, c_sample, cache_k[l], cache_v[l], state_conv[l],
                               state_C[l], state_n[l], state_m[l], *wl)
        outs_p.append(st_p)
        outs_s.append(st_s)
    y_prompt = rmsnorm(xp, g_final)
    y_sample = rmsnorm(xs, g_final)

    def stk(outs, i):
        return jnp.stack([o[i] for o in outs], axis=0)

    return (y_prompt, y_sample,
            stk(outs_p, 0), stk(outs_p, 1), stk(outs_p, 2), stk(outs_p, 3), stk(outs_p, 4), stk(outs_p, 5),
            stk(outs_s, 0), stk(outs_s, 1), stk(outs_s, 2), stk(outs_s, 3), stk(outs_s, 4), stk(outs_s, 5))
```

```python
import functools

import jax
import jax.numpy as jnp
from jax import lax
from jax.experimental import pallas as pl
from jax.experimental.pallas import tpu as pltpu

F32 = jnp.float32
BF16 = jnp.bfloat16

CHUNK = 64
N_PAST_CHUNKS = 8
ATT_HEADS = 8
ATT_HEAD_DIM = 64
ATT_WIDTH = ATT_HEADS * ATT_HEAD_DIM
MAX_REL = 256
M_HEADS = 4
M_HEAD_DIM = 128
M_WIDTH = M_HEADS * M_HEAD_DIM
CONV_W = 4
EPS = 1e-6
NEG = -1e30

LANES = 128
SUBLANES = 8
VMEM_LIMIT = 56 * 1024 * 1024
PROMPT_ROWS = 512
SAMPLE_BATCH_TILE = 8
BAND = (N_PAST_CHUNKS + 1) * CHUNK
N_PAIRS = ATT_HEADS // 2


def _sigmoid(x):
    return 1.0 / (1.0 + jnp.exp(-x))


def _log_sigmoid(x):
    return jnp.minimum(x, 0.0) - jnp.log1p(jnp.exp(-jnp.abs(x)))


def _const_spec(shape, index_map):
    return pl.BlockSpec(shape, index_map, pipeline_mode=pl.Buffered(1))


def _params(sem):
    return pltpu.CompilerParams(dimension_semantics=sem, vmem_limit_bytes=VMEM_LIMIT)


def _ada_kernel(c_ref, w_ref, b_ref, o_ref):
    c = c_ref[...]
    a = (c * _sigmoid(c)).astype(BF16)
    o_ref[0] = jnp.dot(a, w_ref[0].astype(BF16), preferred_element_type=F32) + b_ref[0]


def _ada_call(c_all, w_ada, b_ada):
    depth, d, n = w_ada.shape
    r = c_all.shape[0]
    tn = 1536
    assert n % tn == 0
    return pl.pallas_call(
        _ada_kernel,
        grid=(depth, n // tn),
        in_specs=[
            pl.BlockSpec((r, d), lambda l, j: (0, 0)),
            pl.BlockSpec((1, d, tn), lambda l, j: (l, 0, j)),
            pl.BlockSpec((1, 1, tn), lambda l, j: (l, 0, j)),
        ],
        out_specs=pl.BlockSpec((1, r, tn), lambda l, j: (l, 0, j)),
        out_shape=jax.ShapeDtypeStruct((depth, r, n), F32),
        compiler_params=_params(("parallel", "parallel")),
        name="ada_mod",
    )(c_all, w_ada, b_ada.reshape(depth, 1, n))


def _inproj_kernel(x_ref, sh_ref, sc_ref, g_ref, w_ref, wg_ref, bif_ref, cw_ref, cb_ref, cinit_ref,
                   qkv_ref, kt_ref, vt_ref, qkm_ref, vm_ref, om_ref, gm_ref, gates_ref, ctail_ref,
                   conv_scr, *, bt, tt):
    t = pl.program_id(1)
    d = x_ref.shape[-1]
    m = bt * tt
    x = x_ref[...]
    ms = jnp.mean(x * x, axis=-1, keepdims=True)
    h = x * lax.rsqrt(ms + EPS) * g_ref[...]
    h = h * (1.0 + sc_ref[0, :, 0]) + sh_ref[0, :, 0]
    hb = h.reshape(m, d).astype(BF16)

    def proj(c0, width):
        return jnp.dot(hb, w_ref[:, c0:c0 + width], preferred_element_type=F32)

    aw = ATT_WIDTH
    mw = M_WIDTH
    qkv_ref[:, :, 0:aw] = (proj(0, aw) * (ATT_HEAD_DIM ** -0.5)).astype(BF16).reshape(bt, tt, aw)
    ka = proj(aw, aw).reshape(bt, tt, aw)
    qkv_ref[:, :, aw:2 * aw] = ka.astype(BF16)
    kt_ref[...] = ka
    va = proj(2 * aw, aw).reshape(bt, tt, aw)
    qkv_ref[:, :, 2 * aw:3 * aw] = va.astype(BF16)
    vt_ref[...] = va
    c0 = 3 * aw

    @pl.when(t == 0)
    def _():
        conv_scr[:, 0:SUBLANES, :] = cinit_ref[...]

    conv_scr[:, SUBLANES:SUBLANES + tt, 0:mw] = proj(c0, mw).reshape(bt, tt, mw)
    conv_scr[:, SUBLANES:SUBLANES + tt, mw:2 * mw] = proj(c0 + mw, mw).reshape(bt, tt, mw)
    lo = SUBLANES - (CONV_W - 1)
    acc = conv_scr[:, lo:lo + tt, :] * cw_ref[0:1, :]
    for j in range(1, CONV_W):
        acc = acc + conv_scr[:, lo + j:lo + j + tt, :] * cw_ref[j:j + 1, :]
    acc = acc + cb_ref[...]
    act = acc * _sigmoid(acc)
    qkm_ref[:, :, 0:mw] = act[:, :, 0:mw].astype(BF16)
    qkm_ref[:, :, mw:2 * mw] = (act[:, :, mw:2 * mw] * (M_HEAD_DIM ** -0.5)).astype(BF16)
    tail = conv_scr[:, tt:tt + SUBLANES, :]
    ctail_ref[...] = tail
    conv_scr[:, 0:SUBLANES, :] = tail
    c0 += 2 * mw
    vm_ref[...] = proj(c0, mw).astype(BF16).reshape(bt, tt, mw)
    om_ref[...] = proj(c0 + mw, mw).reshape(bt, tt, mw)
    c0 += 2 * mw
    for j in range(0, 2 * d, 512):
        gm_ref[:, :, j:j + 512] = proj(c0 + j, 512).reshape(bt, tt, 512)
    ug = jnp.dot(hb, wg_ref[...], preferred_element_type=F32) + bif_ref[...]
    col = lax.broadcasted_iota(jnp.int32, ug.shape, 1)
    gates_ref[...] = jnp.where(col >= M_HEADS, _log_sigmoid(ug), ug).reshape(bt, tt, LANES)


def _inproj_call(x, mod, layer, row0, g_mix, w_main, w_gates, b_if, conv_w, conv_b, conv_init, bt, tt):
    b, t, d = x.shape
    nb, nt = b // bt, t // tt
    keep = min(N_PAST_CHUNKS * CHUNK, t)
    assert b % bt == 0 and t % tt == 0 and keep % tt == 0
    t_keep0 = nt - keep // tt
    n_main = w_main.shape[1]
    aw, mw = ATT_WIDTH, M_WIDTH

    def tok(width):
        return pl.BlockSpec((bt, tt, width), lambda i, j: (i, j, 0))

    def modspec(k):
        return pl.BlockSpec((1, bt, 1, 1, d), lambda i, j: (layer, row0 // bt + i, k, 0, 0))

    def const2(shape):
        return _const_spec(shape, lambda i, j: (0, 0))

    keep_spec = pl.BlockSpec((bt, tt, aw), lambda i, j: (i, jnp.maximum(j - t_keep0, 0), 0))
    per_batch8 = pl.BlockSpec((bt, SUBLANES, 2 * mw), lambda i, j: (i, 0, 0))
    outs = pl.pallas_call(
        functools.partial(_inproj_kernel, bt=bt, tt=tt),
        grid=(nb, nt),
        in_specs=[
            tok(d), modspec(0), modspec(1), const2((1, d)),
            const2((d, n_main)), const2((d, LANES)), const2((1, LANES)),
            const2((CONV_W, 2 * mw)), const2((1, 2 * mw)), per_batch8,
        ],
        out_specs=[tok(3 * aw), keep_spec, keep_spec, tok(2 * mw), tok(mw), tok(mw), tok(2 * d),
                   tok(LANES), per_batch8],
        out_shape=[
            jax.ShapeDtypeStruct((b, t, 3 * aw), BF16),
            jax.ShapeDtypeStruct((b, keep, aw), F32),
            jax.ShapeDtypeStruct((b, keep, aw), F32),
            jax.ShapeDtypeStruct((b, t, 2 * mw), BF16),
            jax.ShapeDtypeStruct((b, t, mw), BF16),
            jax.ShapeDtypeStruct((b, t, mw), F32),
            jax.ShapeDtypeStruct((b, t, 2 * d), F32),
            jax.ShapeDtypeStruct((b, t, LANES), F32),
            jax.ShapeDtypeStruct((b, SUBLANES, 2 * mw), F32),
        ],
        scratch_shapes=[pltpu.VMEM((bt, SUBLANES + tt, 2 * mw), F32)],
        compiler_params=_params(("parallel", "arbitrary")),
        name="inproj",
    )(x, mod, mod, g_mix, w_main, w_gates, b_if, conv_w, conv_b, conv_init)
    return outs


def _pair_rows(qp):
    lane = lax.broadcasted_iota(jnp.int32, qp.shape, 1)
    zero = jnp.zeros_like(qp)
    return jnp.concatenate([jnp.where(lane < ATT_HEAD_DIM, qp, zero),
                            jnp.where(lane >= ATT_HEAD_DIM, qp, zero)], axis=0)


def _pair_merge(o, rows):
    lane = lax.broadcasted_iota(jnp.int32, (rows, LANES), 1)
    return jnp.where(lane < ATT_HEAD_DIM, o[0:rows], o[rows:2 * rows])


def _nt_dot(a, b):
    return lax.dot_general(a, b, (((1,), (1,)), ((), ())), preferred_element_type=F32)


def _attn_prompt_kernel(q_ref, kp_ref, kc_ref, vp_ref, vc_ref, bias_ref, o_ref, kband, vband):
    j = pl.program_id(1)
    rows = kc_ref.shape[1]
    past = N_PAST_CHUNKS * CHUNK
    assert rows == past

    @pl.when(j == 0)
    def _():
        kband[0:past, :] = jnp.zeros((past, ATT_WIDTH), BF16)
        vband[0:past, :] = jnp.zeros((past, ATT_WIDTH), BF16)

    @pl.when(j > 0)
    def _():
        kband[0:past, :] = kp_ref[0]
        vband[0:past, :] = vp_ref[0]

    kband[past:past + rows, :] = kc_ref[0]
    vband[past:past + rows, :] = vc_ref[0]

    def chunk(ci, carry):
        r0 = pl.multiple_of(ci * CHUNK, CHUNK)
        first_valid = (N_PAST_CHUNKS - N_PAST_CHUNKS * j - ci) * CHUNK
        col = lax.broadcasted_iota(jnp.int32, (2 * CHUNK, BAND), 1)
        valid = col >= first_valid
        for p in range(N_PAIRS):
            ls = slice(p * LANES, (p + 1) * LANES)
            qs = _pair_rows(q_ref[0, pl.ds(r0, CHUNK), ls])
            s = _nt_dot(qs, kband[pl.ds(r0, BAND), ls]) + bias_ref[p]
            s = jnp.where(valid, s, NEG)
            mx = jnp.max(s, axis=-1, keepdims=True)
            e = jnp.exp(s - mx)
            den = jnp.sum(e, axis=-1, keepdims=True)
            o = jnp.dot(e.astype(BF16), vband[pl.ds(r0, BAND), ls], preferred_element_type=F32) / den
            o_ref[0, pl.ds(r0, CHUNK), ls] = _pair_merge(o, CHUNK).astype(BF16)
        return carry

    lax.fori_loop(0, rows // CHUNK, chunk, 0)


def _attn_prompt_call(qkv, bias_pairs):
    b, t, _ = qkv.shape
    rows = N_PAST_CHUNKS * CHUNK
    assert t % rows == 0
    aw = ATT_WIDTH

    def cur(cb):
        return pl.BlockSpec((1, rows, aw), lambda i, j: (i, j, cb))

    def prev(cb):
        return pl.BlockSpec((1, rows, aw), lambda i, j: (i, jnp.maximum(j - 1, 0), cb))

    return pl.pallas_call(
        _attn_prompt_kernel,
        grid=(b, t // rows),
        in_specs=[cur(0), prev(1), cur(1), prev(2), cur(2),
                  _const_spec((N_PAIRS, 2 * CHUNK, BAND), lambda i, j: (0, 0, 0))],
        out_specs=pl.BlockSpec((1, rows, aw), lambda i, j: (i, j, 0)),
        out_shape=jax.ShapeDtypeStruct((b, t, aw), BF16),
        scratch_shapes=[pltpu.VMEM((2 * rows, aw), BF16), pltpu.VMEM((2 * rows, aw), BF16)],
        compiler_params=_params(("parallel", "parallel")),
        name="attn_prompt",
    )(qkv, qkv, qkv, qkv, qkv, bias_pairs)


def _attn_sample_kernel(q_ref, kn_ref, vn_ref, kc_ref, vc_ref, bias_ref, o_ref, *, bb):
    tq = q_ref.shape[1]
    pc = kc_ref.shape[1]
    for bi in range(bb):
        for p in range(N_PAIRS):
            ls = slice(p * LANES, (p + 1) * LANES)
            qs = _pair_rows(q_ref[bi, :, ls])
            s1 = _nt_dot(qs, kc_ref[bi, :, ls].astype(BF16)) + bias_ref[p, :, 0:pc]
            s2 = _nt_dot(qs, kn_ref[bi, :, ls]) + bias_ref[p, :, pc:pc + tq]
            mx = jnp.maximum(jnp.max(s1, axis=-1, keepdims=True), jnp.max(s2, axis=-1, keepdims=True))
            e1 = jnp.exp(s1 - mx)
            e2 = jnp.exp(s2 - mx)
            den = jnp.sum(e1, axis=-1, keepdims=True) + jnp.sum(e2, axis=-1, keepdims=True)
            o = (jnp.dot(e1.astype(BF16), vc_ref[bi, :, ls].astype(BF16), preferred_element_type=F32)
                 + jnp.dot(e2.astype(BF16), vn_ref[bi, :, ls], preferred_element_type=F32)) / den
            o_ref[bi, :, ls] = _pair_merge(o, tq).astype(BF16)


def _attn_sample_call(qkv, cache_k, cache_v, bias_pairs, bb):
    b, tq, _ = qkv.shape
    pc = cache_k.shape[1]
    aw = ATT_WIDTH
    assert b % bb == 0

    def new(cb):
        return pl.BlockSpec((bb, tq, aw), lambda i: (i, 0, cb))

    cache = pl.BlockSpec((bb, pc, aw), lambda i: (i, 0, 0))
    return pl.pallas_call(
        functools.partial(_attn_sample_kernel, bb=bb),
        grid=(b // bb,),
        in_specs=[new(0), new(1), new(2), cache, cache,
                  _const_spec((N_PAIRS, 2 * tq, pc + tq), lambda i: (0, 0, 0))],
        out_specs=pl.BlockSpec((bb, tq, aw), lambda i: (i, 0, 0)),
        out_shape=jax.ShapeDtypeStruct((b, tq, aw), BF16),
        compiler_params=_params(("parallel",)),
        name="attn_sample",
    )(qkv, qkv, qkv, cache_k, cache_v, bias_pairs)


def _transpose_rows(a):
    r = a.shape[0]
    if r < LANES:
        a = jnp.concatenate([a, jnp.zeros((LANES - r, a.shape[1]), a.dtype)], axis=0)
    return a.T[:, 0:r]


def _mlstm_kernel(qkm_ref, vm_ref, om_ref, gates_ref, gain_ref, c0_ref, n0_ref, m0_ref,
                  ym_ref, c1_ref, n1_ref, m1_ref, c_scr, n_scr, m_scr, *, blk, nblk):
    tb = pl.program_id(1)
    hd = M_HEAD_DIM

    @pl.when(tb == 0)
    def _():
        c_scr[...] = c0_ref[0]
        n_scr[...] = n0_ref[0]
        m_scr[...] = m0_ref[0]

    row = lax.broadcasted_iota(jnp.int32, (blk, blk), 0)
    colm = lax.broadcasted_iota(jnp.int32, (blk, blk), 1)
    tril = row >= colm
    triu = row <= colm

    def block(bi, carry):
        r0 = pl.multiple_of(bi * blk, blk)
        g = gates_ref[0, pl.ds(r0, blk), :]
        gt = _transpose_rows(g)
        for h in range(M_HEADS):
            ls = slice(h * hd, (h + 1) * hd)
            li_col, lf_col = g[:, h:h + 1], g[:, M_HEADS + h:M_HEADS + h + 1]
            li_row, lf_row = gt[h:h + 1, :], gt[M_HEADS + h:M_HEADS + h + 1, :]
            b_col = jnp.sum(jnp.where(tril, lf_row, 0.0), axis=1, keepdims=True)
            b_row = jnp.sum(jnp.where(triu, lf_col, 0.0), axis=0, keepdims=True)
            m_prev = m_scr[h][:, 0:1]
            dmat = jnp.where(tril, b_col - b_row + li_row, NEG)
            a = b_col + m_prev
            m_t = jnp.maximum(a, jnp.max(dmat, axis=1, keepdims=True))
            w_inter = jnp.exp(a - m_t)
            wmat = jnp.exp(dmat - m_t)
            q = qkm_ref[0, pl.ds(r0, blk), ls]
            k = qkm_ref[0, pl.ds(r0, blk), M_WIDTH + h * hd:M_WIDTH + (h + 1) * hd]
            v = vm_ref[0, pl.ds(r0, blk), ls]
            smat = _nt_dot(q, k) * wmat
            c_prev = c_scr[h]
            n_prev = n_scr[h]
            inter = jnp.dot(q, c_prev.astype(BF16), preferred_element_type=F32)
            num = jnp.dot(smat.astype(BF16), v, preferred_element_type=F32) + w_inter * inter
            qn = jnp.sum(q.astype(F32) * n_prev, axis=1, keepdims=True)
            den = jnp.sum(smat, axis=1, keepdims=True) + w_inter * qn
            hh = num / jnp.maximum(jnp.abs(den), jnp.exp(-m_t))
            y = hh * lax.rsqrt(jnp.mean(hh * hh, axis=1, keepdims=True) + EPS) * gain_ref[:, ls]
            y = y * _sigmoid(om_ref[0, pl.ds(r0, blk), ls])
            ym_ref[0, pl.ds(r0, blk), ls] = y.astype(BF16)
            m_new = m_t[blk - 1:blk, :]
            b_last = b_col[blk - 1:blk, :]
            g_state = jnp.exp(b_last + m_prev - m_new)
            w_s = jnp.exp(b_last - b_col + li_col - m_new)
            kw = w_s * k.astype(F32)
            upd = jnp.dot(_transpose_rows(kw).astype(BF16), v, preferred_element_type=F32)
            c_scr[h] = g_state * c_prev + upd
            n_scr[h] = g_state * n_prev + jnp.sum(kw, axis=0, keepdims=True)
            m_scr[h] = jnp.broadcast_to(m_new, (1, LANES))
        return carry

    lax.fori_loop(0, nblk, block, 0)

    @pl.when(tb == pl.num_programs(1) - 1)
    def _():
        c1_ref[0] = c_scr[...]
        n1_ref[0] = n_scr[...]
        m1_ref[0] = m_scr[...]


def _mlstm_call(qkm, vm, om, gates, gain, c0, n0, m0, blk, nblk):
    b, t, _ = qkm.shape
    rt = blk * nblk
    assert t % rt == 0
    mw, hd, nh = M_WIDTH, M_HEAD_DIM, M_HEADS

    def tok(width):
        return pl.BlockSpec((1, rt, width), lambda i, j: (i, j, 0))

    cspec = pl.BlockSpec((1, nh, hd, hd), lambda i, j: (i, 0, 0, 0))
    vspec = pl.BlockSpec((1, nh, 1, hd), lambda i, j: (i, 0, 0, 0))
    return pl.pallas_call(
        functools.partial(_mlstm_kernel, blk=blk, nblk=nblk),
        grid=(b, t // rt),
        in_specs=[tok(2 * mw), tok(mw), tok(mw), tok(LANES),
                  _const_spec((1, mw), lambda i, j: (0, 0)), cspec, vspec, vspec],
        out_specs=[tok(mw), cspec, vspec, vspec],
        out_shape=[
            jax.ShapeDtypeStruct((b, t, mw), BF16),
            jax.ShapeDtypeStruct((b, nh, hd, hd), F32),
            jax.ShapeDtypeStruct((b, nh, 1, hd), F32),
            jax.ShapeDtypeStruct((b, nh, 1, hd), F32),
        ],
        scratch_shapes=[pltpu.VMEM((nh, hd, hd), F32), pltpu.VMEM((nh, 1, hd), F32),
                        pltpu.VMEM((nh, 1, hd), F32)],
        compiler_params=_params(("parallel", "arbitrary")),
        name="mlstm",
    )(qkm, vm, om, gates, gain, c0, n0, m0)


def _ffn_kernel(x_ref, ya_ref, ym_ref, gm_ref, gt1_ref, sh2_ref, sc2_ref, gt2_ref, gffn_ref, gfin_ref,
                wba_ref, wbm_ref, wo_ref, wgu_ref, wd_ref, o_ref, act_scr, *, bt, tt, final):
    d = x_ref.shape[-1]
    m = bt * tt
    dff = wd_ref.shape[0]
    x = x_ref[...]
    pa = jnp.dot(ya_ref[...].reshape(m, ATT_WIDTH), wba_ref[...], preferred_element_type=F32)
    pm = jnp.dot(ym_ref[...].reshape(m, M_WIDTH), wbm_ref[...], preferred_element_type=F32)
    ga = gm_ref[:, :, 0:d].reshape(m, d)
    gmm = gm_ref[:, :, d:2 * d].reshape(m, d)
    merged = _sigmoid(ga) * pa + _sigmoid(gmm) * pm
    y1 = jnp.dot(merged.astype(BF16), wo_ref[...], preferred_element_type=F32)
    x1 = x + gt1_ref[0, :, 0] * y1.reshape(bt, tt, d)
    ms = jnp.mean(x1 * x1, axis=-1, keepdims=True)
    h2 = x1 * lax.rsqrt(ms + EPS) * gffn_ref[...]
    h2 = (h2 * (1.0 + sc2_ref[0, :, 0]) + sh2_ref[0, :, 0]).reshape(m, d).astype(BF16)
    cw = 256
    for c in range(0, dff, cw):
        g = jnp.dot(h2, wgu_ref[:, c:c + cw], preferred_element_type=F32)
        up = jnp.dot(h2, wgu_ref[:, dff + c:dff + c + cw], preferred_element_type=F32)
        act_scr[:, c:c + cw] = (g * _sigmoid(g) * up).astype(BF16)
    y2 = jnp.dot(act_scr[...], wd_ref[...], preferred_element_type=F32)
    x2 = x1 + gt2_ref[0, :, 0] * y2.reshape(bt, tt, d)
    if final:
        ms2 = jnp.mean(x2 * x2, axis=-1, keepdims=True)
        x2 = x2 * lax.rsqrt(ms2 + EPS) * gfin_ref[...]
    o_ref[...] = x2


def _ffn_call(x, ya, ym, gm, mod, layer, row0, g_ffn, g_final, w_br_att, w_br_mlstm, w_out, w_gate_up,
              w_down, bt, tt, final):
    b, t, d = x.shape
    dff = w_down.shape[0]
    assert b % bt == 0 and t % tt == 0 and dff % 256 == 0

    def tok(width):
        return pl.BlockSpec((bt, tt, width), lambda i, j: (i, j, 0))

    def modspec(k):
        return pl.BlockSpec((1, bt, 1, 1, d), lambda i, j: (layer, row0 // bt + i, k, 0, 0))

    def const2(shape):
        return _const_spec(shape, lambda i, j: (0, 0))

    return pl.pallas_call(
        functools.partial(_ffn_kernel, bt=bt, tt=tt, final=final),
        grid=(b // bt, t // tt),
        in_specs=[tok(d), tok(ATT_WIDTH), tok(M_WIDTH), tok(2 * d),
                  modspec(2), modspec(3), modspec(4), modspec(5), const2((1, d)), const2((1, d)),
                  const2((ATT_WIDTH, d)), const2((M_WIDTH, d)), const2((d, d)), const2((d, 2 * dff)),
                  const2((dff, d))],
        out_specs=tok(d),
        out_shape=jax.ShapeDtypeStruct((b, t, d), F32),
        scratch_shapes=[pltpu.VMEM((bt * tt, dff), BF16)],
        compiler_params=_params(("parallel", "parallel")),
        name="merge_ffn",
    )(x, ya, ym, gm, mod, mod, mod, mod, g_ffn, g_final, w_br_att, w_br_mlstm, w_out, w_gate_up, w_down)


def _bias_pairs(rel_bias, n_q, n_k, n_past):
    dist = jnp.arange(n_q)[:, None] + n_past - jnp.arange(n_k)[None, :]
    idx = jnp.clip(dist, -(CHUNK - 1), MAX_REL) + (CHUNK - 1)
    return rel_bias[:, idx].reshape(N_PAIRS, 2 * n_q, n_k)


def _layer(x, mod, layer, row0, cache, conv_init, c0, n0, m0, wts, bt, tt, blk, nblk, final):
    (g_mix, w_main, w_gates, b_if, conv_w, conv_b, rel_bias, mh_gain, w_br_att, w_br_mlstm, w_out, g_ffn,
     w_gate_up, w_down, g_final) = wts
    b, t, _ = x.shape
    qkv, kt, vt, qkm, vm, om, gm, gates, ctail = _inproj_call(
        x, mod, layer, row0, g_mix, w_main, w_gates, b_if, conv_w, conv_b, conv_init, bt, tt)
    if cache is None:
        ya = _attn_prompt_call(qkv, _bias_pairs(rel_bias, CHUNK, BAND, N_PAST_CHUNKS * CHUNK))
    else:
        ck, cv = cache
        pc = ck.shape[1]
        ya = _attn_sample_call(qkv, ck.reshape(b, pc, ATT_WIDTH), cv.reshape(b, pc, ATT_WIDTH),
                               _bias_pairs(rel_bias, t, pc + t, pc), bt)
    ym, c1, n1, m1 = _mlstm_call(qkm, vm, om, gates, mh_gain, c0, n0, m0, blk, nblk)
    x = _ffn_call(x, ya, ym, gm, mod, layer, row0, g_ffn, g_final, w_br_att, w_br_mlstm, w_out, w_gate_up,
                  w_down, bt, tt, final)
    keep = kt.shape[1]
    state = (kt.reshape(b, keep, ATT_HEADS, ATT_HEAD_DIM), vt.reshape(b, keep, ATT_HEADS, ATT_HEAD_DIM),
             ctail[:, SUBLANES - (CONV_W - 1):, :], c1, n1[:, :, 0, :], m1[:, :, 0, 0])
    return x, state


def kernel(x_prompt, x_sample, cache_k, cache_v, state_conv, state_C, state_n, state_m, c_prompt, c_sample,
           w_ada, b_ada, g_mix, w_in, b_if, conv_w, conv_b, rel_bias, mh_gain, w_br_att, w_br_mlstm, w_out,
           g_ffn, w_gate_up, w_down, g_final):
    depth = w_ada.shape[0]
    bp, tp, d = x_prompt.shape
    bs, ts, _ = x_sample.shape
    assert tp % PROMPT_ROWS == 0 and tp % CHUNK == 0 and bs % SAMPLE_BATCH_TILE == 0 and ts % CHUNK != 0
    mw = M_WIDTH
    n_att = 3 * ATT_WIDTH + 4 * M_WIDTH

    mod = _ada_call(jnp.concatenate([c_sample, c_prompt], axis=0), w_ada, b_ada)
    mod = mod.reshape(depth, bs + bp, 6, 1, d)

    pad3 = SUBLANES - (CONV_W - 1)
    zero_conv = jnp.zeros((bp, SUBLANES, 2 * mw), F32)
    zero_c = jnp.zeros((bp, M_HEADS, M_HEAD_DIM, M_HEAD_DIM), F32)
    zero_v = jnp.zeros((bp, M_HEADS, 1, M_HEAD_DIM), F32)

    xp, xs = x_prompt, x_sample
    outs_p, outs_s = [], []
    for l in range(depth):
        w_main = jnp.concatenate([w_in[l, :, :n_att], w_in[l, :, n_att + 2 * M_HEADS:]], axis=1).astype(BF16)
        w_gates = jnp.pad(w_in[l, :, n_att:n_att + 2 * M_HEADS], ((0, 0), (0, LANES - 2 * M_HEADS))).astype(BF16)
        bif = jnp.pad(b_if[l], (0, LANES - 2 * M_HEADS)).reshape(1, LANES)
        wts = (g_mix[l].reshape(1, d), w_main, w_gates, bif, conv_w[l], conv_b[l].reshape(1, 2 * mw),
               rel_bias[l], mh_gain[l].reshape(1, mw), w_br_att[l].astype(BF16), w_br_mlstm[l].astype(BF16),
               w_out[l].astype(BF16), g_ffn[l].reshape(1, d), w_gate_up[l].astype(BF16),
               w_down[l].astype(BF16), g_final.reshape(1, d))
        final = l == depth - 1
        xp, st_p = _layer(xp, mod, l, bs, None, zero_conv, zero_c, zero_v, zero_v, wts,
                          1, PROMPT_ROWS, CHUNK, PROMPT_ROWS // CHUNK, final)
        conv_init = jnp.pad(state_conv[l], ((0, 0), (pad3, 0), (0, 0)))
        m0 = jnp.broadcast_to(state_m[l][:, :, None, None], (bs, M_HEADS, 1, M_HEAD_DIM))
        xs, st_s = _layer(xs, mod, l, 0, (cache_k[l], cache_v[l]), conv_init, state_C[l],
                          state_n[l][:, :, None, :], m0, wts, SAMPLE_BATCH_TILE, ts, ts, 1, final)
        outs_p.append(st_p)
        outs_s.append(st_s)

    def stk(outs, i):
        return jnp.stack([o[i] for o in outs], axis=0)

    return (xp, xs,
            stk(outs_p, 0), stk(outs_p, 1), stk(outs_p, 2), stk(outs_p, 3), stk(outs_p, 4), stk(outs_p, 5),
            stk(outs_s, 0), stk(outs_s, 1), stk(outs_s, 2), stk(outs_s, 3), stk(outs_s, 4), stk(outs_s, 5))
```

```python
import functools

import jax
import jax.numpy as jnp
from jax import lax
from jax.experimental import pallas as pl
from jax.experimental.pallas import tpu as pltpu

F32 = jnp.float32
BF16 = jnp.bfloat16

CHUNK = 64
N_PAST_CHUNKS = 8
ATT_HEADS = 8
ATT_HEAD_DIM = 64
ATT_WIDTH = ATT_HEADS * ATT_HEAD_DIM
MAX_REL = 256
M_HEADS = 4
M_HEAD_DIM = 128
M_WIDTH = M_HEADS * M_HEAD_DIM
CONV_W = 4
EPS = 1e-6
NEG = -1e30

LANES = 128
SUBLANES = 8
VMEM_LIMIT = 56 * 1024 * 1024
PROMPT_ROWS = 512
SAMPLE_BATCH_TILE = 8
BAND = (N_PAST_CHUNKS + 1) * CHUNK
N_PAIRS = ATT_HEADS // 2
ATTN_UNROLL = 2
MLSTM_BLOCK = 128
LOG2E = 1.4426950408889634


def _sigmoid(x):
    return 1.0 / (1.0 + jnp.exp(-x))


def _log_sigmoid(x):
    return jnp.minimum(x, 0.0) - jnp.log1p(jnp.exp(-jnp.abs(x)))


def _const_spec(shape, index_map):
    return pl.BlockSpec(shape, index_map, pipeline_mode=pl.Buffered(1))


def _params(sem):
    return pltpu.CompilerParams(dimension_semantics=sem, vmem_limit_bytes=VMEM_LIMIT)


def _ada_kernel(c_ref, w_ref, b_ref, o_ref):
    c = c_ref[...]
    a = (c * _sigmoid(c)).astype(BF16)
    o_ref[0] = jnp.dot(a, w_ref[0].astype(BF16), preferred_element_type=F32) + b_ref[0]


def _ada_call(c_all, w_ada, b_ada):
    depth, d, n = w_ada.shape
    r = c_all.shape[0]
    tn = 1536
    assert n % tn == 0
    return pl.pallas_call(
        _ada_kernel,
        grid=(depth, n // tn),
        in_specs=[
            pl.BlockSpec((r, d), lambda l, j: (0, 0)),
            pl.BlockSpec((1, d, tn), lambda l, j: (l, 0, j)),
            pl.BlockSpec((1, 1, tn), lambda l, j: (l, 0, j)),
        ],
        out_specs=pl.BlockSpec((1, r, tn), lambda l, j: (l, 0, j)),
        out_shape=jax.ShapeDtypeStruct((depth, r, n), F32),
        compiler_params=_params(("parallel", "parallel")),
        name="ada_mod",
    )(c_all, w_ada, b_ada.reshape(depth, 1, n))


def _inproj_kernel(x_ref, sh_ref, sc_ref, g_ref, w_ref, wg_ref, bif_ref, cw_ref, cb_ref, cinit_ref,
                   qkv_ref, kt_ref, vt_ref, qkm_ref, vm_ref, om_ref, gm_ref, gates_ref, ctail_ref,
                   conv_scr, *, bt, tt):
    t = pl.program_id(1)
    d = x_ref.shape[-1]
    m = bt * tt
    x = x_ref[...]
    ms = jnp.mean(x * x, axis=-1, keepdims=True)
    h = x * lax.rsqrt(ms + EPS) * g_ref[...]
    h = h * (1.0 + sc_ref[0, :, 0]) + sh_ref[0, :, 0]
    hb = h.reshape(m, d).astype(BF16)

    def proj(c0, width):
        return jnp.dot(hb, w_ref[:, c0:c0 + width], preferred_element_type=F32)

    aw = ATT_WIDTH
    mw = M_WIDTH
    qkv_ref[:, :, 0:aw] = (proj(0, aw) * (LOG2E * ATT_HEAD_DIM ** -0.5)).astype(BF16).reshape(bt, tt, aw)
    ka = proj(aw, aw).reshape(bt, tt, aw)
    qkv_ref[:, :, aw:2 * aw] = ka.astype(BF16)
    kt_ref[...] = ka
    va = proj(2 * aw, aw).reshape(bt, tt, aw)
    qkv_ref[:, :, 2 * aw:3 * aw] = va.astype(BF16)
    vt_ref[...] = va
    c0 = 3 * aw

    @pl.when(t == 0)
    def _():
        conv_scr[:, 0:SUBLANES, :] = cinit_ref[...]

    conv_scr[:, SUBLANES:SUBLANES + tt, 0:mw] = proj(c0, mw).reshape(bt, tt, mw)
    conv_scr[:, SUBLANES:SUBLANES + tt, mw:2 * mw] = proj(c0 + mw, mw).reshape(bt, tt, mw)
    lo = SUBLANES - (CONV_W - 1)
    acc = conv_scr[:, lo:lo + tt, :] * cw_ref[0:1, :]
    for j in range(1, CONV_W):
        acc = acc + conv_scr[:, lo + j:lo + j + tt, :] * cw_ref[j:j + 1, :]
    acc = acc + cb_ref[...]
    act = acc * _sigmoid(acc)
    qkm_ref[:, :, 0:mw] = act[:, :, 0:mw].astype(BF16)
    qkm_ref[:, :, mw:2 * mw] = (act[:, :, mw:2 * mw] * (M_HEAD_DIM ** -0.5)).astype(BF16)
    tail = conv_scr[:, tt:tt + SUBLANES, :]
    ctail_ref[...] = tail
    conv_scr[:, 0:SUBLANES, :] = tail
    c0 += 2 * mw
    vm_ref[...] = proj(c0, mw).astype(BF16).reshape(bt, tt, mw)
    om_ref[...] = proj(c0 + mw, mw).reshape(bt, tt, mw)
    c0 += 2 * mw
    for j in range(0, 2 * d, 512):
        gm_ref[:, :, j:j + 512] = proj(c0 + j, 512).reshape(bt, tt, 512)
    ug = jnp.dot(hb, wg_ref[...], preferred_element_type=F32) + bif_ref[...]
    col = lax.broadcasted_iota(jnp.int32, ug.shape, 1)
    gates_ref[...] = jnp.where(col >= M_HEADS, _log_sigmoid(ug), ug).reshape(bt, tt, LANES)


def _inproj_call(x, mod, layer, row0, g_mix, w_main, w_gates, b_if, conv_w, conv_b, conv_init, bt, tt):
    b, t, d = x.shape
    nb, nt = b // bt, t // tt
    keep = min(N_PAST_CHUNKS * CHUNK, t)
    assert b % bt == 0 and t % tt == 0 and keep % tt == 0
    t_keep0 = nt - keep // tt
    n_main = w_main.shape[1]
    aw, mw = ATT_WIDTH, M_WIDTH

    def tok(width):
        return pl.BlockSpec((bt, tt, width), lambda i, j: (i, j, 0))

    def modspec(k):
        return pl.BlockSpec((1, bt, 1, 1, d), lambda i, j: (layer, row0 // bt + i, k, 0, 0))

    def const2(shape):
        return _const_spec(shape, lambda i, j: (0, 0))

    keep_spec = pl.BlockSpec((bt, tt, aw), lambda i, j: (i, jnp.maximum(j - t_keep0, 0), 0))
    per_batch8 = pl.BlockSpec((bt, SUBLANES, 2 * mw), lambda i, j: (i, 0, 0))
    outs = pl.pallas_call(
        functools.partial(_inproj_kernel, bt=bt, tt=tt),
        grid=(nb, nt),
        in_specs=[
            tok(d), modspec(0), modspec(1), const2((1, d)),
            const2((d, n_main)), const2((d, LANES)), const2((1, LANES)),
            const2((CONV_W, 2 * mw)), const2((1, 2 * mw)), per_batch8,
        ],
        out_specs=[tok(3 * aw), keep_spec, keep_spec, tok(2 * mw), tok(mw), tok(mw), tok(2 * d),
                   tok(LANES), per_batch8],
        out_shape=[
            jax.ShapeDtypeStruct((b, t, 3 * aw), BF16),
            jax.ShapeDtypeStruct((b, keep, aw), F32),
            jax.ShapeDtypeStruct((b, keep, aw), F32),
            jax.ShapeDtypeStruct((b, t, 2 * mw), BF16),
            jax.ShapeDtypeStruct((b, t, mw), BF16),
            jax.ShapeDtypeStruct((b, t, mw), F32),
            jax.ShapeDtypeStruct((b, t, 2 * d), F32),
            jax.ShapeDtypeStruct((b, t, LANES), F32),
            jax.ShapeDtypeStruct((b, SUBLANES, 2 * mw), F32),
        ],
        scratch_shapes=[pltpu.VMEM((bt, SUBLANES + tt, 2 * mw), F32)],
        compiler_params=_params(("parallel", "arbitrary")),
        name="inproj",
    )(x, mod, mod, g_mix, w_main, w_gates, b_if, conv_w, conv_b, conv_init)
    return outs


def _bias_kernel(rev_ref, o_ref):
    w = rev_ref.shape[-1]
    x = jnp.broadcast_to(rev_ref[0, 0], (CHUNK, w))
    r = pltpu.roll(x, w - (CHUNK - 1), 1, stride=1, stride_axis=0)
    o_ref[0, 0] = r[:, 0:BAND] * LOG2E


def _bias_call(rel_bias):
    depth, nh, rel = rel_bias.shape
    n_ext = BAND + CHUNK - 1
    w = -(-n_ext // LANES) * LANES
    ext = jnp.concatenate([rel_bias, jnp.broadcast_to(rel_bias[:, :, rel - 1:], (depth, nh, n_ext - rel))], axis=2)
    rev = jnp.pad(ext[:, :, ::-1], ((0, 0), (0, 0), (0, w - n_ext))).reshape(depth, nh, 1, w)
    out = pl.pallas_call(
        _bias_kernel,
        grid=(depth, nh),
        in_specs=[pl.BlockSpec((1, 1, 1, w), lambda l, h: (l, h, 0, 0))],
        out_specs=pl.BlockSpec((1, 1, CHUNK, BAND), lambda l, h: (l, h, 0, 0)),
        out_shape=jax.ShapeDtypeStruct((depth, nh, CHUNK, BAND), F32),
        compiler_params=_params(("parallel", "parallel")),
        name="rel_bias",
    )(rev)
    return out.reshape(depth, N_PAIRS, 2 * CHUNK, BAND)


def _pair_rows(qp):
    lane = lax.broadcasted_iota(jnp.int32, qp.shape, 1)
    zero = jnp.zeros_like(qp)
    return jnp.concatenate([jnp.where(lane < ATT_HEAD_DIM, qp, zero),
                            jnp.where(lane >= ATT_HEAD_DIM, qp, zero)], axis=0)


def _pair_merge(o, rows):
    lane = lax.broadcasted_iota(jnp.int32, (rows, LANES), 1)
    return jnp.where(lane < ATT_HEAD_DIM, o[0:rows], o[rows:2 * rows])


def _nt_dot(a, b):
    return lax.dot_general(a, b, (((1,), (1,)), ((), ())), preferred_element_type=F32)


def _attn_prompt_kernel(q_ref, kp_ref, kc_ref, vp_ref, vc_ref, bias_ref, o_ref, kband, vband, *, unroll):
    j = pl.program_id(1)
    rows = kc_ref.shape[1]
    past = N_PAST_CHUNKS * CHUNK
    assert rows == past

    @pl.when(j == 0)
    def _():
        kband[0:past, :] = jnp.zeros((past, ATT_WIDTH), BF16)
        vband[0:past, :] = jnp.zeros((past, ATT_WIDTH), BF16)

    @pl.when(j > 0)
    def _():
        kband[0:past, :] = kp_ref[0]
        vband[0:past, :] = vp_ref[0]

    kband[past:past + rows, :] = kc_ref[0]
    vband[past:past + rows, :] = vc_ref[0]

    def chunks(i, masked):
        units = [(pl.multiple_of((i * unroll + u) * CHUNK, CHUNK), i * unroll + u, p)
                 for u in range(unroll) for p in range(N_PAIRS)]
        scores = []
        for r0, ci, p in units:
            ls = slice(p * LANES, (p + 1) * LANES)
            qs = _pair_rows(q_ref[0, pl.ds(r0, CHUNK), ls])
            scores.append(_nt_dot(qs, kband[pl.ds(r0, BAND), ls]) + bias_ref[0, p])
        for (r0, ci, p), s in zip(units, scores):
            ls = slice(p * LANES, (p + 1) * LANES)
            if masked:
                col = lax.broadcasted_iota(jnp.int32, s.shape, 1)
                s = jnp.where(col >= (N_PAST_CHUNKS - ci) * CHUNK, s, NEG)
            mx = jnp.max(s, axis=-1, keepdims=True)
            e = jnp.exp2(s - mx)
            den = jnp.sum(e, axis=-1, keepdims=True)
            o = jnp.dot(e.astype(BF16), vband[pl.ds(r0, BAND), ls], preferred_element_type=F32) / den
            o_ref[0, pl.ds(r0, CHUNK), ls] = _pair_merge(o, CHUNK).astype(BF16)

    n_iter = rows // (CHUNK * unroll)

    @pl.when(j == 0)
    def _():
        lax.fori_loop(0, n_iter, lambda i, c: (chunks(i, True), c)[1], 0)

    @pl.when(j > 0)
    def _():
        lax.fori_loop(0, n_iter, lambda i, c: (chunks(i, False), c)[1], 0)


def _attn_prompt_call(qkv, bias, layer):
    b, t, _ = qkv.shape
    rows = N_PAST_CHUNKS * CHUNK
    assert t % rows == 0
    aw = ATT_WIDTH

    def cur(cb):
        return pl.BlockSpec((1, rows, aw), lambda i, j: (i, j, cb))

    def prev(cb):
        return pl.BlockSpec((1, rows, aw), lambda i, j: (i, jnp.maximum(j - 1, 0), cb))

    return pl.pallas_call(
        functools.partial(_attn_prompt_kernel, unroll=ATTN_UNROLL),
        grid=(b, t // rows),
        in_specs=[cur(0), prev(1), cur(1), prev(2), cur(2),
                  _const_spec((1, N_PAIRS, 2 * CHUNK, BAND), lambda i, j: (layer, 0, 0, 0))],
        out_specs=pl.BlockSpec((1, rows, aw), lambda i, j: (i, j, 0)),
        out_shape=jax.ShapeDtypeStruct((b, t, aw), BF16),
        scratch_shapes=[pltpu.VMEM((2 * rows, aw), BF16), pltpu.VMEM((2 * rows, aw), BF16)],
        compiler_params=_params(("parallel", "parallel")),
        name="attn_prompt",
    )(qkv, qkv, qkv, qkv, qkv, bias)


def _attn_sample_kernel(q_ref, kn_ref, vn_ref, kc_ref, vc_ref, bias_ref, o_ref, *, bb):
    tq = q_ref.shape[1]
    pc = kc_ref.shape[2]
    for bi in range(bb):
        for p in range(N_PAIRS):
            ls = slice(p * LANES, (p + 1) * LANES)
            qs = _pair_rows(q_ref[bi, :, ls])
            bias = jnp.concatenate([bias_ref[0, p, 0:tq, :], bias_ref[0, p, CHUNK:CHUNK + tq, :]], axis=0)
            s1 = _nt_dot(qs, kc_ref[0, bi, :, ls].astype(BF16)) + bias[:, 0:pc]
            s2 = _nt_dot(qs, kn_ref[bi, :, ls]) + bias[:, pc:pc + tq]
            mx = jnp.maximum(jnp.max(s1, axis=-1, keepdims=True), jnp.max(s2, axis=-1, keepdims=True))
            e1 = jnp.exp2(s1 - mx)
            e2 = jnp.exp2(s2 - mx)
            den = jnp.sum(e1, axis=-1, keepdims=True) + jnp.sum(e2, axis=-1, keepdims=True)
            o = (jnp.dot(e1.astype(BF16), vc_ref[0, bi, :, ls].astype(BF16), preferred_element_type=F32)
                 + jnp.dot(e2.astype(BF16), vn_ref[bi, :, ls], preferred_element_type=F32)) / den
            o_ref[bi, :, ls] = _pair_merge(o, tq).astype(BF16)


def _attn_sample_call(qkv, cache_k, cache_v, bias, layer, bb):
    b, tq, _ = qkv.shape
    _, _, pc, aw = cache_k.shape
    assert b % bb == 0 and pc == N_PAST_CHUNKS * CHUNK and tq <= CHUNK and aw == ATT_WIDTH

    def new(cb):
        return pl.BlockSpec((bb, tq, aw), lambda i: (i, 0, cb))

    cache = pl.BlockSpec((1, bb, pc, aw), lambda i: (layer, i, 0, 0))
    return pl.pallas_call(
        functools.partial(_attn_sample_kernel, bb=bb),
        grid=(b // bb,),
        in_specs=[new(0), new(1), new(2), cache, cache,
                  _const_spec((1, N_PAIRS, 2 * CHUNK, BAND), lambda i: (layer, 0, 0, 0))],
        out_specs=pl.BlockSpec((bb, tq, aw), lambda i: (i, 0, 0)),
        out_shape=jax.ShapeDtypeStruct((b, tq, aw), BF16),
        compiler_params=_params(("parallel",)),
        name="attn_sample",
    )(qkv, qkv, qkv, cache_k, cache_v, bias)


def _transpose_rows(a):
    r = a.shape[0]
    pad = -r % LANES
    if pad:
        a = jnp.concatenate([a, jnp.zeros((pad, a.shape[1]), a.dtype)], axis=0)
    return a.T[:, 0:r]


def _block_cumsum(tri, g):
    g1 = g.astype(BF16)
    r1 = g - g1.astype(F32)
    g2 = r1.astype(BF16)
    g3 = (r1 - g2.astype(F32)).astype(BF16)
    return (jnp.dot(tri, g1, preferred_element_type=F32) + jnp.dot(tri, g2, preferred_element_type=F32)
            + jnp.dot(tri, g3, preferred_element_type=F32))


def _mlstm_kernel(qkm_ref, vm_ref, om_ref, gates_ref, gain_ref, tri_ref, c0_ref, n0_ref, m0_ref,
                  ym_ref, c1_ref, n1_ref, m1_ref, c_scr, n_scr, m_scr, *, blk, nblk):
    tb = pl.program_id(1)
    hd = M_HEAD_DIM

    @pl.when(tb == 0)
    def _():
        c_scr[...] = c0_ref[0]
        n_scr[...] = n0_ref[0]
        m_scr[...] = m0_ref[0]

    row = lax.broadcasted_iota(jnp.int32, (blk, blk), 0)
    colm = lax.broadcasted_iota(jnp.int32, (blk, blk), 1)
    tril = row >= colm

    g_all = gates_ref[0]
    b_all = _block_cumsum(tri_ref[...], g_all)
    gt_all = _transpose_rows(g_all)
    bt_all = _transpose_rows(b_all)

    for bi in range(nblk):
        rs = slice(bi * blk, (bi + 1) * blk)
        intra = []
        for h in range(M_HEADS):
            ls = slice(h * hd, (h + 1) * hd)
            li_col, b_col = g_all[rs, h:h + 1], b_all[rs, M_HEADS + h:M_HEADS + h + 1]
            li_row, b_row = gt_all[h:h + 1, rs], bt_all[M_HEADS + h:M_HEADS + h + 1, rs]
            dmat = jnp.where(tril, b_col - b_row + li_row, NEG)
            m_in = jnp.max(dmat, axis=1, keepdims=True)
            q = qkm_ref[0, rs, ls]
            k = qkm_ref[0, rs, M_WIDTH + h * hd:M_WIDTH + (h + 1) * hd]
            v = vm_ref[0, rs, ls]
            smat = _nt_dot(q, k) * jnp.exp(dmat - m_in)
            num_i = jnp.dot(smat.astype(BF16), v, preferred_element_type=F32)
            den_i = jnp.sum(smat, axis=1, keepdims=True)
            b_last = b_col[blk - 1:blk, :]
            m_u = m_in[blk - 1:blk, :]
            kw = jnp.exp(b_last - b_col + li_col - m_u) * k.astype(F32)
            upd = jnp.dot(_transpose_rows(kw).astype(BF16), v, preferred_element_type=F32)
            n_upd = jnp.sum(kw, axis=0, keepdims=True)
            intra.append((b_col, m_in, q, num_i, den_i, b_last, m_u, upd, n_upd))
        for h in range(M_HEADS):
            ls = slice(h * hd, (h + 1) * hd)
            b_col, m_in, q, num_i, den_i, b_last, m_u, upd, n_upd = intra[h]
            m_prev = m_scr[h][:, 0:1]
            c_prev = c_scr[h]
            n_prev = n_scr[h]
            a = b_col + m_prev
            m_t = jnp.maximum(a, m_in)
            alpha = jnp.exp(m_in - m_t)
            beta = jnp.exp(a - m_t)
            inter = jnp.dot(q, c_prev.astype(BF16), preferred_element_type=F32)
            qn = jnp.sum(q.astype(F32) * n_prev, axis=1, keepdims=True)
            num = alpha * num_i + beta * inter
            den = alpha * den_i + beta * qn
            hh = num / jnp.maximum(jnp.abs(den), jnp.exp(-m_t))
            y = hh * lax.rsqrt(jnp.mean(hh * hh, axis=1, keepdims=True) + EPS) * gain_ref[:, ls]
            y = y * _sigmoid(om_ref[0, rs, ls])
            ym_ref[0, rs, ls] = y.astype(BF16)
            m_new = m_t[blk - 1:blk, :]
            g_state = jnp.exp(b_last + m_prev - m_new)
            g_upd = jnp.exp(m_u - m_new)
            c_scr[h] = g_state * c_prev + g_upd * upd
            n_scr[h] = g_state * n_prev + g_upd * n_upd
            m_scr[h] = jnp.broadcast_to(m_new, (1, LANES))

    @pl.when(tb == pl.num_programs(1) - 1)
    def _():
        c1_ref[0] = c_scr[...]
        n1_ref[0] = n_scr[...]
        m1_ref[0] = m_scr[...]


def _block_tril(rt, blk):
    r = jnp.arange(rt)
    return ((r[:, None] // blk == r[None, :] // blk) & (r[None, :] <= r[:, None])).astype(BF16)


def _mlstm_call(qkm, vm, om, gates, gain, c0, n0, m0, blk, nblk):
    b, t, _ = qkm.shape
    rt = blk * nblk
    assert t % rt == 0
    mw, hd, nh = M_WIDTH, M_HEAD_DIM, M_HEADS

    def tok(width):
        return pl.BlockSpec((1, rt, width), lambda i, j: (i, j, 0))

    cspec = pl.BlockSpec((1, nh, hd, hd), lambda i, j: (i, 0, 0, 0))
    vspec = pl.BlockSpec((1, nh, 1, hd), lambda i, j: (i, 0, 0, 0))
    return pl.pallas_call(
        functools.partial(_mlstm_kernel, blk=blk, nblk=nblk),
        grid=(b, t // rt),
        in_specs=[tok(2 * mw), tok(mw), tok(mw), tok(LANES),
                  _const_spec((1, mw), lambda i, j: (0, 0)), _const_spec((rt, rt), lambda i, j: (0, 0)),
                  cspec, vspec, vspec],
        out_specs=[tok(mw), cspec, vspec, vspec],
        out_shape=[
            jax.ShapeDtypeStruct((b, t, mw), BF16),
            jax.ShapeDtypeStruct((b, nh, hd, hd), F32),
            jax.ShapeDtypeStruct((b, nh, 1, hd), F32),
            jax.ShapeDtypeStruct((b, nh, 1, hd), F32),
        ],
        scratch_shapes=[pltpu.VMEM((nh, hd, hd), F32), pltpu.VMEM((nh, 1, hd), F32),
                        pltpu.VMEM((nh, 1, hd), F32)],
        compiler_params=_params(("parallel", "arbitrary")),
        name="mlstm",
    )(qkm, vm, om, gates, gain, _block_tril(rt, blk), c0, n0, m0)


def _ffn_kernel(x_ref, ya_ref, ym_ref, gm_ref, gt1_ref, sh2_ref, sc2_ref, gt2_ref, gffn_ref, gfin_ref,
                wba_ref, wbm_ref, wo_ref, wgu_ref, wd_ref, o_ref, act_scr, *, bt, tt, final):
    d = x_ref.shape[-1]
    m = bt * tt
    dff = wd_ref.shape[0]
    x = x_ref[...]
    pa = jnp.dot(ya_ref[...].reshape(m, ATT_WIDTH), wba_ref[...], preferred_element_type=F32)
    pm = jnp.dot(ym_ref[...].reshape(m, M_WIDTH), wbm_ref[...], preferred_element_type=F32)
    ga = gm_ref[:, :, 0:d].reshape(m, d)
    gmm = gm_ref[:, :, d:2 * d].reshape(m, d)
    merged = _sigmoid(ga) * pa + _sigmoid(gmm) * pm
    y1 = jnp.dot(merged.astype(BF16), wo_ref[...], preferred_element_type=F32)
    x1 = x + gt1_ref[0, :, 0] * y1.reshape(bt, tt, d)
    ms = jnp.mean(x1 * x1, axis=-1, keepdims=True)
    h2 = x1 * lax.rsqrt(ms + EPS) * gffn_ref[...]
    h2 = (h2 * (1.0 + sc2_ref[0, :, 0]) + sh2_ref[0, :, 0]).reshape(m, d).astype(BF16)
    cw = 256
    for c in range(0, dff, cw):
        g = jnp.dot(h2, wgu_ref[:, c:c + cw], preferred_element_type=F32)
        up = jnp.dot(h2, wgu_ref[:, dff + c:dff + c + cw], preferred_element_type=F32)
        act_scr[:, c:c + cw] = (g * _sigmoid(g) * up).astype(BF16)
    y2 = jnp.dot(act_scr[...], wd_ref[...], preferred_element_type=F32)
    x2 = x1 + gt2_ref[0, :, 0] * y2.reshape(bt, tt, d)
    if final:
        ms2 = jnp.mean(x2 * x2, axis=-1, keepdims=True)
        x2 = x2 * lax.rsqrt(ms2 + EPS) * gfin_ref[...]
    o_ref[...] = x2


def _ffn_call(x, ya, ym, gm, mod, layer, row0, g_ffn, g_final, w_br_att, w_br_mlstm, w_out, w_gate_up,
              w_down, bt, tt, final):
    b, t, d = x.shape
    dff = w_down.shape[0]
    assert b % bt == 0 and t % tt == 0 and dff % 256 == 0

    def tok(width):
        return pl.BlockSpec((bt, tt, width), lambda i, j: (i, j, 0))

    def modspec(k):
        return pl.BlockSpec((1, bt, 1, 1, d), lambda i, j: (layer, row0 // bt + i, k, 0, 0))

    def const2(shape):
        return _const_spec(shape, lambda i, j: (0, 0))

    return pl.pallas_call(
        functools.partial(_ffn_kernel, bt=bt, tt=tt, final=final),
        grid=(b // bt, t // tt),
        in_specs=[tok(d), tok(ATT_WIDTH), tok(M_WIDTH), tok(2 * d),
                  modspec(2), modspec(3), modspec(4), modspec(5), const2((1, d)), const2((1, d)),
                  const2((ATT_WIDTH, d)), const2((M_WIDTH, d)), const2((d, d)), const2((d, 2 * dff)),
                  const2((dff, d))],
        out_specs=tok(d),
        out_shape=jax.ShapeDtypeStruct((b, t, d), F32),
        scratch_shapes=[pltpu.VMEM((bt * tt, dff), BF16)],
        compiler_params=_params(("parallel", "parallel")),
        name="merge_ffn",
    )(x, ya, ym, gm, mod, mod, mod, mod, g_ffn, g_final, w_br_att, w_br_mlstm, w_out, w_gate_up, w_down)


def _layer(x, mod, layer, row0, cache, conv_init, c0, n0, m0, wts, bt, tt, blk, nblk, final):
    (g_mix, w_main, w_gates, b_if, conv_w, conv_b, bias, mh_gain, w_br_att, w_br_mlstm, w_out, g_ffn,
     w_gate_up, w_down, g_final) = wts
    b, t, _ = x.shape
    qkv, kt, vt, qkm, vm, om, gm, gates, ctail = _inproj_call(
        x, mod, layer, row0, g_mix, w_main, w_gates, b_if, conv_w, conv_b, conv_init, bt, tt)
    if cache is None:
        ya = _attn_prompt_call(qkv, bias, layer)
    else:
        ya = _attn_sample_call(qkv, cache[0], cache[1], bias, layer, bt)
    ym, c1, n1, m1 = _mlstm_call(qkm, vm, om, gates, mh_gain, c0, n0, m0, blk, nblk)
    x = _ffn_call(x, ya, ym, gm, mod, layer, row0, g_ffn, g_final, w_br_att, w_br_mlstm, w_out, w_gate_up,
                  w_down, bt, tt, final)
    keep = kt.shape[1]
    state = (kt.reshape(b, keep, ATT_HEADS, ATT_HEAD_DIM), vt.reshape(b, keep, ATT_HEADS, ATT_HEAD_DIM),
             ctail[:, SUBLANES - (CONV_W - 1):, :], c1, n1[:, :, 0, :], m1[:, :, 0, 0])
    return x, state


def kernel(x_prompt, x_sample, cache_k, cache_v, state_conv, state_C, state_n, state_m, c_prompt, c_sample,
           w_ada, b_ada, g_mix, w_in, b_if, conv_w, conv_b, rel_bias, mh_gain, w_br_att, w_br_mlstm, w_out,
           g_ffn, w_gate_up, w_down, g_final):
    depth = w_ada.shape[0]
    bp, tp, d = x_prompt.shape
    bs, ts, _ = x_sample.shape
    assert tp % PROMPT_ROWS == 0 and tp % CHUNK == 0 and bs % SAMPLE_BATCH_TILE == 0 and ts % CHUNK != 0
    mw = M_WIDTH
    n_att = 3 * ATT_WIDTH + 4 * M_WIDTH

    mod = _ada_call(jnp.concatenate([c_sample, c_prompt], axis=0), w_ada, b_ada)
    mod = mod.reshape(depth, bs + bp, 6, 1, d)

    bias = _bias_call(rel_bias)
    pc = cache_k.shape[2]
    caches = (cache_k.reshape(depth, bs, pc, ATT_WIDTH), cache_v.reshape(depth, bs, pc, ATT_WIDTH))

    pad3 = SUBLANES - (CONV_W - 1)
    zero_conv = jnp.zeros((bp, SUBLANES, 2 * mw), F32)
    zero_c = jnp.zeros((bp, M_HEADS, M_HEAD_DIM, M_HEAD_DIM), F32)
    zero_v = jnp.zeros((bp, M_HEADS, 1, M_HEAD_DIM), F32)

    xp, xs = x_prompt, x_sample
    outs_p, outs_s = [], []
    for l in range(depth):
        w_main = jnp.concatenate([w_in[l, :, :n_att], w_in[l, :, n_att + 2 * M_HEADS:]], axis=1).astype(BF16)
        w_gates = jnp.pad(w_in[l, :, n_att:n_att + 2 * M_HEADS], ((0, 0), (0, LANES - 2 * M_HEADS))).astype(BF16)
        bif = jnp.pad(b_if[l], (0, LANES - 2 * M_HEADS)).reshape(1, LANES)
        wts = (g_mix[l].reshape(1, d), w_main, w_gates, bif, conv_w[l], conv_b[l].reshape(1, 2 * mw),
               bias, mh_gain[l].reshape(1, mw), w_br_att[l].astype(BF16), w_br_mlstm[l].astype(BF16),
               w_out[l].astype(BF16), g_ffn[l].reshape(1, d), w_gate_up[l].astype(BF16),
               w_down[l].astype(BF16), g_final.reshape(1, d))
        final = l == depth - 1
        xp, st_p = _layer(xp, mod, l, bs, None, zero_conv, zero_c, zero_v, zero_v, wts,
                          1, PROMPT_ROWS, MLSTM_BLOCK, PROMPT_ROWS // MLSTM_BLOCK, final)
        conv_init = jnp.pad(state_conv[l], ((0, 0), (pad3, 0), (0, 0)))
        m0 = jnp.broadcast_to(state_m[l][:, :, None, None], (bs, M_HEADS, 1, M_HEAD_DIM))
        xs, st_s = _layer(xs, mod, l, 0, caches, conv_init, state_C[l],
                          state_n[l][:, :, None, :], m0, wts, SAMPLE_BATCH_TILE, ts, ts, 1, final)
        outs_p.append(st_p)
        outs_s.append(st_s)

    def stk(outs, i):
        return jnp.stack([o[i] for o in outs], axis=0)

    return (xp, xs,
            stk(outs_p, 0), stk(outs_p, 1), stk(outs_p, 2), stk(outs_p, 3), stk(outs_p, 4), stk(outs_p, 5),
            stk(outs_s, 0), stk(outs_s, 1), stk(outs_s, 2), stk(outs_s, 3), stk(outs_s, 4), stk(outs_s, 5))
```

```python
import functools

import jax
import jax.numpy as jnp
from jax import lax
from jax.experimental import pallas as pl
from jax.experimental.pallas import tpu as pltpu

F32 = jnp.float32
BF16 = jnp.bfloat16

CHUNK = 64
N_PAST_CHUNKS = 8
ATT_HEADS = 8
ATT_HEAD_DIM = 64
ATT_WIDTH = ATT_HEADS * ATT_HEAD_DIM
MAX_REL = 256
M_HEADS = 4
M_HEAD_DIM = 128
M_WIDTH = M_HEADS * M_HEAD_DIM
CONV_W = 4
EPS = 1e-6
NEG = -1e30

LANES = 128
SUBLANES = 8
VMEM_LIMIT = 56 * 1024 * 1024
PROMPT_ROWS = 512
SAMPLE_BATCH_TILE = 8
BAND = (N_PAST_CHUNKS + 1) * CHUNK
N_PAIRS = ATT_HEADS // 2
ATTN_UNROLL = 2
MLSTM_BLOCK = 128
STATE_EXT_ROWS = 16
LOG2E = 1.4426950408889634


def _sigmoid(x):
    return 1.0 / (1.0 + jnp.exp2(x * (-LOG2E)))


def _log_sigmoid(x):
    return jnp.minimum(x, 0.0) - jnp.log1p(jnp.exp(-jnp.abs(x)))


def _const_spec(shape, index_map):
    return pl.BlockSpec(shape, index_map, pipeline_mode=pl.Buffered(1))


def _params(sem):
    return pltpu.CompilerParams(dimension_semantics=sem, vmem_limit_bytes=VMEM_LIMIT)


def _ada_kernel(c_ref, w_ref, b_ref, o_ref):
    c = c_ref[...]
    a = (c * _sigmoid(c)).astype(BF16)
    o_ref[0] = jnp.dot(a, w_ref[0].astype(BF16), preferred_element_type=F32) + b_ref[0]


def _ada_call(c_all, w_ada, b_ada):
    depth, d, n = w_ada.shape
    r = c_all.shape[0]
    tn = 1536
    assert n % tn == 0
    return pl.pallas_call(
        _ada_kernel,
        grid=(depth, n // tn),
        in_specs=[
            pl.BlockSpec((r, d), lambda l, j: (0, 0)),
            pl.BlockSpec((1, d, tn), lambda l, j: (l, 0, j)),
            pl.BlockSpec((1, 1, tn), lambda l, j: (l, 0, j)),
        ],
        out_specs=pl.BlockSpec((1, r, tn), lambda l, j: (l, 0, j)),
        out_shape=jax.ShapeDtypeStruct((depth, r, n), F32),
        compiler_params=_params(("parallel", "parallel")),
        name="ada_mod",
    )(c_all, w_ada, b_ada.reshape(depth, 1, n))


def _inproj_kernel(x_ref, sh_ref, sc_ref, g_ref, w_ref, wg_ref, bif_ref, cw_ref, cb_ref, cinit_ref,
                   qkv_ref, kt_ref, vt_ref, qkm_ref, vm_ref, om_ref, gm_ref, gates_ref, ctail_ref,
                   conv_scr, *, bt, tt):
    t = pl.program_id(1)
    d = x_ref.shape[-1]
    m = bt * tt
    x = x_ref[...]
    ms = jnp.mean(x * x, axis=-1, keepdims=True)
    h = x * lax.rsqrt(ms + EPS) * g_ref[...]
    h = h * (1.0 + sc_ref[0, :, 0]) + sh_ref[0, :, 0]
    hb = h.reshape(m, d).astype(BF16)

    def proj(c0, width):
        return jnp.dot(hb, w_ref[:, c0:c0 + width], preferred_element_type=F32)

    aw = ATT_WIDTH
    mw = M_WIDTH
    u = proj(0, 3 * aw)
    qkv_ref[:, :, 0:aw] = (u[:, 0:aw] * (LOG2E * ATT_HEAD_DIM ** -0.5)).astype(BF16).reshape(bt, tt, aw)
    qkv_ref[:, :, aw:3 * aw] = u[:, aw:3 * aw].astype(BF16).reshape(bt, tt, 2 * aw)
    kt_ref[...] = u[:, aw:2 * aw].reshape(bt, tt, aw)
    vt_ref[...] = u[:, 2 * aw:3 * aw].reshape(bt, tt, aw)
    c0 = 3 * aw

    @pl.when(t == 0)
    def _():
        conv_scr[:, 0:SUBLANES, :] = cinit_ref[...]

    conv_scr[:, SUBLANES:SUBLANES + tt, :] = proj(c0, 2 * mw).reshape(bt, tt, 2 * mw)
    lo = SUBLANES - (CONV_W - 1)
    acc = conv_scr[:, lo:lo + tt, :] * cw_ref[0:1, :]
    for j in range(1, CONV_W):
        acc = acc + conv_scr[:, lo + j:lo + j + tt, :] * cw_ref[j:j + 1, :]
    acc = acc + cb_ref[...]
    act = acc * _sigmoid(acc)
    qkm_ref[:, :, 0:mw] = act[:, :, 0:mw].astype(BF16)
    qkm_ref[:, :, mw:2 * mw] = (act[:, :, mw:2 * mw] * (M_HEAD_DIM ** -0.5)).astype(BF16)
    tail = conv_scr[:, tt:tt + SUBLANES, :]
    ctail_ref[...] = tail
    conv_scr[:, 0:SUBLANES, :] = tail
    c0 += 2 * mw
    u = proj(c0, 2 * mw)
    vm_ref[...] = u[:, 0:mw].astype(BF16).reshape(bt, tt, mw)
    om_ref[...] = u[:, mw:2 * mw].reshape(bt, tt, mw)
    c0 += 2 * mw
    for j in range(0, 2 * d, d):
        gm_ref[:, :, j:j + d] = proj(c0 + j, d).reshape(bt, tt, d)
    ug = jnp.dot(hb, wg_ref[...], preferred_element_type=F32) + bif_ref[...]
    col = lax.broadcasted_iota(jnp.int32, ug.shape, 1)
    gates_ref[...] = jnp.where(col >= M_HEADS, _log_sigmoid(ug), ug).reshape(bt, tt, LANES)


def _inproj_call(x, mod, layer, row0, g_mix, w_main, w_gates, b_if, conv_w, conv_b, conv_init, bt, tt):
    b, t, d = x.shape
    nb, nt = b // bt, t // tt
    keep = min(N_PAST_CHUNKS * CHUNK, t)
    assert b % bt == 0 and t % tt == 0 and keep % tt == 0
    t_keep0 = nt - keep // tt
    n_main = w_main.shape[1]
    aw, mw = ATT_WIDTH, M_WIDTH

    def tok(width):
        return pl.BlockSpec((bt, tt, width), lambda i, j: (i, j, 0))

    def modspec(k):
        return pl.BlockSpec((1, bt, 1, 1, d), lambda i, j: (layer, row0 // bt + i, k, 0, 0))

    def const2(shape):
        return _const_spec(shape, lambda i, j: (0, 0))

    keep_spec = pl.BlockSpec((bt, tt, aw), lambda i, j: (i, jnp.maximum(j - t_keep0, 0), 0))
    per_batch8 = pl.BlockSpec((bt, SUBLANES, 2 * mw), lambda i, j: (i, 0, 0))
    outs = pl.pallas_call(
        functools.partial(_inproj_kernel, bt=bt, tt=tt),
        grid=(nb, nt),
        in_specs=[
            tok(d), modspec(0), modspec(1), const2((1, d)),
            const2((d, n_main)), const2((d, LANES)), const2((1, LANES)),
            const2((CONV_W, 2 * mw)), const2((1, 2 * mw)), per_batch8,
        ],
        out_specs=[tok(3 * aw), keep_spec, keep_spec, tok(2 * mw), tok(mw), tok(mw), tok(2 * d),
                   tok(LANES), per_batch8],
        out_shape=[
            jax.ShapeDtypeStruct((b, t, 3 * aw), BF16),
            jax.ShapeDtypeStruct((b, keep, aw), F32),
            jax.ShapeDtypeStruct((b, keep, aw), F32),
            jax.ShapeDtypeStruct((b, t, 2 * mw), BF16),
            jax.ShapeDtypeStruct((b, t, mw), BF16),
            jax.ShapeDtypeStruct((b, t, mw), F32),
            jax.ShapeDtypeStruct((b, t, 2 * d), F32),
            jax.ShapeDtypeStruct((b, t, LANES), F32),
            jax.ShapeDtypeStruct((b, SUBLANES, 2 * mw), F32),
        ],
        scratch_shapes=[pltpu.VMEM((bt, SUBLANES + tt, 2 * mw), F32)],
        compiler_params=_params(("parallel", "arbitrary")),
        name="inproj",
    )(x, mod, mod, g_mix, w_main, w_gates, b_if, conv_w, conv_b, conv_init)
    return outs


def _bias_kernel(rev_ref, o_ref):
    w = rev_ref.shape[-1]
    x = jnp.broadcast_to(rev_ref[0, 0], (CHUNK, w))
    r = pltpu.roll(x, w - (CHUNK - 1), 1, stride=1, stride_axis=0)
    o_ref[0, 0] = r[:, 0:BAND] * LOG2E


def _bias_call(rel_bias):
    depth, nh, rel = rel_bias.shape
    n_ext = BAND + CHUNK - 1
    w = -(-n_ext // LANES) * LANES
    ext = jnp.concatenate([rel_bias, jnp.broadcast_to(rel_bias[:, :, rel - 1:], (depth, nh, n_ext - rel))], axis=2)
    rev = jnp.pad(ext[:, :, ::-1], ((0, 0), (0, 0), (0, w - n_ext))).reshape(depth, nh, 1, w)
    out = pl.pallas_call(
        _bias_kernel,
        grid=(depth, nh),
        in_specs=[pl.BlockSpec((1, 1, 1, w), lambda l, h: (l, h, 0, 0))],
        out_specs=pl.BlockSpec((1, 1, CHUNK, BAND), lambda l, h: (l, h, 0, 0)),
        out_shape=jax.ShapeDtypeStruct((depth, nh, CHUNK, BAND), F32),
        compiler_params=_params(("parallel", "parallel")),
        name="rel_bias",
    )(rev)
    return out.reshape(depth, N_PAIRS, 2 * CHUNK, BAND)


def _pair_rows(qp):
    lane = lax.broadcasted_iota(jnp.int32, qp.shape, 1)
    zero = jnp.zeros_like(qp)
    return jnp.concatenate([jnp.where(lane < ATT_HEAD_DIM, qp, zero),
                            jnp.where(lane >= ATT_HEAD_DIM, qp, zero)], axis=0)


def _pair_merge(o, rows):
    lane = lax.broadcasted_iota(jnp.int32, (rows, LANES), 1)
    return jnp.where(lane < ATT_HEAD_DIM, o[0:rows], o[rows:2 * rows])


def _nt_dot(a, b):
    return lax.dot_general(a, b, (((1,), (1,)), ((), ())), preferred_element_type=F32)


def _attn_prompt_kernel(q_ref, kp_ref, kc_ref, vp_ref, vc_ref, bias_ref, o_ref, kband, vband, *, unroll):
    j = pl.program_id(1)
    rows = kc_ref.shape[1]
    past = N_PAST_CHUNKS * CHUNK
    assert rows == past

    @pl.when(j == 0)
    def _():
        kband[0:past, :] = jnp.zeros((past, ATT_WIDTH), BF16)
        vband[0:past, :] = jnp.zeros((past, ATT_WIDTH), BF16)

    @pl.when(j > 0)
    def _():
        kband[0:past, :] = kp_ref[0]
        vband[0:past, :] = vp_ref[0]

    kband[past:past + rows, :] = kc_ref[0]
    vband[past:past + rows, :] = vc_ref[0]

    def chunks(i, masked):
        units = [(pl.multiple_of((i * unroll + u) * CHUNK, CHUNK), i * unroll + u, p)
                 for u in range(unroll) for p in range(N_PAIRS)]
        scores = []
        for r0, ci, p in units:
            ls = slice(p * LANES, (p + 1) * LANES)
            qs = _pair_rows(q_ref[0, pl.ds(r0, CHUNK), ls])
            scores.append(_nt_dot(qs, kband[pl.ds(r0, BAND), ls]) + bias_ref[0, p])
        for (r0, ci, p), s in zip(units, scores):
            ls = slice(p * LANES, (p + 1) * LANES)
            if masked:
                col = lax.broadcasted_iota(jnp.int32, s.shape, 1)
                s = jnp.where(col >= (N_PAST_CHUNKS - ci) * CHUNK, s, NEG)
            mx = jnp.max(s, axis=-1, keepdims=True)
            e = jnp.exp2(s - mx)
            den = jnp.sum(e, axis=-1, keepdims=True)
            o = jnp.dot(e.astype(BF16), vband[pl.ds(r0, BAND), ls], preferred_element_type=F32) / den
            o_ref[0, pl.ds(r0, CHUNK), ls] = _pair_merge(o, CHUNK).astype(BF16)

    n_iter = rows // (CHUNK * unroll)

    @pl.when(j == 0)
    def _():
        lax.fori_loop(0, n_iter, lambda i, c: (chunks(i, True), c)[1], 0)

    @pl.when(j > 0)
    def _():
        lax.fori_loop(0, n_iter, lambda i, c: (chunks(i, False), c)[1], 0)


def _attn_prompt_call(qkv, bias, layer):
    b, t, _ = qkv.shape
    rows = N_PAST_CHUNKS * CHUNK
    assert t % rows == 0
    aw = ATT_WIDTH

    def cur(cb):
        return pl.BlockSpec((1, rows, aw), lambda i, j: (i, j, cb))

    def prev(cb):
        return pl.BlockSpec((1, rows, aw), lambda i, j: (i, jnp.maximum(j - 1, 0), cb))

    return pl.pallas_call(
        functools.partial(_attn_prompt_kernel, unroll=ATTN_UNROLL),
        grid=(b, t // rows),
        in_specs=[cur(0), prev(1), cur(1), prev(2), cur(2),
                  _const_spec((1, N_PAIRS, 2 * CHUNK, BAND), lambda i, j: (layer, 0, 0, 0))],
        out_specs=pl.BlockSpec((1, rows, aw), lambda i, j: (i, j, 0)),
        out_shape=jax.ShapeDtypeStruct((b, t, aw), BF16),
        scratch_shapes=[pltpu.VMEM((2 * rows, aw), BF16), pltpu.VMEM((2 * rows, aw), BF16)],
        compiler_params=_params(("parallel", "parallel")),
        name="attn_prompt",
    )(qkv, qkv, qkv, qkv, qkv, bias)


def _attn_sample_kernel(q_ref, kn_ref, vn_ref, kc_ref, vc_ref, bias_ref, o_ref, *, bb):
    tq = q_ref.shape[1]
    pc = kc_ref.shape[2]
    for bi in range(bb):
        for p in range(N_PAIRS):
            ls = slice(p * LANES, (p + 1) * LANES)
            qs = _pair_rows(q_ref[bi, :, ls])
            bias = jnp.concatenate([bias_ref[0, p, 0:tq, :], bias_ref[0, p, CHUNK:CHUNK + tq, :]], axis=0)
            s1 = _nt_dot(qs, kc_ref[0, bi, :, ls].astype(BF16)) + bias[:, 0:pc]
            s2 = _nt_dot(qs, kn_ref[bi, :, ls]) + bias[:, pc:pc + tq]
            mx = jnp.maximum(jnp.max(s1, axis=-1, keepdims=True), jnp.max(s2, axis=-1, keepdims=True))
            e1 = jnp.exp2(s1 - mx)
            e2 = jnp.exp2(s2 - mx)
            den = jnp.sum(e1, axis=-1, keepdims=True) + jnp.sum(e2, axis=-1, keepdims=True)
            o = (jnp.dot(e1.astype(BF16), vc_ref[0, bi, :, ls].astype(BF16), preferred_element_type=F32)
                 + jnp.dot(e2.astype(BF16), vn_ref[bi, :, ls], preferred_element_type=F32)) / den
            o_ref[bi, :, ls] = _pair_merge(o, tq).astype(BF16)


def _attn_sample_call(qkv, cache_k, cache_v, bias, layer, bb):
    b, tq, _ = qkv.shape
    _, _, pc, aw = cache_k.shape
    assert b % bb == 0 and pc == N_PAST_CHUNKS * CHUNK and tq <= CHUNK and aw == ATT_WIDTH

    def new(cb):
        return pl.BlockSpec((bb, tq, aw), lambda i: (i, 0, cb))

    cache = pl.BlockSpec((1, bb, pc, aw), lambda i: (layer, i, 0, 0))
    return pl.pallas_call(
        functools.partial(_attn_sample_kernel, bb=bb),
        grid=(b // bb,),
        in_specs=[new(0), new(1), new(2), cache, cache,
                  _const_spec((1, N_PAIRS, 2 * CHUNK, BAND), lambda i: (layer, 0, 0, 0))],
        out_specs=pl.BlockSpec((bb, tq, aw), lambda i: (i, 0, 0)),
        out_shape=jax.ShapeDtypeStruct((b, tq, aw), BF16),
        compiler_params=_params(("parallel",)),
        name="attn_sample",
    )(qkv, qkv, qkv, cache_k, cache_v, bias)


def _transpose_rows(a):
    r = a.shape[0]
    pad = -r % LANES
    if pad:
        a = jnp.concatenate([a, jnp.zeros((pad, a.shape[1]), a.dtype)], axis=0)
    return a.T[:, 0:r]


def _transpose_cols(a):
    c = a.shape[1]
    pad = -c % LANES
    if pad:
        a = jnp.concatenate([a, jnp.zeros((a.shape[0], pad), a.dtype)], axis=1)
    return a.T[0:c, :]


def _block_cumsum(tri, g):
    g1 = g.astype(BF16)
    r1 = g - g1.astype(F32)
    g2 = r1.astype(BF16)
    g3 = (r1 - g2.astype(F32)).astype(BF16)
    return (jnp.dot(tri, g1, preferred_element_type=F32) + jnp.dot(tri, g2, preferred_element_type=F32)
            + jnp.dot(tri, g3, preferred_element_type=F32))


def _mlstm_kernel(qkm_ref, vm_ref, om_ref, gates_ref, gain_ref, tri_ref, c0_ref, n0_ref, m0_ref,
                  ym_ref, c1_ref, n1_ref, m1_ref, cn_scr, m_scr, *, blk, nblk):
    tb = pl.program_id(1)
    hd = M_HEAD_DIM
    ext = cn_scr.shape[1] - hd

    def last_lane(r):
        return jnp.broadcast_to(r[:, blk - 1:blk], r.shape)

    ext_row0 = lax.broadcasted_iota(jnp.int32, (ext, hd), 0) == 0

    @pl.when(tb == 0)
    def _():
        for h in range(M_HEADS):
            cn_scr[h, 0:hd, :] = c0_ref[0, h].T
            cn_scr[h, hd:hd + ext, :] = jnp.where(ext_row0, n0_ref[0, h], 0.0)
        m_scr[...] = m0_ref[0]

    srow = lax.broadcasted_iota(jnp.int32, (blk, blk), 0)
    tcol = lax.broadcasted_iota(jnp.int32, (blk, blk), 1)
    causal = srow <= tcol
    ones_ext = jnp.where(lax.broadcasted_iota(jnp.int32, (ext, blk), 0) == 0, 1.0, 0.0)

    g_all = gates_ref[0]
    b_all = _block_cumsum(tri_ref[...], g_all)
    gt_all = _transpose_rows(g_all)
    bt_all = _transpose_rows(b_all)
    c_all = g_all[:, 0:M_HEADS] - b_all[:, M_HEADS:2 * M_HEADS]

    units = [(h, bi) for h in range(M_HEADS) for bi in range(nblk)]
    vt_heads = [_transpose_rows(vm_ref[0, :, h * hd:(h + 1) * hd].astype(F32)) for h in range(M_HEADS)]
    scores = {}
    for h, bi in units:
        rs = slice(bi * blk, (bi + 1) * blk)
        q = qkm_ref[0, rs, h * hd:(h + 1) * hd]
        k = qkm_ref[0, rs, M_WIDTH + h * hd:M_WIDTH + (h + 1) * hd]
        b_row = bt_all[M_HEADS + h:M_HEADS + h + 1, rs]
        dmat = jnp.where(causal, b_row + c_all[rs, h:h + 1], NEG)
        m_in = jnp.max(dmat, axis=0, keepdims=True)
        scores[h, bi] = (m_in, _nt_dot(k, q) * jnp.exp(dmat - m_in))
    intra = {}
    for h, bi in units:
        rs = slice(bi * blk, (bi + 1) * blk)
        k = qkm_ref[0, rs, M_WIDTH + h * hd:M_WIDTH + (h + 1) * hd]
        li_row, b_row = gt_all[h:h + 1, rs], bt_all[M_HEADS + h:M_HEADS + h + 1, rs]
        m_in, smat = scores[h, bi]
        vt = jnp.concatenate([vt_heads[h][:, rs], ones_ext], axis=0)
        den_i = jnp.sum(smat, axis=0, keepdims=True)
        num_i = jnp.dot(vt[0:hd].astype(BF16), smat.astype(BF16), preferred_element_type=F32)
        b_last = last_lane(b_row)
        m_u = last_lane(m_in)
        w_u = jnp.exp(b_last - b_row + li_row - m_u)
        upd = jnp.dot((vt * w_u).astype(BF16), k, preferred_element_type=F32)
        intra[h, bi] = (den_i, num_i, b_last, m_u, upd)
    for bi in range(nblk):
        rs = slice(bi * blk, (bi + 1) * blk)
        for h in range(M_HEADS):
            ls = slice(h * hd, (h + 1) * hd)
            q = qkm_ref[0, rs, ls]
            b_row = bt_all[M_HEADS + h:M_HEADS + h + 1, rs]
            m_in = scores[h, bi][0]
            den_i, num_i, b_last, m_u, upd = intra[h, bi]
            m_prev = m_scr[h][:, 0:blk]
            cn_prev = cn_scr[h]
            a = b_row + m_prev
            m_t = jnp.maximum(a, m_in)
            alpha = jnp.exp(m_in - m_t)
            beta = jnp.exp(a - m_t)
            inter = _nt_dot(cn_prev.astype(BF16), q)
            num = alpha * num_i + beta * inter[0:hd]
            den = alpha * den_i + beta * inter[hd:hd + 1]
            hh = num / jnp.maximum(jnp.abs(den), jnp.exp(-m_t))
            hh = hh * lax.rsqrt(jnp.mean(hh * hh, axis=0, keepdims=True) + EPS)
            y = _transpose_cols(hh) * gain_ref[:, ls] * _sigmoid(om_ref[0, rs, ls])
            ym_ref[0, rs, ls] = y.astype(BF16)
            m_new = last_lane(m_t)
            g_state = jnp.exp(b_last + m_prev - m_new)[:, 0:1]
            g_upd = jnp.exp(m_u - m_new)[:, 0:1]
            cn_scr[h] = g_state * cn_prev + g_upd * upd
            m_scr[h] = jnp.broadcast_to(m_new[:, 0:1], (1, LANES))

    @pl.when(tb == pl.num_programs(1) - 1)
    def _():
        for h in range(M_HEADS):
            c1_ref[0, h] = cn_scr[h, 0:hd, :].T
            n1_ref[0, h] = cn_scr[h, hd:hd + 1, :]
        m1_ref[0] = m_scr[...]


def _block_tril(rt, blk):
    r = jnp.arange(rt)
    return ((r[:, None] // blk == r[None, :] // blk) & (r[None, :] <= r[:, None])).astype(BF16)


def _mlstm_call(qkm, vm, om, gates, gain, c0, n0, m0, blk, nblk):
    b, t, _ = qkm.shape
    rt = blk * nblk
    assert t % rt == 0
    mw, hd, nh = M_WIDTH, M_HEAD_DIM, M_HEADS

    def tok(width):
        return pl.BlockSpec((1, rt, width), lambda i, j: (i, j, 0))

    cspec = pl.BlockSpec((1, nh, hd, hd), lambda i, j: (i, 0, 0, 0))
    vspec = pl.BlockSpec((1, nh, 1, hd), lambda i, j: (i, 0, 0, 0))
    return pl.pallas_call(
        functools.partial(_mlstm_kernel, blk=blk, nblk=nblk),
        grid=(b, t // rt),
        in_specs=[tok(2 * mw), tok(mw), tok(mw), tok(LANES),
                  _const_spec((1, mw), lambda i, j: (0, 0)), _const_spec((rt, rt), lambda i, j: (0, 0)),
                  cspec, vspec, vspec],
        out_specs=[tok(mw), cspec, vspec, vspec],
        out_shape=[
            jax.ShapeDtypeStruct((b, t, mw), BF16),
            jax.ShapeDtypeStruct((b, nh, hd, hd), F32),
            jax.ShapeDtypeStruct((b, nh, 1, hd), F32),
            jax.ShapeDtypeStruct((b, nh, 1, hd), F32),
        ],
        scratch_shapes=[pltpu.VMEM((nh, hd + STATE_EXT_ROWS, hd), F32), pltpu.VMEM((nh, 1, hd), F32)],
        compiler_params=_params(("parallel", "arbitrary")),
        name="mlstm",
    )(qkm, vm, om, gates, gain, _block_tril(rt, blk), c0, n0, m0)


def _ffn_kernel(x_ref, ya_ref, ym_ref, gm_ref, gt1_ref, sh2_ref, sc2_ref, gt2_ref, gffn_ref, gfin_ref,
                wba_ref, wbm_ref, wo_ref, wgu_ref, wd_ref, o_ref, act_scr, *, bt, tt, final):
    d = x_ref.shape[-1]
    m = bt * tt
    dff = wd_ref.shape[0]
    x = x_ref[...]
    pa = jnp.dot(ya_ref[...].reshape(m, ATT_WIDTH), wba_ref[...], preferred_element_type=F32)
    pm = jnp.dot(ym_ref[...].reshape(m, M_WIDTH), wbm_ref[...], preferred_element_type=F32)
    ga = gm_ref[:, :, 0:d].reshape(m, d)
    gmm = gm_ref[:, :, d:2 * d].reshape(m, d)
    merged = _sigmoid(ga) * pa + _sigmoid(gmm) * pm
    y1 = jnp.dot(merged.astype(BF16), wo_ref[...], preferred_element_type=F32)
    x1 = x + gt1_ref[0, :, 0] * y1.reshape(bt, tt, d)
    ms = jnp.mean(x1 * x1, axis=-1, keepdims=True)
    h2 = x1 * lax.rsqrt(ms + EPS) * gffn_ref[...]
    h2 = (h2 * (1.0 + sc2_ref[0, :, 0]) + sh2_ref[0, :, 0]).reshape(m, d).astype(BF16)
    cw = 256
    for c in range(0, dff, cw):
        g = jnp.dot(h2, wgu_ref[:, c:c + cw], preferred_element_type=F32)
        up = jnp.dot(h2, wgu_ref[:, dff + c:dff + c + cw], preferred_element_type=F32)
        act_scr[:, c:c + cw] = (g * _sigmoid(g) * up).astype(BF16)
    y2 = jnp.dot(act_scr[...], wd_ref[...], preferred_element_type=F32)
    x2 = x1 + gt2_ref[0, :, 0] * y2.reshape(bt, tt, d)
    if final:
        ms2 = jnp.mean(x2 * x2, axis=-1, keepdims=True)
        x2 = x2 * lax.rsqrt(ms2 + EPS) * gfin_ref[...]
    o_ref[...] = x2


def _ffn_call(x, ya, ym, gm, mod, layer, row0, g_ffn, g_final, w_br_att, w_br_mlstm, w_out, w_gate_up,
              w_down, bt, tt, final):
    b, t, d = x.shape
    dff = w_down.shape[0]
    assert b % bt == 0 and t % tt == 0 and dff % 256 == 0

    def tok(width):
        return pl.BlockSpec((bt, tt, width), lambda i, j: (i, j, 0))

    def modspec(k):
        return pl.BlockSpec((1, bt, 1, 1, d), lambda i, j: (layer, row0 // bt + i, k, 0, 0))

    def const2(shape):
        return _const_spec(shape, lambda i, j: (0, 0))

    return pl.pallas_call(
        functools.partial(_ffn_kernel, bt=bt, tt=tt, final=final),
        grid=(b // bt, t // tt),
        in_specs=[tok(d), tok(ATT_WIDTH), tok(M_WIDTH), tok(2 * d),
                  modspec(2), modspec(3), modspec(4), modspec(5), const2((1, d)), const2((1, d)),
                  const2((ATT_WIDTH, d)), const2((M_WIDTH, d)), const2((d, d)), const2((d, 2 * dff)),
                  const2((dff, d))],
        out_specs=tok(d),
        out_shape=jax.ShapeDtypeStruct((b, t, d), F32),
        scratch_shapes=[pltpu.VMEM((bt * tt, dff), BF16)],
        compiler_params=_params(("parallel", "parallel")),
        name="merge_ffn",
    )(x, ya, ym, gm, mod, mod, mod, mod, g_ffn, g_final, w_br_att, w_br_mlstm, w_out, w_gate_up, w_down)


def _layer(x, mod, layer, row0, cache, conv_init, c0, n0, m0, wts, bt, tt, blk, nblk, final):
    (g_mix, w_main, w_gates, b_if, conv_w, conv_b, bias, mh_gain, w_br_att, w_br_mlstm, w_out, g_ffn,
     w_gate_up, w_down, g_final) = wts
    b, t, _ = x.shape
    qkv, kt, vt, qkm, vm, om, gm, gates, ctail = _inproj_call(
        x, mod, layer, row0, g_mix, w_main, w_gates, b_if, conv_w, conv_b, conv_init, bt, tt)
    if cache is None:
        ya = _attn_prompt_call(qkv, bias, layer)
    else:
        ya = _attn_sample_call(qkv, cache[0], cache[1], bias, layer, bt)
    ym, c1, n1, m1 = _mlstm_call(qkm, vm, om, gates, mh_gain, c0, n0, m0, blk, nblk)
    x = _ffn_call(x, ya, ym, gm, mod, layer, row0, g_ffn, g_final, w_br_att, w_br_mlstm, w_out, w_gate_up,
                  w_down, bt, tt, final)
    keep = kt.shape[1]
    state = (kt.reshape(b, keep, ATT_HEADS, ATT_HEAD_DIM), vt.reshape(b, keep, ATT_HEADS, ATT_HEAD_DIM),
             ctail[:, SUBLANES - (CONV_W - 1):, :], c1, n1[:, :, 0, :], m1[:, :, 0, 0])
    return x, state


def kernel(x_prompt, x_sample, cache_k, cache_v, state_conv, state_C, state_n, state_m, c_prompt, c_sample,
           w_ada, b_ada, g_mix, w_in, b_if, conv_w, conv_b, rel_bias, mh_gain, w_br_att, w_br_mlstm, w_out,
           g_ffn, w_gate_up, w_down, g_final):
    depth = w_ada.shape[0]
    bp, tp, d = x_prompt.shape
    bs, ts, _ = x_sample.shape
    assert tp % PROMPT_ROWS == 0 and tp % CHUNK == 0 and bs % SAMPLE_BATCH_TILE == 0 and ts % CHUNK != 0
    mw = M_WIDTH
    n_att = 3 * ATT_WIDTH + 4 * M_WIDTH

    mod = _ada_call(jnp.concatenate([c_sample, c_prompt], axis=0), w_ada, b_ada)
    mod = mod.reshape(depth, bs + bp, 6, 1, d)

    bias = _bias_call(rel_bias)
    pc = cache_k.shape[2]
    caches = (cache_k.reshape(depth, bs, pc, ATT_WIDTH), cache_v.reshape(depth, bs, pc, ATT_WIDTH))

    pad3 = SUBLANES - (CONV_W - 1)
    zero_conv = jnp.zeros((bp, SUBLANES, 2 * mw), F32)
    zero_c = jnp.zeros((bp, M_HEADS, M_HEAD_DIM, M_HEAD_DIM), F32)
    zero_v = jnp.zeros((bp, M_HEADS, 1, M_HEAD_DIM), F32)

    xp, xs = x_prompt, x_sample
    outs_p, outs_s = [], []
    for l in range(depth):
        w_main = jnp.concatenate([w_in[l, :, :n_att], w_in[l, :, n_att + 2 * M_HEADS:]], axis=1).astype(BF16)
        w_gates = jnp.pad(w_in[l, :, n_att:n_att + 2 * M_HEADS], ((0, 0), (0, LANES - 2 * M_HEADS))).astype(BF16)
        bif = jnp.pad(b_if[l], (0, LANES - 2 * M_HEADS)).reshape(1, LANES)
        wts = (g_mix[l].reshape(1, d), w_main, w_gates, bif, conv_w[l], conv_b[l].reshape(1, 2 * mw),
               bias, mh_gain[l].reshape(1, mw), w_br_att[l].astype(BF16), w_br_mlstm[l].astype(BF16),
               w_out[l].astype(BF16), g_ffn[l].reshape(1, d), w_gate_up[l].astype(BF16),
               w_down[l].astype(BF16), g_final.reshape(1, d))
        final = l == depth - 1
        xp, st_p = _layer(xp, mod, l, bs, None, zero_conv, zero_c, zero_v, zero_v, wts,
                          1, PROMPT_ROWS, MLSTM_BLOCK, PROMPT_ROWS // MLSTM_BLOCK, final)
        conv_init = jnp.pad(state_conv[l], ((0, 0), (pad3, 0), (0, 0)))
        m0 = jnp.broadcast_to(state_m[l][:, :, None, None], (bs, M_HEADS, 1, M_HEAD_DIM))
        xs, st_s = _layer(xs, mod, l, 0, caches, conv_init, state_C[l],
                          state_n[l][:, :, None, :], m0, wts, SAMPLE_BATCH_TILE, ts, ts, 1, final)
        outs_p.append(st_p)
        outs_s.append(st_s)

    def stk(outs, i):
        return jnp.stack([o[i] for o in outs], axis=0)

    return (xp, xs,
            stk(outs_p, 0), stk(outs_p, 1), stk(outs_p, 2), stk(outs_p, 3), stk(outs_p, 4), stk(outs_p, 5),
            stk(outs_s, 0), stk(outs_s, 1), stk(outs_s, 2), stk(outs_s, 3), stk(outs_s, 4), stk(outs_s, 5))
```

```python
import functools

import jax
import jax.numpy as jnp
from jax import lax
from jax.experimental import pallas as pl
from jax.experimental.pallas import tpu as pltpu

F32 = jnp.float32
BF16 = jnp.bfloat16

CHUNK = 64
N_PAST_CHUNKS = 8
ATT_HEADS = 8
ATT_HEAD_DIM = 64
ATT_WIDTH = ATT_HEADS * ATT_HEAD_DIM
MAX_REL = 256
M_HEADS = 4
M_HEAD_DIM = 128
M_WIDTH = M_HEADS * M_HEAD_DIM
CONV_W = 4
EPS = 1e-6
NEG = -1e30

LANES = 128
SUBLANES = 8
VMEM_LIMIT = 56 * 1024 * 1024
PROMPT_ROWS = 512
SAMPLE_BATCH_TILE = 8
SAMPLE_MLSTM_BATCH_TILE = 4
BAND = (N_PAST_CHUNKS + 1) * CHUNK
N_PAIRS = ATT_HEADS // 2
ATTN_UNROLL = 2
MLSTM_BLOCK = 128
STATE_EXT_ROWS = 16
LOG2E = 1.4426950408889634


def _sigmoid(x):
    return 1.0 / (1.0 + jnp.exp2(x * (-LOG2E)))


def _log_sigmoid(x):
    return jnp.minimum(x, 0.0) - jnp.log1p(jnp.exp(-jnp.abs(x)))


def _const_spec(shape, index_map):
    return pl.BlockSpec(shape, index_map, pipeline_mode=pl.Buffered(1))


def _params(sem):
    return pltpu.CompilerParams(dimension_semantics=sem, vmem_limit_bytes=VMEM_LIMIT)


def _ada_kernel(c_ref, w_ref, b_ref, o_ref):
    c = c_ref[...]
    a = (c * _sigmoid(c)).astype(BF16)
    o_ref[0] = jnp.dot(a, w_ref[0].astype(BF16), preferred_element_type=F32) + b_ref[0]


def _ada_call(c_all, w_ada, b_ada):
    depth, d, n = w_ada.shape
    r = c_all.shape[0]
    tn = 1536
    assert n % tn == 0
    return pl.pallas_call(
        _ada_kernel,
        grid=(depth, n // tn),
        in_specs=[
            pl.BlockSpec((r, d), lambda l, j: (0, 0)),
            pl.BlockSpec((1, d, tn), lambda l, j: (l, 0, j)),
            pl.BlockSpec((1, 1, tn), lambda l, j: (l, 0, j)),
        ],
        out_specs=pl.BlockSpec((1, r, tn), lambda l, j: (l, 0, j)),
        out_shape=jax.ShapeDtypeStruct((depth, r, n), F32),
        compiler_params=_params(("parallel", "parallel")),
        name="ada_mod",
    )(c_all, w_ada, b_ada.reshape(depth, 1, n))


def _wprep_kernel(w_ref, main_ref, gates_ref):
    n_att = 3 * ATT_WIDTH + 4 * M_WIDTH
    n_g = 2 * M_HEADS
    w = w_ref[0]
    main_ref[0, :, 0:n_att] = w[:, 0:n_att].astype(BF16)
    main_ref[0, :, n_att:] = w[:, n_att + n_g:].astype(BF16)
    slab = w[:, n_att:n_att + LANES]
    lane = lax.broadcasted_iota(jnp.int32, slab.shape, 1)
    gates_ref[0] = jnp.where(lane < n_g, slab, 0.0).astype(BF16)


def _wprep_call(w_in):
    depth, d, n = w_in.shape
    n_main = n - 2 * M_HEADS
    tr = 256
    assert d % tr == 0 and n_main % LANES == 0
    return pl.pallas_call(
        _wprep_kernel,
        grid=(depth, d // tr),
        in_specs=[pl.BlockSpec((1, tr, n), lambda l, i: (l, i, 0))],
        out_specs=[pl.BlockSpec((1, tr, n_main), lambda l, i: (l, i, 0)),
                   pl.BlockSpec((1, tr, LANES), lambda l, i: (l, i, 0))],
        out_shape=[jax.ShapeDtypeStruct((depth, d, n_main), BF16), jax.ShapeDtypeStruct((depth, d, LANES), BF16)],
        compiler_params=_params(("parallel", "parallel")),
        name="w_in_prep",
    )(w_in)


def _inproj_kernel(x_ref, sh_ref, sc_ref, g_ref, w_ref, wg_ref, bif_ref, cw_ref, cb_ref, cinit_ref,
                   qkv_ref, kt_ref, vt_ref, qkm_ref, vm_ref, om_ref, gm_ref, gates_ref, ctail_ref,
                   conv_scr, *, bt, tt):
    t = pl.program_id(1)
    d = x_ref.shape[-1]
    m = bt * tt
    x = x_ref[...]
    ms = jnp.mean(x * x, axis=-1, keepdims=True)
    h = x * lax.rsqrt(ms + EPS) * g_ref[...]
    h = h * (1.0 + sc_ref[0, :, 0]) + sh_ref[0, :, 0]
    hb = h.reshape(m, d).astype(BF16)

    def proj(c0, width):
        return jnp.dot(hb, w_ref[0, :, c0:c0 + width], preferred_element_type=F32)

    aw = ATT_WIDTH
    mw = M_WIDTH
    c0 = 3 * aw

    @pl.when(t == 0)
    def _():
        conv_scr[:, 0:SUBLANES, :] = cinit_ref[...]

    conv_scr[:, SUBLANES:SUBLANES + tt, :] = proj(c0, 2 * mw).reshape(bt, tt, 2 * mw)
    lo = SUBLANES - (CONV_W - 1)
    acc = conv_scr[:, lo:lo + tt, :] * cw_ref[0:1, :]
    for j in range(1, CONV_W):
        acc = acc + conv_scr[:, lo + j:lo + j + tt, :] * cw_ref[j:j + 1, :]
    acc = acc + cb_ref[...]
    act = acc * _sigmoid(acc)
    qkm_ref[:, :, 0:mw] = act[:, :, 0:mw].astype(BF16)
    qkm_ref[:, :, mw:2 * mw] = (act[:, :, mw:2 * mw] * (M_HEAD_DIM ** -0.5)).astype(BF16)
    tail = conv_scr[:, tt:tt + SUBLANES, :]
    ctail_ref[...] = tail
    conv_scr[:, 0:SUBLANES, :] = tail
    u = proj(0, 3 * aw)
    qkv_ref[:, :, 0:aw] = (u[:, 0:aw] * (LOG2E * ATT_HEAD_DIM ** -0.5)).astype(BF16).reshape(bt, tt, aw)
    qkv_ref[:, :, aw:3 * aw] = u[:, aw:3 * aw].astype(BF16).reshape(bt, tt, 2 * aw)
    kt_ref[...] = u[:, aw:2 * aw].reshape(bt, tt, aw)
    vt_ref[...] = u[:, 2 * aw:3 * aw].reshape(bt, tt, aw)
    c0 += 2 * mw
    u = proj(c0, 2 * mw)
    vm_ref[...] = u[:, 0:mw].astype(BF16).reshape(bt, tt, mw)
    om_ref[...] = u[:, mw:2 * mw].reshape(bt, tt, mw)
    c0 += 2 * mw
    for j in range(0, 2 * d, d):
        gm_ref[:, :, j:j + d] = proj(c0 + j, d).reshape(bt, tt, d)
    ug = jnp.dot(hb, wg_ref[0], preferred_element_type=F32) + bif_ref[...]
    col = lax.broadcasted_iota(jnp.int32, ug.shape, 1)
    gates_ref[...] = jnp.where(col >= M_HEADS, _log_sigmoid(ug), ug).reshape(bt, tt, LANES)


def _inproj_call(x, mod, layer, row0, g_mix, w_main, w_gates, b_if, conv_w, conv_b, conv_init, bt, tt):
    b, t, d = x.shape
    nb, nt = b // bt, t // tt
    keep = min(N_PAST_CHUNKS * CHUNK, t)
    assert b % bt == 0 and t % tt == 0 and keep % tt == 0
    t_keep0 = nt - keep // tt
    n_main = w_main.shape[2]
    aw, mw = ATT_WIDTH, M_WIDTH

    def wspec(shape):
        return _const_spec((1,) + shape, lambda i, j: (layer, 0, 0))

    def tok(width):
        return pl.BlockSpec((bt, tt, width), lambda i, j: (i, j, 0))

    def modspec(k):
        return pl.BlockSpec((1, bt, 1, 1, d), lambda i, j: (layer, row0 // bt + i, k, 0, 0))

    def const2(shape):
        return _const_spec(shape, lambda i, j: (0, 0))

    keep_spec = pl.BlockSpec((bt, tt, aw), lambda i, j: (i, jnp.maximum(j - t_keep0, 0), 0))
    per_batch8 = pl.BlockSpec((bt, SUBLANES, 2 * mw), lambda i, j: (i, 0, 0))
    outs = pl.pallas_call(
        functools.partial(_inproj_kernel, bt=bt, tt=tt),
        grid=(nb, nt),
        in_specs=[
            tok(d), modspec(0), modspec(1), const2((1, d)),
            wspec((d, n_main)), wspec((d, LANES)), const2((1, LANES)),
            const2((CONV_W, 2 * mw)), const2((1, 2 * mw)), per_batch8,
        ],
        out_specs=[tok(3 * aw), keep_spec, keep_spec, tok(2 * mw), tok(mw), tok(mw), tok(2 * d),
                   tok(LANES), per_batch8],
        out_shape=[
            jax.ShapeDtypeStruct((b, t, 3 * aw), BF16),
            jax.ShapeDtypeStruct((b, keep, aw), F32),
            jax.ShapeDtypeStruct((b, keep, aw), F32),
            jax.ShapeDtypeStruct((b, t, 2 * mw), BF16),
            jax.ShapeDtypeStruct((b, t, mw), BF16),
            jax.ShapeDtypeStruct((b, t, mw), F32),
            jax.ShapeDtypeStruct((b, t, 2 * d), F32),
            jax.ShapeDtypeStruct((b, t, LANES), F32),
            jax.ShapeDtypeStruct((b, SUBLANES, 2 * mw), F32),
        ],
        scratch_shapes=[pltpu.VMEM((bt, SUBLANES + tt, 2 * mw), F32)],
        compiler_params=_params(("parallel", "arbitrary")),
        name="inproj",
    )(x, mod, mod, g_mix, w_main, w_gates, b_if, conv_w, conv_b, conv_init)
    return outs


def _bias_kernel(rev_ref, o_ref):
    w = rev_ref.shape[-1]
    x = jnp.broadcast_to(rev_ref[0, 0], (CHUNK, w))
    r = pltpu.roll(x, w - (CHUNK - 1), 1, stride=1, stride_axis=0)
    o_ref[0, 0] = r[:, 0:BAND] * LOG2E


def _bias_call(rel_bias):
    depth, nh, rel = rel_bias.shape
    n_ext = BAND + CHUNK - 1
    w = -(-n_ext // LANES) * LANES
    ext = jnp.concatenate([rel_bias, jnp.broadcast_to(rel_bias[:, :, rel - 1:], (depth, nh, n_ext - rel))], axis=2)
    rev = jnp.pad(ext[:, :, ::-1], ((0, 0), (0, 0), (0, w - n_ext))).reshape(depth, nh, 1, w)
    out = pl.pallas_call(
        _bias_kernel,
        grid=(depth, nh),
        in_specs=[pl.BlockSpec((1, 1, 1, w), lambda l, h: (l, h, 0, 0))],
        out_specs=pl.BlockSpec((1, 1, CHUNK, BAND), lambda l, h: (l, h, 0, 0)),
        out_shape=jax.ShapeDtypeStruct((depth, nh, CHUNK, BAND), F32),
        compiler_params=_params(("parallel", "parallel")),
        name="rel_bias",
    )(rev)
    return out.reshape(depth, N_PAIRS, 2 * CHUNK, BAND)


def _pair_rows(qp):
    lane = lax.broadcasted_iota(jnp.int32, qp.shape, 1)
    zero = jnp.zeros_like(qp)
    return jnp.concatenate([jnp.where(lane < ATT_HEAD_DIM, qp, zero),
                            jnp.where(lane >= ATT_HEAD_DIM, qp, zero)], axis=0)


def _pair_merge(o, rows):
    lane = lax.broadcasted_iota(jnp.int32, (rows, LANES), 1)
    return jnp.where(lane < ATT_HEAD_DIM, o[0:rows], o[rows:2 * rows])


def _nt_dot(a, b):
    return lax.dot_general(a, b, (((1,), (1,)), ((), ())), preferred_element_type=F32)


def _attn_prompt_kernel(q_ref, kp_ref, kc_ref, vp_ref, vc_ref, bias_ref, o_ref, kband, vband, *, unroll):
    j = pl.program_id(1)
    rows = kc_ref.shape[1]
    past = N_PAST_CHUNKS * CHUNK
    assert rows == past

    @pl.when(j == 0)
    def _():
        kband[0:past, :] = jnp.zeros((past, ATT_WIDTH), BF16)
        vband[0:past, :] = jnp.zeros((past, ATT_WIDTH), BF16)

    @pl.when(j > 0)
    def _():
        kband[0:past, :] = kp_ref[0]
        vband[0:past, :] = vp_ref[0]

    kband[past:past + rows, :] = kc_ref[0]
    vband[past:past + rows, :] = vc_ref[0]

    def chunks(i, masked):
        units = [(pl.multiple_of((i * unroll + u) * CHUNK, CHUNK), i * unroll + u, p)
                 for u in range(unroll) for p in range(N_PAIRS)]
        scores = []
        for r0, ci, p in units:
            ls = slice(p * LANES, (p + 1) * LANES)
            qs = _pair_rows(q_ref[0, pl.ds(r0, CHUNK), ls])
            scores.append(_nt_dot(qs, kband[pl.ds(r0, BAND), ls]) + bias_ref[0, p])
        for (r0, ci, p), s in zip(units, scores):
            ls = slice(p * LANES, (p + 1) * LANES)
            if masked:
                col = lax.broadcasted_iota(jnp.int32, s.shape, 1)
                s = jnp.where(col >= (N_PAST_CHUNKS - ci) * CHUNK, s, NEG)
            mx = jnp.max(s, axis=-1, keepdims=True)
            e = jnp.exp2(s - mx)
            den = jnp.sum(e, axis=-1, keepdims=True)
            o = jnp.dot(e.astype(BF16), vband[pl.ds(r0, BAND), ls], preferred_element_type=F32) / den
            o_ref[0, pl.ds(r0, CHUNK), ls] = _pair_merge(o, CHUNK).astype(BF16)

    n_iter = rows // (CHUNK * unroll)

    @pl.when(j == 0)
    def _():
        lax.fori_loop(0, n_iter, lambda i, c: (chunks(i, True), c)[1], 0)

    @pl.when(j > 0)
    def _():
        lax.fori_loop(0, n_iter, lambda i, c: (chunks(i, False), c)[1], 0)


def _attn_prompt_call(qkv, bias, layer):
    b, t, _ = qkv.shape
    rows = N_PAST_CHUNKS * CHUNK
    assert t % rows == 0
    aw = ATT_WIDTH

    def cur(cb):
        return pl.BlockSpec((1, rows, aw), lambda i, j: (i, j, cb))

    def prev(cb):
        return pl.BlockSpec((1, rows, aw), lambda i, j: (i, jnp.maximum(j - 1, 0), cb))

    return pl.pallas_call(
        functools.partial(_attn_prompt_kernel, unroll=ATTN_UNROLL),
        grid=(b, t // rows),
        in_specs=[cur(0), prev(1), cur(1), prev(2), cur(2),
                  _const_spec((1, N_PAIRS, 2 * CHUNK, BAND), lambda i, j: (layer, 0, 0, 0))],
        out_specs=pl.BlockSpec((1, rows, aw), lambda i, j: (i, j, 0)),
        out_shape=jax.ShapeDtypeStruct((b, t, aw), BF16),
        scratch_shapes=[pltpu.VMEM((2 * rows, aw), BF16), pltpu.VMEM((2 * rows, aw), BF16)],
        compiler_params=_params(("parallel", "parallel")),
        name="attn_prompt",
    )(qkv, qkv, qkv, qkv, qkv, bias)


def _attn_sample_kernel(q_ref, kn_ref, vn_ref, kc_ref, vc_ref, bias_ref, o_ref, *, bb):
    tq = q_ref.shape[1]
    pc = kc_ref.shape[2]
    for bi in range(bb):
        for p in range(N_PAIRS):
            ls = slice(p * LANES, (p + 1) * LANES)
            qs = _pair_rows(q_ref[bi, :, ls])
            bias = jnp.concatenate([bias_ref[0, p, 0:tq, :], bias_ref[0, p, CHUNK:CHUNK + tq, :]], axis=0)
            s1 = _nt_dot(qs, kc_ref[0, bi, :, ls]) + bias[:, 0:pc]
            s2 = _nt_dot(qs, kn_ref[bi, :, ls]) + bias[:, pc:pc + tq]
            mx = jnp.maximum(jnp.max(s1, axis=-1, keepdims=True), jnp.max(s2, axis=-1, keepdims=True))
            e1 = jnp.exp2(s1 - mx)
            e2 = jnp.exp2(s2 - mx)
            den = jnp.sum(e1, axis=-1, keepdims=True) + jnp.sum(e2, axis=-1, keepdims=True)
            o = (jnp.dot(e1.astype(BF16), vc_ref[0, bi, :, ls], preferred_element_type=F32)
                 + jnp.dot(e2.astype(BF16), vn_ref[bi, :, ls], preferred_element_type=F32)) / den
            o_ref[bi, :, ls] = _pair_merge(o, tq).astype(BF16)


def _attn_sample_call(qkv, cache_k, cache_v, bias, layer, bb):
    b, tq, _ = qkv.shape
    _, _, pc, aw = cache_k.shape
    assert b % bb == 0 and pc == N_PAST_CHUNKS * CHUNK and tq <= CHUNK and aw == ATT_WIDTH

    def new(cb):
        return pl.BlockSpec((bb, tq, aw), lambda i: (i, 0, cb))

    cache = pl.BlockSpec((1, bb, pc, aw), lambda i: (layer, i, 0, 0))
    return pl.pallas_call(
        functools.partial(_attn_sample_kernel, bb=bb),
        grid=(b // bb,),
        in_specs=[new(0), new(1), new(2), cache, cache,
                  _const_spec((1, N_PAIRS, 2 * CHUNK, BAND), lambda i: (layer, 0, 0, 0))],
        out_specs=pl.BlockSpec((bb, tq, aw), lambda i: (i, 0, 0)),
        out_shape=jax.ShapeDtypeStruct((b, tq, aw), BF16),
        compiler_params=_params(("parallel",)),
        name="attn_sample",
    )(qkv, qkv, qkv, cache_k, cache_v, bias)


def _transpose_rows(a):
    r = a.shape[0]
    pad = -r % LANES
    if pad:
        a = jnp.concatenate([a, jnp.zeros((pad, a.shape[1]), a.dtype)], axis=0)
    return a.T[:, 0:r]


def _transpose_cols(a):
    c = a.shape[1]
    pad = -c % LANES
    if pad:
        a = jnp.concatenate([a, jnp.zeros((a.shape[0], pad), a.dtype)], axis=1)
    return a.T[0:c, :]


def _block_cumsum(tri, g):
    g1 = g.astype(BF16)
    r1 = g - g1.astype(F32)
    g2 = r1.astype(BF16)
    g3 = (r1 - g2.astype(F32)).astype(BF16)
    return (jnp.dot(tri, g1, preferred_element_type=F32) + jnp.dot(tri, g2, preferred_element_type=F32)
            + jnp.dot(tri, g3, preferred_element_type=F32))


def _mlstm_kernel(qkm_ref, vm_ref, om_ref, gates_ref, gain_ref, tri_ref, c0_ref, n0_ref, m0_ref,
                  ym_ref, c1_ref, n1_ref, m1_ref, cn_scr, m_scr, *, bb, blk, nblk):
    tb = pl.program_id(1)
    hd = M_HEAD_DIM
    ext = cn_scr.shape[2] - hd

    def last_lane(r):
        return jnp.broadcast_to(r[:, blk - 1:blk], r.shape)

    ext_row0 = lax.broadcasted_iota(jnp.int32, (ext, hd), 0) == 0

    @pl.when(tb == 0)
    def _():
        for e in range(bb):
            for h in range(M_HEADS):
                cn_scr[e, h, 0:hd, :] = c0_ref[e, h].T
                cn_scr[e, h, hd:hd + ext, :] = jnp.where(ext_row0, n0_ref[e, h], 0.0)
        m_scr[...] = m0_ref[...]

    srow = lax.broadcasted_iota(jnp.int32, (blk, blk), 0)
    tcol = lax.broadcasted_iota(jnp.int32, (blk, blk), 1)
    causal = srow <= tcol
    ones_ext = jnp.where(lax.broadcasted_iota(jnp.int32, (ext, blk), 0) == 0, 1.0, 0.0)

    gt_all, bt_all, c_all = [], [], []
    for e in range(bb):
        g = gates_ref[e]
        b = _block_cumsum(tri_ref[...], g)
        gt_all.append(_transpose_rows(g))
        bt_all.append(_transpose_rows(b))
        c_all.append(g[:, 0:M_HEADS] - b[:, M_HEADS:2 * M_HEADS])

    units = [(e, h, bi) for e in range(bb) for h in range(M_HEADS) for bi in range(nblk)]
    vt_heads = {(e, h): _transpose_rows(vm_ref[e, :, h * hd:(h + 1) * hd].astype(F32))
                for e in range(bb) for h in range(M_HEADS)}
    scores = {}
    for e, h, bi in units:
        rs = slice(bi * blk, (bi + 1) * blk)
        q = qkm_ref[e, rs, h * hd:(h + 1) * hd]
        k = qkm_ref[e, rs, M_WIDTH + h * hd:M_WIDTH + (h + 1) * hd]
        b_row = bt_all[e][M_HEADS + h:M_HEADS + h + 1, rs]
        dmat = jnp.where(causal, b_row + c_all[e][rs, h:h + 1], NEG)
        m_in = jnp.max(dmat, axis=0, keepdims=True)
        scores[e, h, bi] = (m_in, _nt_dot(k, q) * jnp.exp(dmat - m_in))
    intra = {}
    for e, h, bi in units:
        rs = slice(bi * blk, (bi + 1) * blk)
        k = qkm_ref[e, rs, M_WIDTH + h * hd:M_WIDTH + (h + 1) * hd]
        li_row, b_row = gt_all[e][h:h + 1, rs], bt_all[e][M_HEADS + h:M_HEADS + h + 1, rs]
        m_in, smat = scores[e, h, bi]
        vt = jnp.concatenate([vt_heads[e, h][:, rs], ones_ext], axis=0)
        den_i = jnp.sum(smat, axis=0, keepdims=True)
        num_i = jnp.dot(vt[0:hd].astype(BF16), smat.astype(BF16), preferred_element_type=F32)
        b_last = last_lane(b_row)
        m_u = last_lane(m_in)
        w_u = jnp.exp(b_last - b_row + li_row - m_u)
        upd = jnp.dot((vt * w_u).astype(BF16), k, preferred_element_type=F32)
        intra[e, h, bi] = (den_i, num_i, b_last, m_u, upd)
    for bi in range(nblk):
        rs = slice(bi * blk, (bi + 1) * blk)
        for e, h in [(e, h) for e in range(bb) for h in range(M_HEADS)]:
            ls = slice(h * hd, (h + 1) * hd)
            q = qkm_ref[e, rs, ls]
            b_row = bt_all[e][M_HEADS + h:M_HEADS + h + 1, rs]
            m_in = scores[e, h, bi][0]
            den_i, num_i, b_last, m_u, upd = intra[e, h, bi]
            m_prev = m_scr[e, h][:, 0:blk]
            cn_prev = cn_scr[e, h]
            a = b_row + m_prev
            m_t = jnp.maximum(a, m_in)
            alpha = jnp.exp(m_in - m_t)
            beta = jnp.exp(a - m_t)
            inter = _nt_dot(cn_prev.astype(BF16), q)
            num = alpha * num_i + beta * inter[0:hd]
            den = alpha * den_i + beta * inter[hd:hd + 1]
            hh = num / jnp.maximum(jnp.abs(den), jnp.exp(-m_t))
            hh = hh * lax.rsqrt(jnp.mean(hh * hh, axis=0, keepdims=True) + EPS)
            y = _transpose_cols(hh) * gain_ref[:, ls] * _sigmoid(om_ref[e, rs, ls])
            ym_ref[e, rs, ls] = y.astype(BF16)
            m_new = last_lane(m_t)
            g_state = jnp.exp(b_last + m_prev - m_new)[:, 0:1]
            g_upd = jnp.exp(m_u - m_new)[:, 0:1]
            cn_scr[e, h] = g_state * cn_prev + g_upd * upd
            m_scr[e, h] = jnp.broadcast_to(m_new[:, 0:1], (1, LANES))

    @pl.when(tb == pl.num_programs(1) - 1)
    def _():
        for e in range(bb):
            for h in range(M_HEADS):
                c1_ref[e, h] = cn_scr[e, h, 0:hd, :].T
                n1_ref[e, h] = cn_scr[e, h, hd:hd + 1, :]
        m1_ref[...] = m_scr[...]


def _block_tril(rt, blk):
    r = jnp.arange(rt)
    return ((r[:, None] // blk == r[None, :] // blk) & (r[None, :] <= r[:, None])).astype(BF16)


def _mlstm_call(qkm, vm, om, gates, gain, c0, n0, m0, bb, blk, nblk):
    b, t, _ = qkm.shape
    rt = blk * nblk
    assert t % rt == 0 and b % bb == 0
    mw, hd, nh = M_WIDTH, M_HEAD_DIM, M_HEADS

    def tok(width):
        return pl.BlockSpec((bb, rt, width), lambda i, j: (i, j, 0))

    cspec = pl.BlockSpec((bb, nh, hd, hd), lambda i, j: (i, 0, 0, 0))
    vspec = pl.BlockSpec((bb, nh, 1, hd), lambda i, j: (i, 0, 0, 0))
    return pl.pallas_call(
        functools.partial(_mlstm_kernel, bb=bb, blk=blk, nblk=nblk),
        grid=(b // bb, t // rt),
        in_specs=[tok(2 * mw), tok(mw), tok(mw), tok(LANES),
                  _const_spec((1, mw), lambda i, j: (0, 0)), _const_spec((rt, rt), lambda i, j: (0, 0)),
                  cspec, vspec, vspec],
        out_specs=[tok(mw), cspec, vspec, vspec],
        out_shape=[
            jax.ShapeDtypeStruct((b, t, mw), BF16),
            jax.ShapeDtypeStruct((b, nh, hd, hd), F32),
            jax.ShapeDtypeStruct((b, nh, 1, hd), F32),
            jax.ShapeDtypeStruct((b, nh, 1, hd), F32),
        ],
        scratch_shapes=[pltpu.VMEM((bb, nh, hd + STATE_EXT_ROWS, hd), F32), pltpu.VMEM((bb, nh, 1, hd), F32)],
        compiler_params=_params(("parallel", "arbitrary")),
        name="mlstm",
    )(qkm, vm, om, gates, gain, _block_tril(rt, blk), c0, n0, m0)


def _ffn_kernel(x_ref, ya_ref, ym_ref, gm_ref, gt1_ref, sh2_ref, sc2_ref, gt2_ref, gffn_ref, gfin_ref,
                wba_ref, wbm_ref, wo_ref, wgu_ref, wd_ref, o_ref, act_scr, *, bt, tt, final):
    d = x_ref.shape[-1]
    m = bt * tt
    dff = wd_ref.shape[1]
    x = x_ref[...]
    pa = jnp.dot(ya_ref[...].reshape(m, ATT_WIDTH), wba_ref[0], preferred_element_type=F32)
    pm = jnp.dot(ym_ref[...].reshape(m, M_WIDTH), wbm_ref[0], preferred_element_type=F32)
    ga = gm_ref[:, :, 0:d].reshape(m, d)
    gmm = gm_ref[:, :, d:2 * d].reshape(m, d)
    merged = _sigmoid(ga) * pa + _sigmoid(gmm) * pm
    y1 = jnp.dot(merged.astype(BF16), wo_ref[0], preferred_element_type=F32)
    x1 = x + gt1_ref[0, :, 0] * y1.reshape(bt, tt, d)
    ms = jnp.mean(x1 * x1, axis=-1, keepdims=True)
    h2 = x1 * lax.rsqrt(ms + EPS) * gffn_ref[...]
    h2 = (h2 * (1.0 + sc2_ref[0, :, 0]) + sh2_ref[0, :, 0]).reshape(m, d).astype(BF16)
    cw = 256
    for c in range(0, dff, cw):
        g = jnp.dot(h2, wgu_ref[0, :, c:c + cw], preferred_element_type=F32)
        up = jnp.dot(h2, wgu_ref[0, :, dff + c:dff + c + cw], preferred_element_type=F32)
        act_scr[:, c:c + cw] = (g * _sigmoid(g) * up).astype(BF16)
    y2 = jnp.dot(act_scr[...], wd_ref[0], preferred_element_type=F32)
    x2 = x1 + gt2_ref[0, :, 0] * y2.reshape(bt, tt, d)
    if final:
        ms2 = jnp.mean(x2 * x2, axis=-1, keepdims=True)
        x2 = x2 * lax.rsqrt(ms2 + EPS) * gfin_ref[...]
    o_ref[...] = x2


def _ffn_call(x, ya, ym, gm, mod, layer, row0, g_ffn, g_final, w_br_att, w_br_mlstm, w_out, w_gate_up,
              w_down, bt, tt, final):
    b, t, d = x.shape
    dff = w_down.shape[1]
    assert b % bt == 0 and t % tt == 0 and dff % 256 == 0

    def wspec(shape):
        return _const_spec((1,) + shape, lambda i, j: (layer, 0, 0))

    def tok(width):
        return pl.BlockSpec((bt, tt, width), lambda i, j: (i, j, 0))

    def modspec(k):
        return pl.BlockSpec((1, bt, 1, 1, d), lambda i, j: (layer, row0 // bt + i, k, 0, 0))

    def const2(shape):
        return _const_spec(shape, lambda i, j: (0, 0))

    return pl.pallas_call(
        functools.partial(_ffn_kernel, bt=bt, tt=tt, final=final),
        grid=(b // bt, t // tt),
        in_specs=[tok(d), tok(ATT_WIDTH), tok(M_WIDTH), tok(2 * d),
                  modspec(2), modspec(3), modspec(4), modspec(5), const2((1, d)), const2((1, d)),
                  wspec((ATT_WIDTH, d)), wspec((M_WIDTH, d)), wspec((d, d)), wspec((d, 2 * dff)),
                  wspec((dff, d))],
        out_specs=tok(d),
        out_shape=jax.ShapeDtypeStruct((b, t, d), F32),
        scratch_shapes=[pltpu.VMEM((bt * tt, dff), BF16)],
        compiler_params=_params(("parallel", "parallel")),
        name="merge_ffn",
    )(x, ya, ym, gm, mod, mod, mod, mod, g_ffn, g_final, w_br_att, w_br_mlstm, w_out, w_gate_up, w_down)


def _layer(x, mod, layer, row0, cache, conv_init, c0, n0, m0, wts, bt, tt, blk, nblk, final):
    (g_mix, w_main, w_gates, b_if, conv_w, conv_b, bias, mh_gain, w_br_att, w_br_mlstm, w_out, g_ffn,
     w_gate_up, w_down, g_final) = wts
    b, t, _ = x.shape
    qkv, kt, vt, qkm, vm, om, gm, gates, ctail = _inproj_call(
        x, mod, layer, row0, g_mix, w_main, w_gates, b_if, conv_w, conv_b, conv_init, bt, tt)
    if cache is None:
        ya = _attn_prompt_call(qkv, bias, layer)
    else:
        ya = _attn_sample_call(qkv, cache[0], cache[1], bias, layer, bt)
    mb = 1 if nblk > 1 else SAMPLE_MLSTM_BATCH_TILE
    ym, c1, n1, m1 = _mlstm_call(qkm, vm, om, gates, mh_gain, c0, n0, m0, mb, blk, nblk)
    x = _ffn_call(x, ya, ym, gm, mod, layer, row0, g_ffn, g_final, w_br_att, w_br_mlstm, w_out, w_gate_up,
                  w_down, bt, tt, final)
    keep = kt.shape[1]
    state = (kt.reshape(b, keep, ATT_HEADS, ATT_HEAD_DIM), vt.reshape(b, keep, ATT_HEADS, ATT_HEAD_DIM),
             ctail[:, SUBLANES - (CONV_W - 1):, :], c1, n1[:, :, 0, :], m1[:, :, 0, 0])
    return x, state


def kernel(x_prompt, x_sample, cache_k, cache_v, state_conv, state_C, state_n, state_m, c_prompt, c_sample,
           w_ada, b_ada, g_mix, w_in, b_if, conv_w, conv_b, rel_bias, mh_gain, w_br_att, w_br_mlstm, w_out,
           g_ffn, w_gate_up, w_down, g_final):
    depth = w_ada.shape[0]
    bp, tp, d = x_prompt.shape
    bs, ts, _ = x_sample.shape
    assert tp % PROMPT_ROWS == 0 and tp % CHUNK == 0 and bs % SAMPLE_BATCH_TILE == 0 and ts % CHUNK != 0
    mw = M_WIDTH

    mod = _ada_call(jnp.concatenate([c_sample, c_prompt], axis=0), w_ada, b_ada)
    mod = mod.reshape(depth, bs + bp, 6, 1, d)

    bias = _bias_call(rel_bias)
    pc = cache_k.shape[2]
    caches = (cache_k.reshape(depth, bs, pc, ATT_WIDTH).astype(BF16),
              cache_v.reshape(depth, bs, pc, ATT_WIDTH).astype(BF16))

    pad3 = SUBLANES - (CONV_W - 1)
    zero_conv = jnp.zeros((bp, SUBLANES, 2 * mw), F32)
    zero_c = jnp.zeros((bp, M_HEADS, M_HEAD_DIM, M_HEAD_DIM), F32)
    zero_v = jnp.zeros((bp, M_HEADS, 1, M_HEAD_DIM), F32)

    w_main, w_gates = _wprep_call(w_in)
    wb_att, wb_mlstm, wb_out = w_br_att.astype(BF16), w_br_mlstm.astype(BF16), w_out.astype(BF16)
    wb_gate_up, wb_down = w_gate_up.astype(BF16), w_down.astype(BF16)

    xp, xs = x_prompt, x_sample
    outs_p, outs_s = [], []
    for l in range(depth):
        bif = jnp.pad(b_if[l], (0, LANES - 2 * M_HEADS)).reshape(1, LANES)
        wts = (g_mix[l].reshape(1, d), w_main, w_gates, bif, conv_w[l], conv_b[l].reshape(1, 2 * mw),
               bias, mh_gain[l].reshape(1, mw), wb_att, wb_mlstm, wb_out, g_ffn[l].reshape(1, d), wb_gate_up,
               wb_down, g_final.reshape(1, d))
        final = l == depth - 1
        xp, st_p = _layer(xp, mod, l, bs, None, zero_conv, zero_c, zero_v, zero_v, wts,
                          1, PROMPT_ROWS, MLSTM_BLOCK, PROMPT_ROWS // MLSTM_BLOCK, final)
        conv_init = jnp.pad(state_conv[l], ((0, 0), (pad3, 0), (0, 0)))
        m0 = jnp.broadcast_to(state_m[l][:, :, None, None], (bs, M_HEADS, 1, M_HEAD_DIM))
        xs, st_s = _layer(xs, mod, l, 0, caches, conv_init, state_C[l],
                          state_n[l][:, :, None, :], m0, wts, SAMPLE_BATCH_TILE, ts, ts, 1, final)
        outs_p.append(st_p)
        outs_s.append(st_s)

    def stk(outs, i):
        return jnp.stack([o[i] for o in outs], axis=0)

    return (xp, xs,
            stk(outs_p, 0), stk(outs_p, 1), stk(outs_p, 2), stk(outs_p, 3), stk(outs_p, 4), stk(outs_p, 5),
            stk(outs_s, 0), stk(outs_s, 1), stk(outs_s, 2), stk(outs_s, 3), stk(outs_s, 4), stk(outs_s, 5))
```

```python
import functools

import jax
import jax.numpy as jnp
from jax import lax
from jax.experimental import pallas as pl
from jax.experimental.pallas import tpu as pltpu

F32 = jnp.float32
BF16 = jnp.bfloat16

CHUNK = 64
N_PAST_CHUNKS = 8
ATT_HEADS = 8
ATT_HEAD_DIM = 64
ATT_WIDTH = ATT_HEADS * ATT_HEAD_DIM
MAX_REL = 256
M_HEADS = 4
M_HEAD_DIM = 128
M_WIDTH = M_HEADS * M_HEAD_DIM
CONV_W = 4
EPS = 1e-6
NEG = -1e30

LANES = 128
SUBLANES = 8
VMEM_LIMIT = 56 * 1024 * 1024
PROMPT_ROWS = 512
SAMPLE_BATCH_TILE = 8
SAMPLE_MLSTM_BATCH_TILE = 4
BAND = (N_PAST_CHUNKS + 1) * CHUNK
N_PAIRS = ATT_HEADS // 2
ATTN_UNROLL = 4
MLSTM_BLOCK = 128
MLSTM_TILE_ROWS = 512
STATE_EXT_ROWS = 16
LOG2E = 1.4426950408889634


def _sigmoid(x):
    return 1.0 / (1.0 + jnp.exp2(x * (-LOG2E)))


def _log_sigmoid(x):
    return jnp.minimum(x, 0.0) - jnp.log1p(jnp.exp(-jnp.abs(x)))


def _const_spec(shape, index_map):
    return pl.BlockSpec(shape, index_map, pipeline_mode=pl.Buffered(1))


def _params(sem):
    return pltpu.CompilerParams(dimension_semantics=sem, vmem_limit_bytes=VMEM_LIMIT)


def _ada_kernel(c_ref, w_ref, b_ref, o_ref):
    c = c_ref[...]
    a = (c * _sigmoid(c)).astype(BF16)
    o_ref[0] = jnp.dot(a, w_ref[0].astype(BF16), preferred_element_type=F32) + b_ref[0]


def _ada_call(c_all, w_ada, b_ada):
    depth, d, n = w_ada.shape
    r = c_all.shape[0]
    tn = 1536
    assert n % tn == 0
    return pl.pallas_call(
        _ada_kernel,
        grid=(depth, n // tn),
        in_specs=[
            pl.BlockSpec((r, d), lambda l, j: (0, 0)),
            pl.BlockSpec((1, d, tn), lambda l, j: (l, 0, j)),
            pl.BlockSpec((1, 1, tn), lambda l, j: (l, 0, j)),
        ],
        out_specs=pl.BlockSpec((1, r, tn), lambda l, j: (l, 0, j)),
        out_shape=jax.ShapeDtypeStruct((depth, r, n), F32),
        compiler_params=_params(("parallel", "parallel")),
        name="ada_mod",
    )(c_all, w_ada, b_ada.reshape(depth, 1, n))


def _wprep_kernel(w_ref, main_ref, gates_ref):
    n_att = 3 * ATT_WIDTH + 4 * M_WIDTH
    n_g = 2 * M_HEADS
    w = w_ref[0]
    main_ref[0, :, 0:n_att] = w[:, 0:n_att].astype(BF16)
    main_ref[0, :, n_att:] = w[:, n_att + n_g:].astype(BF16)
    slab = w[:, n_att:n_att + LANES]
    lane = lax.broadcasted_iota(jnp.int32, slab.shape, 1)
    gates_ref[0] = jnp.where(lane < n_g, slab, 0.0).astype(BF16)


def _wprep_call(w_in):
    depth, d, n = w_in.shape
    n_main = n - 2 * M_HEADS
    tr = 256
    assert d % tr == 0 and n_main % LANES == 0
    return pl.pallas_call(
        _wprep_kernel,
        grid=(depth, d // tr),
        in_specs=[pl.BlockSpec((1, tr, n), lambda l, i: (l, i, 0))],
        out_specs=[pl.BlockSpec((1, tr, n_main), lambda l, i: (l, i, 0)),
                   pl.BlockSpec((1, tr, LANES), lambda l, i: (l, i, 0))],
        out_shape=[jax.ShapeDtypeStruct((depth, d, n_main), BF16), jax.ShapeDtypeStruct((depth, d, LANES), BF16)],
        compiler_params=_params(("parallel", "parallel")),
        name="w_in_prep",
    )(w_in)


def _inproj_kernel(x_ref, sh_ref, sc_ref, g_ref, w_ref, wg_ref, bif_ref, cw_ref, cb_ref, cinit_ref,
                   qkv_ref, kt_ref, vt_ref, qkm_ref, vm_ref, om_ref, gm_ref, gates_ref, ctail_ref,
                   conv_scr, hb_scr, *, bt, tt):
    t = pl.program_id(1)
    d = x_ref.shape[-1]
    m = bt * tt
    x = x_ref[...]
    ms = jnp.mean(x * x, axis=-1, keepdims=True)
    h = x * lax.rsqrt(ms + EPS) * g_ref[...]
    h = h * (1.0 + sc_ref[0, :, 0]) + sh_ref[0, :, 0]
    hb_scr[...] = h.reshape(m, d).astype(BF16)

    def proj(c0, width):
        return jnp.dot(hb_scr[...], w_ref[0, :, c0:c0 + width], preferred_element_type=F32)

    aw = ATT_WIDTH
    mw = M_WIDTH
    c0 = 3 * aw

    @pl.when(t == 0)
    def _():
        conv_scr[:, 0:SUBLANES, :] = cinit_ref[...]

    conv_scr[:, SUBLANES:SUBLANES + tt, :] = proj(c0, 2 * mw).reshape(bt, tt, 2 * mw)
    lo = SUBLANES - (CONV_W - 1)
    acc = conv_scr[:, lo:lo + tt, :] * cw_ref[0:1, :]
    for j in range(1, CONV_W):
        acc = acc + conv_scr[:, lo + j:lo + j + tt, :] * cw_ref[j:j + 1, :]
    acc = acc + cb_ref[...]
    act = acc * _sigmoid(acc)
    qkm_ref[:, :, 0:mw] = act[:, :, 0:mw].astype(BF16)
    qkm_ref[:, :, mw:2 * mw] = (act[:, :, mw:2 * mw] * (M_HEAD_DIM ** -0.5)).astype(BF16)
    tail = conv_scr[:, tt:tt + SUBLANES, :]
    ctail_ref[...] = tail
    conv_scr[:, 0:SUBLANES, :] = tail
    u = proj(0, 3 * aw)
    qkv_ref[:, :, 0:aw] = (u[:, 0:aw] * (LOG2E * ATT_HEAD_DIM ** -0.5)).astype(BF16).reshape(bt, tt, aw)
    qkv_ref[:, :, aw:3 * aw] = u[:, aw:3 * aw].astype(BF16).reshape(bt, tt, 2 * aw)
    kt_ref[...] = u[:, aw:2 * aw].reshape(bt, tt, aw)
    vt_ref[...] = u[:, 2 * aw:3 * aw].reshape(bt, tt, aw)
    c0 += 2 * mw
    u = proj(c0, 2 * mw)
    vm_ref[...] = u[:, 0:mw].astype(BF16).reshape(bt, tt, mw)
    om_ref[...] = u[:, mw:2 * mw].reshape(bt, tt, mw)
    c0 += 2 * mw
    for j in range(0, 2 * d, d):
        gm_ref[:, :, j:j + d] = proj(c0 + j, d).reshape(bt, tt, d)
    ug = jnp.dot(hb_scr[...], wg_ref[0], preferred_element_type=F32) + bif_ref[...]
    col = lax.broadcasted_iota(jnp.int32, ug.shape, 1)
    gates_ref[...] = jnp.where(col >= M_HEADS, _log_sigmoid(ug), ug).reshape(bt, tt, LANES)


def _inproj_call(x, mod, layer, row0, g_mix, w_main, w_gates, b_if, conv_w, conv_b, conv_init, bt, tt):
    b, t, d = x.shape
    nb, nt = b // bt, t // tt
    keep = min(N_PAST_CHUNKS * CHUNK, t)
    assert b % bt == 0 and t % tt == 0 and keep % tt == 0
    t_keep0 = nt - keep // tt
    n_main = w_main.shape[2]
    aw, mw = ATT_WIDTH, M_WIDTH

    def wspec(shape):
        return _const_spec((1,) + shape, lambda i, j: (layer, 0, 0))

    def tok(width):
        return pl.BlockSpec((bt, tt, width), lambda i, j: (i, j, 0))

    def modspec(k):
        return pl.BlockSpec((1, bt, 1, 1, d), lambda i, j: (layer, row0 // bt + i, k, 0, 0))

    def const2(shape):
        return _const_spec(shape, lambda i, j: (0, 0))

    keep_spec = pl.BlockSpec((bt, tt, aw), lambda i, j: (i, jnp.maximum(j - t_keep0, 0), 0))
    per_batch8 = pl.BlockSpec((bt, SUBLANES, 2 * mw), lambda i, j: (i, 0, 0))
    outs = pl.pallas_call(
        functools.partial(_inproj_kernel, bt=bt, tt=tt),
        grid=(nb, nt),
        in_specs=[
            tok(d), modspec(0), modspec(1), const2((1, d)),
            wspec((d, n_main)), wspec((d, LANES)), const2((1, LANES)),
            const2((CONV_W, 2 * mw)), const2((1, 2 * mw)), per_batch8,
        ],
        out_specs=[tok(3 * aw), keep_spec, keep_spec, tok(2 * mw), tok(mw), tok(mw), tok(2 * d),
                   tok(LANES), per_batch8],
        out_shape=[
            jax.ShapeDtypeStruct((b, t, 3 * aw), BF16),
            jax.ShapeDtypeStruct((b, keep, aw), F32),
            jax.ShapeDtypeStruct((b, keep, aw), F32),
            jax.ShapeDtypeStruct((b, t, 2 * mw), BF16),
            jax.ShapeDtypeStruct((b, t, mw), BF16),
            jax.ShapeDtypeStruct((b, t, mw), F32),
            jax.ShapeDtypeStruct((b, t, 2 * d), F32),
            jax.ShapeDtypeStruct((b, t, LANES), F32),
            jax.ShapeDtypeStruct((b, SUBLANES, 2 * mw), F32),
        ],
        scratch_shapes=[pltpu.VMEM((bt, SUBLANES + tt, 2 * mw), F32), pltpu.VMEM((bt * tt, d), BF16)],
        compiler_params=_params(("parallel", "arbitrary")),
        name="inproj",
    )(x, mod, mod, g_mix, w_main, w_gates, b_if, conv_w, conv_b, conv_init)
    return outs


def _bias_kernel(rev_ref, o_ref):
    w = rev_ref.shape[-1]
    x = jnp.broadcast_to(rev_ref[0, 0], (CHUNK, w))
    r = pltpu.roll(x, w - (CHUNK - 1), 1, stride=1, stride_axis=0)
    o_ref[0, 0] = r[:, 0:BAND] * LOG2E


def _bias_call(rel_bias):
    depth, nh, rel = rel_bias.shape
    n_ext = BAND + CHUNK - 1
    w = -(-n_ext // LANES) * LANES
    ext = jnp.concatenate([rel_bias, jnp.broadcast_to(rel_bias[:, :, rel - 1:], (depth, nh, n_ext - rel))], axis=2)
    rev = jnp.pad(ext[:, :, ::-1], ((0, 0), (0, 0), (0, w - n_ext))).reshape(depth, nh, 1, w)
    out = pl.pallas_call(
        _bias_kernel,
        grid=(depth, nh),
        in_specs=[pl.BlockSpec((1, 1, 1, w), lambda l, h: (l, h, 0, 0))],
        out_specs=pl.BlockSpec((1, 1, CHUNK, BAND), lambda l, h: (l, h, 0, 0)),
        out_shape=jax.ShapeDtypeStruct((depth, nh, CHUNK, BAND), F32),
        compiler_params=_params(("parallel", "parallel")),
        name="rel_bias",
    )(rev)
    return out.reshape(depth, N_PAIRS, 2 * CHUNK, BAND)


def _pair_rows(qp):
    lane = lax.broadcasted_iota(jnp.int32, qp.shape, 1)
    zero = jnp.zeros_like(qp)
    return jnp.concatenate([jnp.where(lane < ATT_HEAD_DIM, qp, zero),
                            jnp.where(lane >= ATT_HEAD_DIM, qp, zero)], axis=0)


def _pair_merge(o, rows):
    lane = lax.broadcasted_iota(jnp.int32, (rows, LANES), 1)
    return jnp.where(lane < ATT_HEAD_DIM, o[0:rows], o[rows:2 * rows])


def _nt_dot(a, b):
    return lax.dot_general(a, b, (((1,), (1,)), ((), ())), preferred_element_type=F32)


def _attn_prompt_kernel(q_ref, kp_ref, kc_ref, vp_ref, vc_ref, bias_ref, o_ref, kband, vband, *, unroll):
    j = pl.program_id(1)
    rows = kc_ref.shape[1]
    past = N_PAST_CHUNKS * CHUNK
    assert rows == past

    @pl.when(j == 0)
    def _():
        kband[0:past, :] = jnp.zeros((past, ATT_WIDTH), BF16)
        vband[0:past, :] = jnp.zeros((past, ATT_WIDTH), BF16)

    @pl.when(j > 0)
    def _():
        kband[0:past, :] = kp_ref[0]
        vband[0:past, :] = vp_ref[0]

    kband[past:past + rows, :] = kc_ref[0]
    vband[past:past + rows, :] = vc_ref[0]

    def chunks(i, masked):
        units = [(pl.multiple_of((i * unroll + u) * CHUNK, CHUNK), i * unroll + u, p)
                 for u in range(unroll) for p in range(N_PAIRS)]
        scores = []
        for r0, ci, p in units:
            ls = slice(p * LANES, (p + 1) * LANES)
            qs = _pair_rows(q_ref[0, pl.ds(r0, CHUNK), ls])
            scores.append(_nt_dot(qs, kband[pl.ds(r0, BAND), ls]) + bias_ref[0, p])
        for (r0, ci, p), s in zip(units, scores):
            ls = slice(p * LANES, (p + 1) * LANES)
            if masked:
                col = lax.broadcasted_iota(jnp.int32, s.shape, 1)
                s = jnp.where(col >= (N_PAST_CHUNKS - ci) * CHUNK, s, NEG)
            mx = jnp.max(s, axis=-1, keepdims=True)
            e = jnp.exp2(s - mx)
            den = jnp.sum(e, axis=-1, keepdims=True)
            o = jnp.dot(e.astype(BF16), vband[pl.ds(r0, BAND), ls], preferred_element_type=F32) / den
            o_ref[0, pl.ds(r0, CHUNK), ls] = _pair_merge(o, CHUNK).astype(BF16)

    n_iter = rows // (CHUNK * unroll)

    @pl.when(j == 0)
    def _():
        lax.fori_loop(0, n_iter, lambda i, c: (chunks(i, True), c)[1], 0)

    @pl.when(j > 0)
    def _():
        lax.fori_loop(0, n_iter, lambda i, c: (chunks(i, False), c)[1], 0)


def _attn_prompt_call(qkv, bias, layer):
    b, t, _ = qkv.shape
    rows = N_PAST_CHUNKS * CHUNK
    assert t % rows == 0
    aw = ATT_WIDTH

    def cur(cb):
        return pl.BlockSpec((1, rows, aw), lambda i, j: (i, j, cb))

    def prev(cb):
        return pl.BlockSpec((1, rows, aw), lambda i, j: (i, jnp.maximum(j - 1, 0), cb))

    return pl.pallas_call(
        functools.partial(_attn_prompt_kernel, unroll=ATTN_UNROLL),
        grid=(b, t // rows),
        in_specs=[cur(0), prev(1), cur(1), prev(2), cur(2),
                  _const_spec((1, N_PAIRS, 2 * CHUNK, BAND), lambda i, j: (layer, 0, 0, 0))],
        out_specs=pl.BlockSpec((1, rows, aw), lambda i, j: (i, j, 0)),
        out_shape=jax.ShapeDtypeStruct((b, t, aw), BF16),
        scratch_shapes=[pltpu.VMEM((2 * rows, aw), BF16), pltpu.VMEM((2 * rows, aw), BF16)],
        compiler_params=_params(("parallel", "parallel")),
        name="attn_prompt",
    )(qkv, qkv, qkv, qkv, qkv, bias)


def _attn_sample_kernel(q_ref, kn_ref, vn_ref, kc_ref, vc_ref, bias_ref, o_ref, *, bb):
    tq = q_ref.shape[1]
    pc = kc_ref.shape[2]
    for bi in range(bb):
        scores = []
        for p in range(N_PAIRS):
            ls = slice(p * LANES, (p + 1) * LANES)
            qs = _pair_rows(q_ref[bi, :, ls])
            bias = jnp.concatenate([bias_ref[0, p, 0:tq, :], bias_ref[0, p, CHUNK:CHUNK + tq, :]], axis=0)
            scores.append((_nt_dot(qs, kc_ref[0, bi, :, ls].astype(BF16)) + bias[:, 0:pc],
                           _nt_dot(qs, kn_ref[bi, :, ls]) + bias[:, pc:pc + tq]))
        for p, (s1, s2) in enumerate(scores):
            ls = slice(p * LANES, (p + 1) * LANES)
            mx = jnp.maximum(jnp.max(s1, axis=-1, keepdims=True), jnp.max(s2, axis=-1, keepdims=True))
            e1 = jnp.exp2(s1 - mx)
            e2 = jnp.exp2(s2 - mx)
            den = jnp.sum(e1, axis=-1, keepdims=True) + jnp.sum(e2, axis=-1, keepdims=True)
            o = (jnp.dot(e1.astype(BF16), vc_ref[0, bi, :, ls].astype(BF16), preferred_element_type=F32)
                 + jnp.dot(e2.astype(BF16), vn_ref[bi, :, ls], preferred_element_type=F32)) / den
            o_ref[bi, :, ls] = _pair_merge(o, tq).astype(BF16)


def _attn_sample_call(qkv, cache_k, cache_v, bias, layer, bb):
    b, tq, _ = qkv.shape
    _, _, pc, aw = cache_k.shape
    assert b % bb == 0 and pc == N_PAST_CHUNKS * CHUNK and tq <= CHUNK and aw == ATT_WIDTH

    def new(cb):
        return pl.BlockSpec((bb, tq, aw), lambda i: (i, 0, cb))

    cache = pl.BlockSpec((1, bb, pc, aw), lambda i: (layer, i, 0, 0))
    return pl.pallas_call(
        functools.partial(_attn_sample_kernel, bb=bb),
        grid=(b // bb,),
        in_specs=[new(0), new(1), new(2), cache, cache,
                  _const_spec((1, N_PAIRS, 2 * CHUNK, BAND), lambda i: (layer, 0, 0, 0))],
        out_specs=pl.BlockSpec((bb, tq, aw), lambda i: (i, 0, 0)),
        out_shape=jax.ShapeDtypeStruct((b, tq, aw), BF16),
        compiler_params=_params(("parallel",)),
        name="attn_sample",
    )(qkv, qkv, qkv, cache_k, cache_v, bias)


def _transpose_rows(a):
    r = a.shape[0]
    pad = -r % LANES
    if pad:
        a = jnp.concatenate([a, jnp.zeros((pad, a.shape[1]), a.dtype)], axis=0)
    return a.T[:, 0:r]


def _transpose_cols(a):
    c = a.shape[1]
    pad = -c % LANES
    if pad:
        a = jnp.concatenate([a, jnp.zeros((a.shape[0], pad), a.dtype)], axis=1)
    return a.T[0:c, :]


def _block_cumsum(tri, g):
    g1 = g.astype(BF16)
    r1 = g - g1.astype(F32)
    g2 = r1.astype(BF16)
    g3 = (r1 - g2.astype(F32)).astype(BF16)
    return (jnp.dot(tri, g1, preferred_element_type=F32) + jnp.dot(tri, g2, preferred_element_type=F32)
            + jnp.dot(tri, g3, preferred_element_type=F32))


def _mlstm_kernel(qkm_ref, vm_ref, om_ref, gates_ref, gain_ref, tri_ref, c0_ref, n0_ref, m0_ref,
                  ym_ref, c1_ref, n1_ref, m1_ref, cn_scr, m_scr, *, bb, blk, nblk):
    tb = pl.program_id(1)
    hd = M_HEAD_DIM
    ext = cn_scr.shape[2] - hd

    def last_lane(r):
        return jnp.broadcast_to(r[:, blk - 1:blk], r.shape)

    ext_row0 = lax.broadcasted_iota(jnp.int32, (ext, hd), 0) == 0

    @pl.when(tb == 0)
    def _():
        for e in range(bb):
            for h in range(M_HEADS):
                cn_scr[e, h, 0:hd, :] = c0_ref[e, h].T
                cn_scr[e, h, hd:hd + ext, :] = jnp.where(ext_row0, n0_ref[e, h], 0.0)
        m_scr[...] = m0_ref[...]

    srow = lax.broadcasted_iota(jnp.int32, (blk, blk), 0)
    tcol = lax.broadcasted_iota(jnp.int32, (blk, blk), 1)
    causal = srow <= tcol
    ones_ext = jnp.where(lax.broadcasted_iota(jnp.int32, (ext, blk), 0) == 0, 1.0, 0.0)

    gt_all, bt_all, c_all = [], [], []
    for e in range(bb):
        g = gates_ref[e]
        b = _block_cumsum(tri_ref[...], g)
        gt_all.append(_transpose_rows(g))
        bt_all.append(_transpose_rows(b))
        c_all.append(g[:, 0:M_HEADS] - b[:, M_HEADS:2 * M_HEADS])

    units = [(e, h, bi) for e in range(bb) for h in range(M_HEADS) for bi in range(nblk)]
    vt_heads = {(e, h): _transpose_rows(vm_ref[e, :, h * hd:(h + 1) * hd].astype(F32))
                for e in range(bb) for h in range(M_HEADS)}
    scores = {}
    for e, h, bi in units:
        rs = slice(bi * blk, (bi + 1) * blk)
        q = qkm_ref[e, rs, h * hd:(h + 1) * hd]
        k = qkm_ref[e, rs, M_WIDTH + h * hd:M_WIDTH + (h + 1) * hd]
        b_row = bt_all[e][M_HEADS + h:M_HEADS + h + 1, rs]
        dmat = jnp.where(causal, b_row + c_all[e][rs, h:h + 1], NEG)
        m_in = jnp.max(dmat, axis=0, keepdims=True)
        scores[e, h, bi] = (m_in, _nt_dot(k, q) * jnp.exp(dmat - m_in))
    intra = {}
    for e, h, bi in units:
        rs = slice(bi * blk, (bi + 1) * blk)
        k = qkm_ref[e, rs, M_WIDTH + h * hd:M_WIDTH + (h + 1) * hd]
        li_row, b_row = gt_all[e][h:h + 1, rs], bt_all[e][M_HEADS + h:M_HEADS + h + 1, rs]
        m_in, smat = scores[e, h, bi]
        vt = jnp.concatenate([vt_heads[e, h][:, rs], ones_ext], axis=0)
        den_i = jnp.sum(smat, axis=0, keepdims=True)
        num_i = jnp.dot(vt[0:hd].astype(BF16), smat.astype(BF16), preferred_element_type=F32)
        b_last = last_lane(b_row)
        m_u = last_lane(m_in)
        w_u = jnp.exp(b_last - b_row + li_row - m_u)
        upd = jnp.dot((vt * w_u).astype(BF16), k, preferred_element_type=F32)
        intra[e, h, bi] = (den_i, num_i, b_last, m_u, upd)
    for bi in range(nblk):
        rs = slice(bi * blk, (bi + 1) * blk)
        for e, h in [(e, h) for e in range(bb) for h in range(M_HEADS)]:
            ls = slice(h * hd, (h + 1) * hd)
            q = qkm_ref[e, rs, ls]
            b_row = bt_all[e][M_HEADS + h:M_HEADS + h + 1, rs]
            m_in = scores[e, h, bi][0]
            den_i, num_i, b_last, m_u, upd = intra[e, h, bi]
            m_prev = m_scr[e, h][:, 0:blk]
            cn_prev = cn_scr[e, h]
            a = b_row + m_prev
            m_t = jnp.maximum(a, m_in)
            alpha = jnp.exp(m_in - m_t)
            beta = jnp.exp(a - m_t)
            inter = _nt_dot(cn_prev.astype(BF16), q)
            num = alpha * num_i + beta * inter[0:hd]
            den = alpha * den_i + beta * inter[hd:hd + 1]
            hh = num / jnp.maximum(jnp.abs(den), jnp.exp(-m_t))
            hh = hh * lax.rsqrt(jnp.mean(hh * hh, axis=0, keepdims=True) + EPS)
            y = _transpose_cols(hh) * gain_ref[:, ls] * _sigmoid(om_ref[e, rs, ls])
            ym_ref[e, rs, ls] = y.astype(BF16)
            m_new = last_lane(m_t)
            g_state = jnp.exp(b_last + m_prev - m_new)[:, 0:1]
            g_upd = jnp.exp(m_u - m_new)[:, 0:1]
            cn_scr[e, h] = g_state * cn_prev + g_upd * upd
            m_scr[e, h] = jnp.broadcast_to(m_new[:, 0:1], (1, LANES))

    @pl.when(tb == pl.num_programs(1) - 1)
    def _():
        for e in range(bb):
            for h in range(M_HEADS):
                c1_ref[e, h] = cn_scr[e, h, 0:hd, :].T
                n1_ref[e, h] = cn_scr[e, h, hd:hd + 1, :]
        m1_ref[...] = m_scr[...]


def _block_tril(rt, blk):
    r = jnp.arange(rt)
    return ((r[:, None] // blk == r[None, :] // blk) & (r[None, :] <= r[:, None])).astype(BF16)


def _mlstm_call(qkm, vm, om, gates, gain, c0, n0, m0, bb, blk, nblk):
    b, t, _ = qkm.shape
    rt = blk * nblk
    assert t % rt == 0 and b % bb == 0
    mw, hd, nh = M_WIDTH, M_HEAD_DIM, M_HEADS

    def tok(width):
        return pl.BlockSpec((bb, rt, width), lambda i, j: (i, j, 0))

    cspec = pl.BlockSpec((bb, nh, hd, hd), lambda i, j: (i, 0, 0, 0))
    vspec = pl.BlockSpec((bb, nh, 1, hd), lambda i, j: (i, 0, 0, 0))
    return pl.pallas_call(
        functools.partial(_mlstm_kernel, bb=bb, blk=blk, nblk=nblk),
        grid=(b // bb, t // rt),
        in_specs=[tok(2 * mw), tok(mw), tok(mw), tok(LANES),
                  _const_spec((1, mw), lambda i, j: (0, 0)), _const_spec((rt, rt), lambda i, j: (0, 0)),
                  cspec, vspec, vspec],
        out_specs=[tok(mw), cspec, vspec, vspec],
        out_shape=[
            jax.ShapeDtypeStruct((b, t, mw), BF16),
            jax.ShapeDtypeStruct((b, nh, hd, hd), F32),
            jax.ShapeDtypeStruct((b, nh, 1, hd), F32),
            jax.ShapeDtypeStruct((b, nh, 1, hd), F32),
        ],
        scratch_shapes=[pltpu.VMEM((bb, nh, hd + STATE_EXT_ROWS, hd), F32), pltpu.VMEM((bb, nh, 1, hd), F32)],
        compiler_params=_params(("parallel", "arbitrary")),
        name="mlstm",
    )(qkm, vm, om, gates, gain, _block_tril(rt, blk), c0, n0, m0)


def _ffn_kernel(x_ref, ya_ref, ym_ref, gm_ref, gt1_ref, sh2_ref, sc2_ref, gt2_ref, gffn_ref, gfin_ref,
                wba_ref, wbm_ref, wo_ref, wgu_ref, wd_ref, o_ref, act_scr, *, bt, tt, final):
    d = x_ref.shape[-1]
    m = bt * tt
    dff = wd_ref.shape[1]
    x = x_ref[...]
    pa = jnp.dot(ya_ref[...].reshape(m, ATT_WIDTH), wba_ref[0], preferred_element_type=F32)
    pm = jnp.dot(ym_ref[...].reshape(m, M_WIDTH), wbm_ref[0], preferred_element_type=F32)
    ga = gm_ref[:, :, 0:d].reshape(m, d)
    gmm = gm_ref[:, :, d:2 * d].reshape(m, d)
    merged = _sigmoid(ga) * pa + _sigmoid(gmm) * pm
    y1 = jnp.dot(merged.astype(BF16), wo_ref[0], preferred_element_type=F32)
    x1 = x + gt1_ref[0, :, 0] * y1.reshape(bt, tt, d)
    ms = jnp.mean(x1 * x1, axis=-1, keepdims=True)
    h2 = x1 * lax.rsqrt(ms + EPS) * gffn_ref[...]
    h2 = (h2 * (1.0 + sc2_ref[0, :, 0]) + sh2_ref[0, :, 0]).reshape(m, d).astype(BF16)
    cw = 256
    for c in range(0, dff, cw):
        g = jnp.dot(h2, wgu_ref[0, :, c:c + cw], preferred_element_type=F32)
        up = jnp.dot(h2, wgu_ref[0, :, dff + c:dff + c + cw], preferred_element_type=F32)
        act_scr[:, c:c + cw] = (g * _sigmoid(g) * up).astype(BF16)
    y2 = jnp.dot(act_scr[...], wd_ref[0], preferred_element_type=F32)
    x2 = x1 + gt2_ref[0, :, 0] * y2.reshape(bt, tt, d)
    if final:
        ms2 = jnp.mean(x2 * x2, axis=-1, keepdims=True)
        x2 = x2 * lax.rsqrt(ms2 + EPS) * gfin_ref[...]
    o_ref[...] = x2


def _ffn_call(x, ya, ym, gm, mod, layer, row0, g_ffn, g_final, w_br_att, w_br_mlstm, w_out, w_gate_up,
              w_down, bt, tt, final):
    b, t, d = x.shape
    dff = w_down.shape[1]
    assert b % bt == 0 and t % tt == 0 and dff % 256 == 0

    def wspec(shape):
        return _const_spec((1,) + shape, lambda i, j: (layer, 0, 0))

    def tok(width):
        return pl.BlockSpec((bt, tt, width), lambda i, j: (i, j, 0))

    def modspec(k):
        return pl.BlockSpec((1, bt, 1, 1, d), lambda i, j: (layer, row0 // bt + i, k, 0, 0))

    def const2(shape):
        return _const_spec(shape, lambda i, j: (0, 0))

    return pl.pallas_call(
        functools.partial(_ffn_kernel, bt=bt, tt=tt, final=final),
        grid=(b // bt, t // tt),
        in_specs=[tok(d), tok(ATT_WIDTH), tok(M_WIDTH), tok(2 * d),
                  modspec(2), modspec(3), modspec(4), modspec(5), const2((1, d)), const2((1, d)),
                  wspec((ATT_WIDTH, d)), wspec((M_WIDTH, d)), wspec((d, d)), wspec((d, 2 * dff)),
                  wspec((dff, d))],
        out_specs=tok(d),
        out_shape=jax.ShapeDtypeStruct((b, t, d), F32),
        scratch_shapes=[pltpu.VMEM((bt * tt, dff), BF16)],
        compiler_params=_params(("parallel", "parallel")),
        name="merge_ffn",
    )(x, ya, ym, gm, mod, mod, mod, mod, g_ffn, g_final, w_br_att, w_br_mlstm, w_out, w_gate_up, w_down)


def _layer(x, mod, layer, row0, cache, conv_init, c0, n0, m0, wts, bt, tt, blk, nblk, final):
    (g_mix, w_main, w_gates, b_if, conv_w, conv_b, bias, mh_gain, w_br_att, w_br_mlstm, w_out, g_ffn,
     w_gate_up, w_down, g_final) = wts
    b, t, _ = x.shape
    qkv, kt, vt, qkm, vm, om, gm, gates, ctail = _inproj_call(
        x, mod, layer, row0, g_mix, w_main, w_gates, b_if, conv_w, conv_b, conv_init, bt, tt)
    if cache is None:
        ya = _attn_prompt_call(qkv, bias, layer)
    else:
        ya = _attn_sample_call(qkv, cache[0], cache[1], bias, layer, bt)
    mb = 1 if nblk > 1 else SAMPLE_MLSTM_BATCH_TILE
    ym, c1, n1, m1 = _mlstm_call(qkm, vm, om, gates, mh_gain, c0, n0, m0, mb, blk, nblk)
    x = _ffn_call(x, ya, ym, gm, mod, layer, row0, g_ffn, g_final, w_br_att, w_br_mlstm, w_out, w_gate_up,
                  w_down, bt, tt, final)
    keep = kt.shape[1]
    state = (kt.reshape(b, keep, ATT_HEADS, ATT_HEAD_DIM), vt.reshape(b, keep, ATT_HEADS, ATT_HEAD_DIM),
             ctail[:, SUBLANES - (CONV_W - 1):, :], c1, n1[:, :, 0, :], m1[:, :, 0, 0])
    return x, state


def kernel(x_prompt, x_sample, cache_k, cache_v, state_conv, state_C, state_n, state_m, c_prompt, c_sample,
           w_ada, b_ada, g_mix, w_in, b_if, conv_w, conv_b, rel_bias, mh_gain, w_br_att, w_br_mlstm, w_out,
           g_ffn, w_gate_up, w_down, g_final):
    depth = w_ada.shape[0]
    bp, tp, d = x_prompt.shape
    bs, ts, _ = x_sample.shape
    assert tp % PROMPT_ROWS == 0 and tp % CHUNK == 0 and bs % SAMPLE_BATCH_TILE == 0 and ts % CHUNK != 0
    mw = M_WIDTH

    mod = _ada_call(jnp.concatenate([c_sample, c_prompt], axis=0), w_ada, b_ada)
    mod = mod.reshape(depth, bs + bp, 6, 1, d)

    bias = _bias_call(rel_bias)
    pc = cache_k.shape[2]
    caches = (cache_k.reshape(depth, bs, pc, ATT_WIDTH), cache_v.reshape(depth, bs, pc, ATT_WIDTH))

    pad3 = SUBLANES - (CONV_W - 1)
    zero_conv = jnp.zeros((bp, SUBLANES, 2 * mw), F32)
    zero_c = jnp.zeros((bp, M_HEADS, M_HEAD_DIM, M_HEAD_DIM), F32)
    zero_v = jnp.zeros((bp, M_HEADS, 1, M_HEAD_DIM), F32)

    w_main, w_gates = _wprep_call(w_in)
    wb_att, wb_mlstm, wb_out = w_br_att.astype(BF16), w_br_mlstm.astype(BF16), w_out.astype(BF16)
    wb_gate_up, wb_down = w_gate_up.astype(BF16), w_down.astype(BF16)

    xp, xs = x_prompt, x_sample
    outs_p, outs_s = [], []
    for l in range(depth):
        bif = jnp.pad(b_if[l], (0, LANES - 2 * M_HEADS)).reshape(1, LANES)
        wts = (g_mix[l].reshape(1, d), w_main, w_gates, bif, conv_w[l], conv_b[l].reshape(1, 2 * mw),
               bias, mh_gain[l].reshape(1, mw), wb_att, wb_mlstm, wb_out, g_ffn[l].reshape(1, d), wb_gate_up,
               wb_down, g_final.reshape(1, d))
        final = l == depth - 1
        xp, st_p = _layer(xp, mod, l, bs, None, zero_conv, zero_c, zero_v, zero_v, wts,
                          1, PROMPT_ROWS, MLSTM_BLOCK, MLSTM_TILE_ROWS // MLSTM_BLOCK, final)
        conv_init = jnp.pad(state_conv[l], ((0, 0), (pad3, 0), (0, 0)))
        m0 = jnp.broadcast_to(state_m[l][:, :, None, None], (bs, M_HEADS, 1, M_HEAD_DIM))
        xs, st_s = _layer(xs, mod, l, 0, caches, conv_init, state_C[l],
                          state_n[l][:, :, None, :], m0, wts, SAMPLE_BATCH_TILE, ts, ts, 1, final)
        outs_p.append(st_p)
        outs_s.append(st_s)

    def stk(outs, i):
        return jnp.stack([o[i] for o in outs], axis=0)

    return (xp, xs,
            stk(outs_p, 0), stk(outs_p, 1), stk(outs_p, 2), stk(outs_p, 3), stk(outs_p, 4), stk(outs_p, 5),
            stk(outs_s, 0), stk(outs_s, 1), stk(outs_s, 2), stk(outs_s, 3), stk(outs_s, 4), stk(outs_s, 5))
```

```python
import functools

import jax
import jax.numpy as jnp
from jax import lax
from jax.experimental import pallas as pl
from jax.experimental.pallas import tpu as pltpu

F32 = jnp.float32
BF16 = jnp.bfloat16

CHUNK = 64
N_PAST_CHUNKS = 8
ATT_HEADS = 8
ATT_HEAD_DIM = 64
ATT_WIDTH = ATT_HEADS * ATT_HEAD_DIM
MAX_REL = 256
M_HEADS = 4
M_HEAD_DIM = 128
M_WIDTH = M_HEADS * M_HEAD_DIM
CONV_W = 4
EPS = 1e-6
NEG = -1e30

LANES = 128
SUBLANES = 8
VMEM_LIMIT = 56 * 1024 * 1024
PROMPT_ROWS = 512
SAMPLE_BATCH_TILE = 8
SAMPLE_MLSTM_BATCH_TILE = 4
BAND = (N_PAST_CHUNKS + 1) * CHUNK
N_PAIRS = ATT_HEADS // 2
ATTN_UNROLL = 4
MLSTM_BLOCK = 128
MLSTM_TILE_ROWS = 512
STATE_EXT_ROWS = 16
LOG2E = 1.4426950408889634


def _sigmoid(x):
    return 1.0 / (1.0 + jnp.exp2(x * (-LOG2E)))


def _log_sigmoid(x):
    return jnp.minimum(x, 0.0) - jnp.log1p(jnp.exp(-jnp.abs(x)))


def _const_spec(shape, index_map):
    return pl.BlockSpec(shape, index_map, pipeline_mode=pl.Buffered(1))


def _params(sem):
    return pltpu.CompilerParams(dimension_semantics=sem, vmem_limit_bytes=VMEM_LIMIT)


def _ada_kernel(c_ref, w_ref, b_ref, o_ref):
    c = c_ref[...]
    a = (c * _sigmoid(c)).astype(BF16)
    o_ref[0] = jnp.dot(a, w_ref[0].astype(BF16), preferred_element_type=F32) + b_ref[0]


def _ada_call(c_all, w_ada, b_ada):
    depth, d, n = w_ada.shape
    r = c_all.shape[0]
    tn = 1536
    assert n % tn == 0
    return pl.pallas_call(
        _ada_kernel,
        grid=(depth, n // tn),
        in_specs=[
            pl.BlockSpec((r, d), lambda l, j: (0, 0)),
            pl.BlockSpec((1, d, tn), lambda l, j: (l, 0, j)),
            pl.BlockSpec((1, 1, tn), lambda l, j: (l, 0, j)),
        ],
        out_specs=pl.BlockSpec((1, r, tn), lambda l, j: (l, 0, j)),
        out_shape=jax.ShapeDtypeStruct((depth, r, n), F32),
        compiler_params=_params(("parallel", "parallel")),
        name="ada_mod",
    )(c_all, w_ada, b_ada.reshape(depth, 1, n))


def _wprep_kernel(wt_ref, main_ref, gates_ref):
    n_att = 3 * ATT_WIDTH + 4 * M_WIDTH
    n_g = 2 * M_HEADS
    n_main = main_ref.shape[2]
    step = 512
    for c0 in range(0, n_main, step):
        r0 = c0 if c0 < n_att else c0 + n_g
        main_ref[0, :, c0:c0 + step] = wt_ref[0, r0:r0 + step, :].T.astype(BF16)
    slab = wt_ref[0, n_att:n_att + LANES, :].T
    lane = lax.broadcasted_iota(jnp.int32, slab.shape, 1)
    gates_ref[0] = jnp.where(lane < n_g, slab, 0.0).astype(BF16)


def _wprep_call(w_in):
    depth, d, n = w_in.shape
    n_main = n - 2 * M_HEADS
    assert n_main % 512 == 0 and (3 * ATT_WIDTH + 4 * M_WIDTH) % 512 == 0
    return pl.pallas_call(
        _wprep_kernel,
        grid=(depth,),
        in_specs=[pl.BlockSpec((1, n, d), lambda l: (l, 0, 0), pipeline_mode=pl.Buffered(1))],
        out_specs=[pl.BlockSpec((1, d, n_main), lambda l: (l, 0, 0)),
                   pl.BlockSpec((1, d, LANES), lambda l: (l, 0, 0))],
        out_shape=[jax.ShapeDtypeStruct((depth, d, n_main), BF16), jax.ShapeDtypeStruct((depth, d, LANES), BF16)],
        compiler_params=_params(("parallel",)),
        name="w_in_prep",
    )(jnp.swapaxes(w_in, 1, 2))


def _inproj_kernel(x_ref, sh_ref, sc_ref, g_ref, w_ref, wg_ref, bif_ref, cw_ref, cb_ref, cinit_ref,
                   qkv_ref, kt_ref, vt_ref, qkm_ref, vm_ref, om_ref, gm_ref, gates_ref, ctail_ref,
                   conv_scr, hb_scr, *, bt, tt):
    t = pl.program_id(1)
    d = x_ref.shape[-1]
    m = bt * tt
    x = x_ref[...]
    ms = jnp.mean(x * x, axis=-1, keepdims=True)
    h = x * lax.rsqrt(ms + EPS) * g_ref[...]
    h = h * (1.0 + sc_ref[0, :, 0]) + sh_ref[0, :, 0]
    hb_scr[...] = h.reshape(m, d).astype(BF16)

    def proj(c0, width):
        return jnp.dot(hb_scr[...], w_ref[0, :, c0:c0 + width], preferred_element_type=F32)

    aw = ATT_WIDTH
    mw = M_WIDTH
    c0 = 3 * aw

    @pl.when(t == 0)
    def _():
        conv_scr[:, 0:SUBLANES, :] = cinit_ref[...]

    conv_scr[:, SUBLANES:SUBLANES + tt, :] = proj(c0, 2 * mw).reshape(bt, tt, 2 * mw)
    lo = SUBLANES - (CONV_W - 1)
    acc = conv_scr[:, lo:lo + tt, :] * cw_ref[0:1, :]
    for j in range(1, CONV_W):
        acc = acc + conv_scr[:, lo + j:lo + j + tt, :] * cw_ref[j:j + 1, :]
    acc = acc + cb_ref[...]
    act = acc * _sigmoid(acc)
    qkm_ref[:, :, 0:mw] = act[:, :, 0:mw].astype(BF16)
    qkm_ref[:, :, mw:2 * mw] = (act[:, :, mw:2 * mw] * (M_HEAD_DIM ** -0.5)).astype(BF16)
    tail = conv_scr[:, tt:tt + SUBLANES, :]
    ctail_ref[...] = tail
    conv_scr[:, 0:SUBLANES, :] = tail
    u = proj(0, 3 * aw)
    qkv_ref[:, :, 0:aw] = (u[:, 0:aw] * (LOG2E * ATT_HEAD_DIM ** -0.5)).astype(BF16).reshape(bt, tt, aw)
    qkv_ref[:, :, aw:3 * aw] = u[:, aw:3 * aw].astype(BF16).reshape(bt, tt, 2 * aw)
    kt_ref[...] = u[:, aw:2 * aw].reshape(bt, tt, aw)
    vt_ref[...] = u[:, 2 * aw:3 * aw].reshape(bt, tt, aw)
    c0 += 2 * mw
    u = proj(c0, 2 * mw)
    vm_ref[...] = u[:, 0:mw].astype(BF16).reshape(bt, tt, mw)
    om_ref[...] = u[:, mw:2 * mw].reshape(bt, tt, mw)
    c0 += 2 * mw
    for j in range(0, 2 * d, d):
        gm_ref[:, :, j:j + d] = proj(c0 + j, d).reshape(bt, tt, d)
    ug = jnp.dot(hb_scr[...], wg_ref[0], preferred_element_type=F32) + bif_ref[...]
    col = lax.broadcasted_iota(jnp.int32, ug.shape, 1)
    gates_ref[...] = jnp.where(col >= M_HEADS, _log_sigmoid(ug), ug).reshape(bt, tt, LANES)


def _inproj_call(x, mod, layer, row0, g_mix, w_main, w_gates, b_if, conv_w, conv_b, conv_init, bt, tt):
    b, t, d = x.shape
    nb, nt = b // bt, t // tt
    keep = min(N_PAST_CHUNKS * CHUNK, t)
    assert b % bt == 0 and t % tt == 0 and keep % tt == 0
    t_keep0 = nt - keep // tt
    n_main = w_main.shape[2]
    aw, mw = ATT_WIDTH, M_WIDTH

    def wspec(shape):
        return _const_spec((1,) + shape, lambda i, j: (layer, 0, 0))

    def tok(width):
        return pl.BlockSpec((bt, tt, width), lambda i, j: (i, j, 0))

    def modspec(k):
        return pl.BlockSpec((1, bt, 1, 1, d), lambda i, j: (layer, row0 // bt + i, k, 0, 0))

    def const2(shape):
        return _const_spec(shape, lambda i, j: (0, 0))

    keep_spec = pl.BlockSpec((bt, tt, aw), lambda i, j: (i, jnp.maximum(j - t_keep0, 0), 0))
    per_batch8 = pl.BlockSpec((bt, SUBLANES, 2 * mw), lambda i, j: (i, 0, 0))
    outs = pl.pallas_call(
        functools.partial(_inproj_kernel, bt=bt, tt=tt),
        grid=(nb, nt),
        in_specs=[
            tok(d), modspec(0), modspec(1), const2((1, d)),
            wspec((d, n_main)), wspec((d, LANES)), const2((1, LANES)),
            const2((CONV_W, 2 * mw)), const2((1, 2 * mw)), per_batch8,
        ],
        out_specs=[tok(3 * aw), keep_spec, keep_spec, tok(2 * mw), tok(mw), tok(mw), tok(2 * d),
                   tok(LANES), per_batch8],
        out_shape=[
            jax.ShapeDtypeStruct((b, t, 3 * aw), BF16),
            jax.ShapeDtypeStruct((b, keep, aw), F32),
            jax.ShapeDtypeStruct((b, keep, aw), F32),
            jax.ShapeDtypeStruct((b, t, 2 * mw), BF16),
            jax.ShapeDtypeStruct((b, t, mw), BF16),
            jax.ShapeDtypeStruct((b, t, mw), F32),
            jax.ShapeDtypeStruct((b, t, 2 * d), F32),
            jax.ShapeDtypeStruct((b, t, LANES), F32),
            jax.ShapeDtypeStruct((b, SUBLANES, 2 * mw), F32),
        ],
        scratch_shapes=[pltpu.VMEM((bt, SUBLANES + tt, 2 * mw), F32), pltpu.VMEM((bt * tt, d), BF16)],
        compiler_params=_params(("parallel", "arbitrary")),
        name="inproj",
    )(x, mod, mod, g_mix, w_main, w_gates, b_if, conv_w, conv_b, conv_init)
    return outs


def _bias_kernel(rev_ref, o_ref):
    w = rev_ref.shape[-1]
    x = jnp.broadcast_to(rev_ref[0, 0], (CHUNK, w))
    r = pltpu.roll(x, w - (CHUNK - 1), 1, stride=1, stride_axis=0)
    o_ref[0, 0] = r[:, 0:BAND] * LOG2E


def _bias_call(rel_bias):
    depth, nh, rel = rel_bias.shape
    n_ext = BAND + CHUNK - 1
    w = -(-n_ext // LANES) * LANES
    ext = jnp.concatenate([rel_bias, jnp.broadcast_to(rel_bias[:, :, rel - 1:], (depth, nh, n_ext - rel))], axis=2)
    rev = jnp.pad(ext[:, :, ::-1], ((0, 0), (0, 0), (0, w - n_ext))).reshape(depth, nh, 1, w)
    out = pl.pallas_call(
        _bias_kernel,
        grid=(depth, nh),
        in_specs=[pl.BlockSpec((1, 1, 1, w), lambda l, h: (l, h, 0, 0))],
        out_specs=pl.BlockSpec((1, 1, CHUNK, BAND), lambda l, h: (l, h, 0, 0)),
        out_shape=jax.ShapeDtypeStruct((depth, nh, CHUNK, BAND), F32),
        compiler_params=_params(("parallel", "parallel")),
        name="rel_bias",
    )(rev)
    return out.reshape(depth, N_PAIRS, 2 * CHUNK, BAND)


def _pair_rows(qp):
    lane = lax.broadcasted_iota(jnp.int32, qp.shape, 1)
    zero = jnp.zeros_like(qp)
    return jnp.concatenate([jnp.where(lane < ATT_HEAD_DIM, qp, zero),
                            jnp.where(lane >= ATT_HEAD_DIM, qp, zero)], axis=0)


def _pair_merge(o, rows):
    lane = lax.broadcasted_iota(jnp.int32, (rows, LANES), 1)
    return jnp.where(lane < ATT_HEAD_DIM, o[0:rows], o[rows:2 * rows])


def _nt_dot(a, b):
    return lax.dot_general(a, b, (((1,), (1,)), ((), ())), preferred_element_type=F32)


def _attn_prompt_kernel(q_ref, kp_ref, kc_ref, vp_ref, vc_ref, bias_ref, o_ref, kband, vband, *, unroll):
    j = pl.program_id(1)
    rows = kc_ref.shape[1]
    past = N_PAST_CHUNKS * CHUNK
    assert rows == past

    @pl.when(j == 0)
    def _():
        kband[0:past, :] = jnp.zeros((past, ATT_WIDTH), BF16)
        vband[0:past, :] = jnp.zeros((past, ATT_WIDTH), BF16)

    @pl.when(j > 0)
    def _():
        kband[0:past, :] = kp_ref[0]
        vband[0:past, :] = vp_ref[0]

    kband[past:past + rows, :] = kc_ref[0]
    vband[past:past + rows, :] = vc_ref[0]

    def chunks(i, masked):
        units = [(pl.multiple_of((i * unroll + u) * CHUNK, CHUNK), i * unroll + u, p)
                 for u in range(unroll) for p in range(N_PAIRS)]
        scores = []
        for r0, ci, p in units:
            ls = slice(p * LANES, (p + 1) * LANES)
            qs = _pair_rows(q_ref[0, pl.ds(r0, CHUNK), ls])
            scores.append(_nt_dot(qs, kband[pl.ds(r0, BAND), ls]) + bias_ref[0, p])
        for (r0, ci, p), s in zip(units, scores):
            ls = slice(p * LANES, (p + 1) * LANES)
            if masked:
                col = lax.broadcasted_iota(jnp.int32, s.shape, 1)
                s = jnp.where(col >= (N_PAST_CHUNKS - ci) * CHUNK, s, NEG)
            mx = jnp.max(s, axis=-1, keepdims=True)
            e = jnp.exp2(s - mx)
            den = jnp.sum(e, axis=-1, keepdims=True)
            o = jnp.dot(e.astype(BF16), vband[pl.ds(r0, BAND), ls], preferred_element_type=F32) / den
            o_ref[0, pl.ds(r0, CHUNK), ls] = _pair_merge(o, CHUNK).astype(BF16)

    n_iter = rows // (CHUNK * unroll)

    @pl.when(j == 0)
    def _():
        lax.fori_loop(0, n_iter, lambda i, c: (chunks(i, True), c)[1], 0)

    @pl.when(j > 0)
    def _():
        lax.fori_loop(0, n_iter, lambda i, c: (chunks(i, False), c)[1], 0)


def _attn_prompt_call(qkv, bias, layer):
    b, t, _ = qkv.shape
    rows = N_PAST_CHUNKS * CHUNK
    assert t % rows == 0
    aw = ATT_WIDTH

    def cur(cb):
        return pl.BlockSpec((1, rows, aw), lambda i, j: (i, j, cb))

    def prev(cb):
        return pl.BlockSpec((1, rows, aw), lambda i, j: (i, jnp.maximum(j - 1, 0), cb))

    return pl.pallas_call(
        functools.partial(_attn_prompt_kernel, unroll=ATTN_UNROLL),
        grid=(b, t // rows),
        in_specs=[cur(0), prev(1), cur(1), prev(2), cur(2),
                  _const_spec((1, N_PAIRS, 2 * CHUNK, BAND), lambda i, j: (layer, 0, 0, 0))],
        out_specs=pl.BlockSpec((1, rows, aw), lambda i, j: (i, j, 0)),
        out_shape=jax.ShapeDtypeStruct((b, t, aw), BF16),
        scratch_shapes=[pltpu.VMEM((2 * rows, aw), BF16), pltpu.VMEM((2 * rows, aw), BF16)],
        compiler_params=_params(("parallel", "parallel")),
        name="attn_prompt",
    )(qkv, qkv, qkv, qkv, qkv, bias)


def _attn_sample_kernel(q_ref, kn_ref, vn_ref, kc_ref, vc_ref, bias_ref, o_ref, *, bb):
    tq = q_ref.shape[1]
    pc = kc_ref.shape[3]
    for bi in range(bb):
        scores = []
        for p in range(N_PAIRS):
            ls = slice(p * LANES, (p + 1) * LANES)
            qs = _pair_rows(q_ref[bi, :, ls])
            bias = jnp.concatenate([bias_ref[0, p, 0:tq, :], bias_ref[0, p, CHUNK:CHUNK + tq, :]], axis=0)
            s1 = jnp.dot(qs, kc_ref[0, bi, ls, :].astype(BF16), preferred_element_type=F32)
            scores.append((s1 + bias[:, 0:pc], _nt_dot(qs, kn_ref[bi, :, ls]) + bias[:, pc:pc + tq]))
        for p, (s1, s2) in enumerate(scores):
            ls = slice(p * LANES, (p + 1) * LANES)
            mx = jnp.maximum(jnp.max(s1, axis=-1, keepdims=True), jnp.max(s2, axis=-1, keepdims=True))
            e1 = jnp.exp2(s1 - mx)
            e2 = jnp.exp2(s2 - mx)
            den = jnp.sum(e1, axis=-1, keepdims=True) + jnp.sum(e2, axis=-1, keepdims=True)
            o = (_nt_dot(e1.astype(BF16), vc_ref[0, bi, ls, :].astype(BF16))
                 + jnp.dot(e2.astype(BF16), vn_ref[bi, :, ls], preferred_element_type=F32)) / den
            o_ref[bi, :, ls] = _pair_merge(o, tq).astype(BF16)


def _attn_sample_call(qkv, cache_k, cache_v, bias, layer, bb):
    b, tq, _ = qkv.shape
    _, _, aw, pc = cache_k.shape
    assert b % bb == 0 and pc == N_PAST_CHUNKS * CHUNK and tq <= CHUNK and aw == ATT_WIDTH

    def new(cb):
        return pl.BlockSpec((bb, tq, aw), lambda i: (i, 0, cb))

    cache = pl.BlockSpec((1, bb, aw, pc), lambda i: (layer, i, 0, 0))
    return pl.pallas_call(
        functools.partial(_attn_sample_kernel, bb=bb),
        grid=(b // bb,),
        in_specs=[new(0), new(1), new(2), cache, cache,
                  _const_spec((1, N_PAIRS, 2 * CHUNK, BAND), lambda i: (layer, 0, 0, 0))],
        out_specs=pl.BlockSpec((bb, tq, aw), lambda i: (i, 0, 0)),
        out_shape=jax.ShapeDtypeStruct((b, tq, aw), BF16),
        compiler_params=_params(("parallel",)),
        name="attn_sample",
    )(qkv, qkv, qkv, cache_k, cache_v, bias)


def _transpose_rows(a):
    r = a.shape[0]
    pad = -r % LANES
    if pad:
        a = jnp.concatenate([a, jnp.zeros((pad, a.shape[1]), a.dtype)], axis=0)
    return a.T[:, 0:r]


def _transpose_cols(a):
    c = a.shape[1]
    pad = -c % LANES
    if pad:
        a = jnp.concatenate([a, jnp.zeros((a.shape[0], pad), a.dtype)], axis=1)
    return a.T[0:c, :]


def _block_cumsum(tri, g):
    g1 = g.astype(BF16)
    r1 = g - g1.astype(F32)
    g2 = r1.astype(BF16)
    g3 = (r1 - g2.astype(F32)).astype(BF16)
    return (jnp.dot(tri, g1, preferred_element_type=F32) + jnp.dot(tri, g2, preferred_element_type=F32)
            + jnp.dot(tri, g3, preferred_element_type=F32))


def _mlstm_kernel(qkm_ref, vm_ref, om_ref, gates_ref, gain_ref, tri_ref, c0_ref, n0_ref, m0_ref,
                  ym_ref, c1_ref, n1_ref, m1_ref, cn_scr, m_scr, *, bb, blk, nblk):
    tb = pl.program_id(1)
    hd = M_HEAD_DIM
    ext = cn_scr.shape[2] - hd

    def last_lane(r):
        return jnp.broadcast_to(r[:, blk - 1:blk], r.shape)

    ext_row0 = lax.broadcasted_iota(jnp.int32, (ext, hd), 0) == 0

    @pl.when(tb == 0)
    def _():
        for e in range(bb):
            for h in range(M_HEADS):
                cn_scr[e, h, 0:hd, :] = c0_ref[e, h].T
                cn_scr[e, h, hd:hd + ext, :] = jnp.where(ext_row0, n0_ref[e, h], 0.0)
        m_scr[...] = m0_ref[...]

    srow = lax.broadcasted_iota(jnp.int32, (blk, blk), 0)
    tcol = lax.broadcasted_iota(jnp.int32, (blk, blk), 1)
    causal = srow <= tcol
    ones_ext = jnp.where(lax.broadcasted_iota(jnp.int32, (ext, blk), 0) == 0, 1.0, 0.0)

    gt_all, bt_all, c_all = [], [], []
    for e in range(bb):
        g = gates_ref[e]
        b = _block_cumsum(tri_ref[...], g)
        gt_all.append(_transpose_rows(g))
        bt_all.append(_transpose_rows(b))
        c_all.append(g[:, 0:M_HEADS] - b[:, M_HEADS:2 * M_HEADS])

    units = [(e, h, bi) for e in range(bb) for h in range(M_HEADS) for bi in range(nblk)]
    vt_heads = {(e, h): _transpose_rows(vm_ref[e, :, h * hd:(h + 1) * hd].astype(F32))
                for e in range(bb) for h in range(M_HEADS)}
    scores = {}
    for e, h, bi in units:
        rs = slice(bi * blk, (bi + 1) * blk)
        q = qkm_ref[e, rs, h * hd:(h + 1) * hd]
        k = qkm_ref[e, rs, M_WIDTH + h * hd:M_WIDTH + (h + 1) * hd]
        b_row = bt_all[e][M_HEADS + h:M_HEADS + h + 1, rs]
        dmat = jnp.where(causal, b_row + c_all[e][rs, h:h + 1], NEG)
        m_in = jnp.max(dmat, axis=0, keepdims=True)
        scores[e, h, bi] = (m_in, _nt_dot(k, q) * jnp.exp(dmat - m_in))
    intra = {}
    for e, h, bi in units:
        rs = slice(bi * blk, (bi + 1) * blk)
        k = qkm_ref[e, rs, M_WIDTH + h * hd:M_WIDTH + (h + 1) * hd]
        li_row, b_row = gt_all[e][h:h + 1, rs], bt_all[e][M_HEADS + h:M_HEADS + h + 1, rs]
        m_in, smat = scores[e, h, bi]
        vt = jnp.concatenate([vt_heads[e, h][:, rs], ones_ext], axis=0)
        den_i = jnp.sum(smat, axis=0, keepdims=True)
        num_i = jnp.dot(vt[0:hd].astype(BF16), smat.astype(BF16), preferred_element_type=F32)
        b_last = last_lane(b_row)
        m_u = last_lane(m_in)
        w_u = jnp.exp(b_last - b_row + li_row - m_u)
        upd = jnp.dot((vt * w_u).astype(BF16), k, preferred_element_type=F32)
        intra[e, h, bi] = (den_i, num_i, b_last, m_u, upd)
    for bi in range(nblk):
        rs = slice(bi * blk, (bi + 1) * blk)
        for e, h in [(e, h) for e in range(bb) for h in range(M_HEADS)]:
            ls = slice(h * hd, (h + 1) * hd)
            q = qkm_ref[e, rs, ls]
            b_row = bt_all[e][M_HEADS + h:M_HEADS + h + 1, rs]
            m_in = scores[e, h, bi][0]
            den_i, num_i, b_last, m_u, upd = intra[e, h, bi]
            m_prev = m_scr[e, h][:, 0:blk]
            cn_prev = cn_scr[e, h]
            a = b_row + m_prev
            m_t = jnp.maximum(a, m_in)
            alpha = jnp.exp(m_in - m_t)
            beta = jnp.exp(a - m_t)
            inter = _nt_dot(cn_prev.astype(BF16), q)
            num = alpha * num_i + beta * inter[0:hd]
            den = alpha * den_i + beta * inter[hd:hd + 1]
            hh = num / jnp.maximum(jnp.abs(den), jnp.exp(-m_t))
            hh = hh * lax.rsqrt(jnp.mean(hh * hh, axis=0, keepdims=True) + EPS)
            y = _transpose_cols(hh) * gain_ref[:, ls] * _sigmoid(om_ref[e, rs, ls])
            ym_ref[e, rs, ls] = y.astype(BF16)
            m_new = last_lane(m_t)
            g_state = jnp.exp(b_last + m_prev - m_new)[:, 0:1]
            g_upd = jnp.exp(m_u - m_new)[:, 0:1]
            cn_scr[e, h] = g_state * cn_prev + g_upd * upd
            m_scr[e, h] = jnp.broadcast_to(m_new[:, 0:1], (1, LANES))

    @pl.when(tb == pl.num_programs(1) - 1)
    def _():
        for e in range(bb):
            for h in range(M_HEADS):
                c1_ref[e, h] = cn_scr[e, h, 0:hd, :].T
                n1_ref[e, h] = cn_scr[e, h, hd:hd + 1, :]
        m1_ref[...] = m_scr[...]


def _block_tril(rt, blk):
    r = jnp.arange(rt)
    return ((r[:, None] // blk == r[None, :] // blk) & (r[None, :] <= r[:, None])).astype(BF16)


def _mlstm_call(qkm, vm, om, gates, gain, c0, n0, m0, bb, blk, nblk):
    b, t, _ = qkm.shape
    rt = blk * nblk
    assert t % rt == 0 and b % bb == 0
    mw, hd, nh = M_WIDTH, M_HEAD_DIM, M_HEADS

    def tok(width):
        return pl.BlockSpec((bb, rt, width), lambda i, j: (i, j, 0))

    cspec = pl.BlockSpec((bb, nh, hd, hd), lambda i, j: (i, 0, 0, 0))
    vspec = pl.BlockSpec((bb, nh, 1, hd), lambda i, j: (i, 0, 0, 0))
    return pl.pallas_call(
        functools.partial(_mlstm_kernel, bb=bb, blk=blk, nblk=nblk),
        grid=(b // bb, t // rt),
        in_specs=[tok(2 * mw), tok(mw), tok(mw), tok(LANES),
                  _const_spec((1, mw), lambda i, j: (0, 0)), _const_spec((rt, rt), lambda i, j: (0, 0)),
                  cspec, vspec, vspec],
        out_specs=[tok(mw), cspec, vspec, vspec],
        out_shape=[
            jax.ShapeDtypeStruct((b, t, mw), BF16),
            jax.ShapeDtypeStruct((b, nh, hd, hd), F32),
            jax.ShapeDtypeStruct((b, nh, 1, hd), F32),
            jax.ShapeDtypeStruct((b, nh, 1, hd), F32),
        ],
        scratch_shapes=[pltpu.VMEM((bb, nh, hd + STATE_EXT_ROWS, hd), F32), pltpu.VMEM((bb, nh, 1, hd), F32)],
        compiler_params=_params(("parallel", "arbitrary")),
        name="mlstm",
    )(qkm, vm, om, gates, gain, _block_tril(rt, blk), c0, n0, m0)


def _ffn_kernel(x_ref, ya_ref, ym_ref, gm_ref, gt1_ref, sh2_ref, sc2_ref, gt2_ref, gffn_ref, gfin_ref,
                wba_ref, wbm_ref, wo_ref, wgu_ref, wd_ref, o_ref, act_scr, *, bt, tt, final):
    d = x_ref.shape[-1]
    m = bt * tt
    dff = wd_ref.shape[1]
    x = x_ref[...]
    pa = jnp.dot(ya_ref[...].reshape(m, ATT_WIDTH), wba_ref[0], preferred_element_type=F32)
    pm = jnp.dot(ym_ref[...].reshape(m, M_WIDTH), wbm_ref[0], preferred_element_type=F32)
    ga = gm_ref[:, :, 0:d].reshape(m, d)
    gmm = gm_ref[:, :, d:2 * d].reshape(m, d)
    merged = _sigmoid(ga) * pa + _sigmoid(gmm) * pm
    y1 = jnp.dot(merged.astype(BF16), wo_ref[0], preferred_element_type=F32)
    x1 = x + gt1_ref[0, :, 0] * y1.reshape(bt, tt, d)
    ms = jnp.mean(x1 * x1, axis=-1, keepdims=True)
    h2 = x1 * lax.rsqrt(ms + EPS) * gffn_ref[...]
    h2 = (h2 * (1.0 + sc2_ref[0, :, 0]) + sh2_ref[0, :, 0]).reshape(m, d).astype(BF16)
    cw = 256
    for c in range(0, dff, cw):
        g = jnp.dot(h2, wgu_ref[0, :, c:c + cw], preferred_element_type=F32)
        up = jnp.dot(h2, wgu_ref[0, :, dff + c:dff + c + cw], preferred_element_type=F32)
        act_scr[:, c:c + cw] = (g * _sigmoid(g) * up).astype(BF16)
    y2 = jnp.dot(act_scr[...], wd_ref[0], preferred_element_type=F32)
    x2 = x1 + gt2_ref[0, :, 0] * y2.reshape(bt, tt, d)
    if final:
        ms2 = jnp.mean(x2 * x2, axis=-1, keepdims=True)
        x2 = x2 * lax.rsqrt(ms2 + EPS) * gfin_ref[...]
    o_ref[...] = x2


def _ffn_call(x, ya, ym, gm, mod, layer, row0, g_ffn, g_final, w_br_att, w_br_mlstm, w_out, w_gate_up,
              w_down, bt, tt, final):
    b, t, d = x.shape
    dff = w_down.shape[1]
    assert b % bt == 0 and t % tt == 0 and dff % 256 == 0

    def wspec(shape):
        return _const_spec((1,) + shape, lambda i, j: (layer, 0, 0))

    def tok(width):
        return pl.BlockSpec((bt, tt, width), lambda i, j: (i, j, 0))

    def modspec(k):
        return pl.BlockSpec((1, bt, 1, 1, d), lambda i, j: (layer, row0 // bt + i, k, 0, 0))

    def const2(shape):
        return _const_spec(shape, lambda i, j: (0, 0))

    return pl.pallas_call(
        functools.partial(_ffn_kernel, bt=bt, tt=tt, final=final),
        grid=(b // bt, t // tt),
        in_specs=[tok(d), tok(ATT_WIDTH), tok(M_WIDTH), tok(2 * d),
                  modspec(2), modspec(3), modspec(4), modspec(5), const2((1, d)), const2((1, d)),
                  wspec((ATT_WIDTH, d)), wspec((M_WIDTH, d)), wspec((d, d)), wspec((d, 2 * dff)),
                  wspec((dff, d))],
        out_specs=tok(d),
        out_shape=jax.ShapeDtypeStruct((b, t, d), F32),
        scratch_shapes=[pltpu.VMEM((bt * tt, dff), BF16)],
        compiler_params=_params(("parallel", "parallel")),
        name="merge_ffn",
    )(x, ya, ym, gm, mod, mod, mod, mod, g_ffn, g_final, w_br_att, w_br_mlstm, w_out, w_gate_up, w_down)


def _layer(x, mod, layer, row0, cache, conv_init, c0, n0, m0, wts, bt, tt, blk, nblk, final):
    (g_mix, w_main, w_gates, b_if, conv_w, conv_b, bias, mh_gain, w_br_att, w_br_mlstm, w_out, g_ffn,
     w_gate_up, w_down, g_final) = wts
    b, t, _ = x.shape
    qkv, kt, vt, qkm, vm, om, gm, gates, ctail = _inproj_call(
        x, mod, layer, row0, g_mix, w_main, w_gates, b_if, conv_w, conv_b, conv_init, bt, tt)
    if cache is None:
        ya = _attn_prompt_call(qkv, bias, layer)
    else:
        ya = _attn_sample_call(qkv, cache[0], cache[1], bias, layer, bt)
    mb = 1 if nblk > 1 else SAMPLE_MLSTM_BATCH_TILE
    ym, c1, n1, m1 = _mlstm_call(qkm, vm, om, gates, mh_gain, c0, n0, m0, mb, blk, nblk)
    x = _ffn_call(x, ya, ym, gm, mod, layer, row0, g_ffn, g_final, w_br_att, w_br_mlstm, w_out, w_gate_up,
                  w_down, bt, tt, final)
    keep = kt.shape[1]
    state = (kt.reshape(b, keep, ATT_HEADS, ATT_HEAD_DIM), vt.reshape(b, keep, ATT_HEADS, ATT_HEAD_DIM),
             ctail[:, SUBLANES - (CONV_W - 1):, :], c1, n1[:, :, 0, :], m1[:, :, 0, 0])
    return x, state


def kernel(x_prompt, x_sample, cache_k, cache_v, state_conv, state_C, state_n, state_m, c_prompt, c_sample,
           w_ada, b_ada, g_mix, w_in, b_if, conv_w, conv_b, rel_bias, mh_gain, w_br_att, w_br_mlstm, w_out,
           g_ffn, w_gate_up, w_down, g_final):
    depth = w_ada.shape[0]
    bp, tp, d = x_prompt.shape
    bs, ts, _ = x_sample.shape
    assert tp % PROMPT_ROWS == 0 and tp % CHUNK == 0 and bs % SAMPLE_BATCH_TILE == 0 and ts % CHUNK != 0
    mw = M_WIDTH

    mod = _ada_call(jnp.concatenate([c_sample, c_prompt], axis=0), w_ada, b_ada)
    mod = mod.reshape(depth, bs + bp, 6, 1, d)

    bias = _bias_call(rel_bias)
    pc = cache_k.shape[2]
    caches = tuple(jnp.transpose(c, (0, 1, 3, 4, 2)).reshape(depth, bs, ATT_WIDTH, pc) for c in (cache_k, cache_v))

    pad3 = SUBLANES - (CONV_W - 1)
    zero_conv = jnp.zeros((bp, SUBLANES, 2 * mw), F32)
    zero_c = jnp.zeros((bp, M_HEADS, M_HEAD_DIM, M_HEAD_DIM), F32)
    zero_v = jnp.zeros((bp, M_HEADS, 1, M_HEAD_DIM), F32)

    w_main, w_gates = _wprep_call(w_in)
    wb_att, wb_mlstm, wb_out = w_br_att.astype(BF16), w_br_mlstm.astype(BF16), w_out.astype(BF16)
    wb_gate_up, wb_down = w_gate_up.astype(BF16), w_down.astype(BF16)

    xp, xs = x_prompt, x_sample
    outs_p, outs_s = [], []
    for l in range(depth):
        bif = jnp.pad(b_if[l], (0, LANES - 2 * M_HEADS)).reshape(1, LANES)
        wts = (g_mix[l].reshape(1, d), w_main, w_gates, bif, conv_w[l], conv_b[l].reshape(1, 2 * mw),
               bias, mh_gain[l].reshape(1, mw), wb_att, wb_mlstm, wb_out, g_ffn[l].reshape(1, d), wb_gate_up,
               wb_down, g_final.reshape(1, d))
        final = l == depth - 1
        xp, st_p = _layer(xp, mod, l, bs, None, zero_conv, zero_c, zero_v, zero_v, wts,
                          1, PROMPT_ROWS, MLSTM_BLOCK, MLSTM_TILE_ROWS // MLSTM_BLOCK, final)
        conv_init = jnp.pad(state_conv[l], ((0, 0), (pad3, 0), (0, 0)))
        m0 = jnp.broadcast_to(state_m[l][:, :, None, None], (bs, M_HEADS, 1, M_HEAD_DIM))
        xs, st_s = _layer(xs, mod, l, 0, caches, conv_init, state_C[l],
                          state_n[l][:, :, None, :], m0, wts, SAMPLE_BATCH_TILE, ts, ts, 1, final)
        outs_p.append(st_p)
        outs_s.append(st_s)

    def stk(outs, i):
        return jnp.stack([o[i] for o in outs], axis=0)

    return (xp, xs,
            stk(outs_p, 0), stk(outs_p, 1), stk(outs_p, 2), stk(outs_p, 3), stk(outs_p, 4), stk(outs_p, 5),
            stk(outs_s, 0), stk(outs_s, 1), stk(outs_s, 2), stk(outs_s, 3), stk(outs_s, 4), stk(outs_s, 5))
```

```python
import functools

import jax
import jax.numpy as jnp
from jax import lax
from jax.experimental import pallas as pl
from jax.experimental.pallas import tpu as pltpu

F32 = jnp.float32
BF16 = jnp.bfloat16

CHUNK = 64
N_PAST_CHUNKS = 8
ATT_HEADS = 8
ATT_HEAD_DIM = 64
ATT_WIDTH = ATT_HEADS * ATT_HEAD_DIM
MAX_REL = 256
M_HEADS = 4
M_HEAD_DIM = 128
M_WIDTH = M_HEADS * M_HEAD_DIM
CONV_W = 4
EPS = 1e-6
NEG = -1e30

LANES = 128
SUBLANES = 8
VMEM_LIMIT = 56 * 1024 * 1024
PROMPT_ROWS = 512
SAMPLE_BATCH_TILE = 8
MLSTM_BATCH_TILE = 4
BAND = (N_PAST_CHUNKS + 1) * CHUNK
N_PAIRS = ATT_HEADS // 2
ATTN_UNROLL = 8
MLSTM_BLOCK = 128
MLSTM_TILE_ROWS = 512
STATE_EXT_ROWS = 16
LOG2E = 1.4426950408889634


def _sigmoid(x):
    return 1.0 / (1.0 + jnp.exp2(x * (-LOG2E)))


def _log_sigmoid(x):
    return jnp.minimum(x, 0.0) - jnp.log1p(jnp.exp(-jnp.abs(x)))


def _const_spec(shape, index_map):
    return pl.BlockSpec(shape, index_map, pipeline_mode=pl.Buffered(1))


def _params(sem):
    return pltpu.CompilerParams(dimension_semantics=sem, vmem_limit_bytes=VMEM_LIMIT)


def _ada_kernel(c_ref, w_ref, b_ref, o_ref):
    c = c_ref[...]
    a = (c * _sigmoid(c)).astype(BF16)
    o_ref[0] = jnp.dot(a, w_ref[0].astype(BF16), preferred_element_type=F32) + b_ref[0]


def _ada_call(c_all, w_ada, b_ada):
    depth, d, n = w_ada.shape
    r = c_all.shape[0]
    tn = 1536
    assert n % tn == 0
    return pl.pallas_call(
        _ada_kernel,
        grid=(depth, n // tn),
        in_specs=[
            pl.BlockSpec((r, d), lambda l, j: (0, 0)),
            pl.BlockSpec((1, d, tn), lambda l, j: (l, 0, j)),
            pl.BlockSpec((1, 1, tn), lambda l, j: (l, 0, j)),
        ],
        out_specs=pl.BlockSpec((1, r, tn), lambda l, j: (l, 0, j)),
        out_shape=jax.ShapeDtypeStruct((depth, r, n), F32),
        compiler_params=_params(("parallel", "parallel")),
        name="ada_mod",
    )(c_all, w_ada, b_ada.reshape(depth, 1, n))


def _wprep_kernel(wt_ref, main_ref, gates_ref):
    n_att = 3 * ATT_WIDTH + 4 * M_WIDTH
    n_g = 2 * M_HEADS
    n_main = main_ref.shape[2]
    step = 512
    for c0 in range(0, n_main, step):
        r0 = c0 if c0 < n_att else c0 + n_g
        main_ref[0, :, c0:c0 + step] = wt_ref[0, r0:r0 + step, :].T.astype(BF16)
    slab = wt_ref[0, n_att:n_att + LANES, :].T
    lane = lax.broadcasted_iota(jnp.int32, slab.shape, 1)
    gates_ref[0] = jnp.where(lane < n_g, slab, 0.0).astype(BF16)


def _wprep_call(w_in):
    depth, d, n = w_in.shape
    n_main = n - 2 * M_HEADS
    assert n_main % 512 == 0 and (3 * ATT_WIDTH + 4 * M_WIDTH) % 512 == 0
    return pl.pallas_call(
        _wprep_kernel,
        grid=(depth,),
        in_specs=[pl.BlockSpec((1, n, d), lambda l: (l, 0, 0), pipeline_mode=pl.Buffered(1))],
        out_specs=[pl.BlockSpec((1, d, n_main), lambda l: (l, 0, 0)),
                   pl.BlockSpec((1, d, LANES), lambda l: (l, 0, 0))],
        out_shape=[jax.ShapeDtypeStruct((depth, d, n_main), BF16), jax.ShapeDtypeStruct((depth, d, LANES), BF16)],
        compiler_params=_params(("parallel",)),
        name="w_in_prep",
    )(jnp.swapaxes(w_in, 1, 2))


def _inproj_kernel(x_ref, sh_ref, sc_ref, g_ref, w_ref, wg_ref, bif_ref, cw_ref, cb_ref, cinit_ref,
                   qkv_ref, kt_ref, vt_ref, qkm_ref, vm_ref, om_ref, gm_ref, gates_ref, ctail_ref,
                   conv_scr, hb_scr, *, bt, tt):
    t = pl.program_id(1)
    d = x_ref.shape[-1]
    m = bt * tt
    x = x_ref[...]
    ms = jnp.mean(x * x, axis=-1, keepdims=True)
    h = x * lax.rsqrt(ms + EPS) * g_ref[...]
    h = h * (1.0 + sc_ref[0, :, 0]) + sh_ref[0, :, 0]
    hb_scr[...] = h.reshape(m, d).astype(BF16)

    def proj(c0, width):
        return jnp.dot(hb_scr[...], w_ref[0, :, c0:c0 + width], preferred_element_type=F32)

    aw = ATT_WIDTH
    mw = M_WIDTH
    c0 = 3 * aw

    @pl.when(t == 0)
    def _():
        conv_scr[:, 0:SUBLANES, :] = cinit_ref[...]

    conv_scr[:, SUBLANES:SUBLANES + tt, :] = proj(c0, 2 * mw).reshape(bt, tt, 2 * mw)
    lo = SUBLANES - (CONV_W - 1)
    acc = conv_scr[:, lo:lo + tt, :] * cw_ref[0:1, :]
    for j in range(1, CONV_W):
        acc = acc + conv_scr[:, lo + j:lo + j + tt, :] * cw_ref[j:j + 1, :]
    acc = acc + cb_ref[...]
    act = acc * _sigmoid(acc)
    qkm_ref[:, :, 0:mw] = act[:, :, 0:mw].astype(BF16)
    qkm_ref[:, :, mw:2 * mw] = (act[:, :, mw:2 * mw] * (M_HEAD_DIM ** -0.5)).astype(BF16)
    tail = conv_scr[:, tt:tt + SUBLANES, :]
    ctail_ref[...] = tail
    conv_scr[:, 0:SUBLANES, :] = tail
    u = proj(0, 3 * aw)
    qkv_ref[:, :, 0:aw] = (u[:, 0:aw] * (LOG2E * ATT_HEAD_DIM ** -0.5)).astype(BF16).reshape(bt, tt, aw)
    qkv_ref[:, :, aw:3 * aw] = u[:, aw:3 * aw].astype(BF16).reshape(bt, tt, 2 * aw)
    kt_ref[...] = u[:, aw:2 * aw].reshape(bt, tt, aw)
    vt_ref[...] = u[:, 2 * aw:3 * aw].reshape(bt, tt, aw)
    c0 += 2 * mw
    u = proj(c0, 2 * mw)
    vm_ref[...] = u[:, 0:mw].astype(BF16).reshape(bt, tt, mw)
    om_ref[...] = u[:, mw:2 * mw].reshape(bt, tt, mw)
    c0 += 2 * mw
    for j in range(0, 2 * d, d):
        gm_ref[:, :, j:j + d] = proj(c0 + j, d).reshape(bt, tt, d)
    ug = jnp.dot(hb_scr[...], wg_ref[0], preferred_element_type=F32) + bif_ref[...]
    col = lax.broadcasted_iota(jnp.int32, ug.shape, 1)
    gates_ref[...] = jnp.where(col >= M_HEADS, _log_sigmoid(ug), ug).reshape(bt, tt, LANES)


def _inproj_call(x, mod, layer, row0, g_mix, w_main, w_gates, b_if, conv_w, conv_b, conv_init, bt, tt):
    b, t, d = x.shape
    nb, nt = b // bt, t // tt
    keep = min(N_PAST_CHUNKS * CHUNK, t)
    assert b % bt == 0 and t % tt == 0 and keep % tt == 0
    t_keep0 = nt - keep // tt
    n_main = w_main.shape[2]
    aw, mw = ATT_WIDTH, M_WIDTH

    def wspec(shape):
        return _const_spec((1,) + shape, lambda i, j: (layer, 0, 0))

    def tok(width):
        return pl.BlockSpec((bt, tt, width), lambda i, j: (i, j, 0))

    def modspec(k):
        return pl.BlockSpec((1, bt, 1, 1, d), lambda i, j: (layer, row0 // bt + i, k, 0, 0))

    def const2(shape):
        return _const_spec(shape, lambda i, j: (0, 0))

    keep_spec = pl.BlockSpec((bt, tt, aw), lambda i, j: (i, jnp.maximum(j - t_keep0, 0), 0))
    per_batch8 = pl.BlockSpec((bt, SUBLANES, 2 * mw), lambda i, j: (i, 0, 0))
    outs = pl.pallas_call(
        functools.partial(_inproj_kernel, bt=bt, tt=tt),
        grid=(nb, nt),
        in_specs=[
            tok(d), modspec(0), modspec(1), const2((1, d)),
            wspec((d, n_main)), wspec((d, LANES)), const2((1, LANES)),
            const2((CONV_W, 2 * mw)), const2((1, 2 * mw)), per_batch8,
        ],
        out_specs=[tok(3 * aw), keep_spec, keep_spec, tok(2 * mw), tok(mw), tok(mw), tok(2 * d),
                   tok(LANES), per_batch8],
        out_shape=[
            jax.ShapeDtypeStruct((b, t, 3 * aw), BF16),
            jax.ShapeDtypeStruct((b, keep, aw), F32),
            jax.ShapeDtypeStruct((b, keep, aw), F32),
            jax.ShapeDtypeStruct((b, t, 2 * mw), BF16),
            jax.ShapeDtypeStruct((b, t, mw), BF16),
            jax.ShapeDtypeStruct((b, t, mw), F32),
            jax.ShapeDtypeStruct((b, t, 2 * d), F32),
            jax.ShapeDtypeStruct((b, t, LANES), F32),
            jax.ShapeDtypeStruct((b, SUBLANES, 2 * mw), F32),
        ],
        scratch_shapes=[pltpu.VMEM((bt, SUBLANES + tt, 2 * mw), F32), pltpu.VMEM((bt * tt, d), BF16)],
        compiler_params=_params(("parallel", "arbitrary")),
        name="inproj",
    )(x, mod, mod, g_mix, w_main, w_gates, b_if, conv_w, conv_b, conv_init)
    return outs


def _bias_kernel(rev_ref, o_ref):
    w = rev_ref.shape[-1]
    x = jnp.broadcast_to(rev_ref[0, 0], (CHUNK, w))
    r = pltpu.roll(x, w - (CHUNK - 1), 1, stride=1, stride_axis=0)
    o_ref[0, 0] = r[:, 0:BAND] * LOG2E


def _bias_call(rel_bias):
    depth, nh, rel = rel_bias.shape
    n_ext = BAND + CHUNK - 1
    w = -(-n_ext // LANES) * LANES
    ext = jnp.concatenate([rel_bias, jnp.broadcast_to(rel_bias[:, :, rel - 1:], (depth, nh, n_ext - rel))], axis=2)
    rev = jnp.pad(ext[:, :, ::-1], ((0, 0), (0, 0), (0, w - n_ext))).reshape(depth, nh, 1, w)
    out = pl.pallas_call(
        _bias_kernel,
        grid=(depth, nh),
        in_specs=[pl.BlockSpec((1, 1, 1, w), lambda l, h: (l, h, 0, 0))],
        out_specs=pl.BlockSpec((1, 1, CHUNK, BAND), lambda l, h: (l, h, 0, 0)),
        out_shape=jax.ShapeDtypeStruct((depth, nh, CHUNK, BAND), F32),
        compiler_params=_params(("parallel", "parallel")),
        name="rel_bias",
    )(rev)
    return out.reshape(depth, N_PAIRS, 2 * CHUNK, BAND)


def _pair_rows(qp):
    lane = lax.broadcasted_iota(jnp.int32, qp.shape, 1)
    zero = jnp.zeros_like(qp)
    return jnp.concatenate([jnp.where(lane < ATT_HEAD_DIM, qp, zero),
                            jnp.where(lane >= ATT_HEAD_DIM, qp, zero)], axis=0)


def _pair_merge(o, rows):
    lane = lax.broadcasted_iota(jnp.int32, (rows, LANES), 1)
    return jnp.where(lane < ATT_HEAD_DIM, o[0:rows], o[rows:2 * rows])


def _nt_dot(a, b):
    return lax.dot_general(a, b, (((1,), (1,)), ((), ())), preferred_element_type=F32)


def _attn_prompt_kernel(q_ref, kp_ref, kc_ref, vp_ref, vc_ref, bias_ref, o_ref, kband, vband, *, unroll):
    j = pl.program_id(1)
    rows = kc_ref.shape[1]
    past = N_PAST_CHUNKS * CHUNK
    assert rows == past

    @pl.when(j == 0)
    def _():
        kband[0:past, :] = jnp.zeros((past, ATT_WIDTH), BF16)
        vband[0:past, :] = jnp.zeros((past, ATT_WIDTH), BF16)

    @pl.when(j > 0)
    def _():
        kband[0:past, :] = kp_ref[0]
        vband[0:past, :] = vp_ref[0]

    kband[past:past + rows, :] = kc_ref[0]
    vband[past:past + rows, :] = vc_ref[0]

    def chunks(i, masked):
        units = [(pl.multiple_of((i * unroll + u) * CHUNK, CHUNK), i * unroll + u, p)
                 for u in range(unroll) for p in range(N_PAIRS)]
        scores = []
        for r0, ci, p in units:
            ls = slice(p * LANES, (p + 1) * LANES)
            qs = _pair_rows(q_ref[0, pl.ds(r0, CHUNK), ls])
            scores.append(_nt_dot(qs, kband[pl.ds(r0, BAND), ls]) + bias_ref[0, p])
        for (r0, ci, p), s in zip(units, scores):
            ls = slice(p * LANES, (p + 1) * LANES)
            if masked:
                col = lax.broadcasted_iota(jnp.int32, s.shape, 1)
                s = jnp.where(col >= (N_PAST_CHUNKS - ci) * CHUNK, s, NEG)
            mx = jnp.max(s, axis=-1, keepdims=True)
            e = jnp.exp2(s - mx)
            den = jnp.sum(e, axis=-1, keepdims=True)
            o = jnp.dot(e.astype(BF16), vband[pl.ds(r0, BAND), ls], preferred_element_type=F32) / den
            o_ref[0, pl.ds(r0, CHUNK), ls] = _pair_merge(o, CHUNK).astype(BF16)

    n_iter = rows // (CHUNK * unroll)

    @pl.when(j == 0)
    def _():
        lax.fori_loop(0, n_iter, lambda i, c: (chunks(i, True), c)[1], 0)

    @pl.when(j > 0)
    def _():
        lax.fori_loop(0, n_iter, lambda i, c: (chunks(i, False), c)[1], 0)


def _attn_prompt_call(qkv, bias, layer):
    b, t, _ = qkv.shape
    rows = N_PAST_CHUNKS * CHUNK
    assert t % rows == 0
    aw = ATT_WIDTH

    def cur(cb):
        return pl.BlockSpec((1, rows, aw), lambda i, j: (i, j, cb))

    def prev(cb):
        return pl.BlockSpec((1, rows, aw), lambda i, j: (i, jnp.maximum(j - 1, 0), cb))

    return pl.pallas_call(
        functools.partial(_attn_prompt_kernel, unroll=ATTN_UNROLL),
        grid=(b, t // rows),
        in_specs=[cur(0), prev(1), cur(1), prev(2), cur(2),
                  _const_spec((1, N_PAIRS, 2 * CHUNK, BAND), lambda i, j: (layer, 0, 0, 0))],
        out_specs=pl.BlockSpec((1, rows, aw), lambda i, j: (i, j, 0)),
        out_shape=jax.ShapeDtypeStruct((b, t, aw), BF16),
        scratch_shapes=[pltpu.VMEM((2 * rows, aw), BF16), pltpu.VMEM((2 * rows, aw), BF16)],
        compiler_params=_params(("parallel", "parallel")),
        name="attn_prompt",
    )(qkv, qkv, qkv, qkv, qkv, bias)


def _attn_sample_kernel(q_ref, kn_ref, vn_ref, kc_ref, vc_ref, bias_ref, o_ref, *, bb):
    tq = q_ref.shape[1]
    pc = kc_ref.shape[3]
    for bi in range(bb):
        scores = []
        for p in range(N_PAIRS):
            ls = slice(p * LANES, (p + 1) * LANES)
            qs = _pair_rows(q_ref[bi, :, ls])
            bias = jnp.concatenate([bias_ref[0, p, 0:tq, :], bias_ref[0, p, CHUNK:CHUNK + tq, :]], axis=0)
            s1 = jnp.dot(qs, kc_ref[0, bi, ls, :].astype(BF16), preferred_element_type=F32)
            scores.append((s1 + bias[:, 0:pc], _nt_dot(qs, kn_ref[bi, :, ls]) + bias[:, pc:pc + tq]))
        for p, (s1, s2) in enumerate(scores):
            ls = slice(p * LANES, (p + 1) * LANES)
            mx = jnp.maximum(jnp.max(s1, axis=-1, keepdims=True), jnp.max(s2, axis=-1, keepdims=True))
            e1 = jnp.exp2(s1 - mx)
            e2 = jnp.exp2(s2 - mx)
            den = jnp.sum(e1, axis=-1, keepdims=True) + jnp.sum(e2, axis=-1, keepdims=True)
            o = (_nt_dot(e1.astype(BF16), vc_ref[0, bi, ls, :].astype(BF16))
                 + jnp.dot(e2.astype(BF16), vn_ref[bi, :, ls], preferred_element_type=F32)) / den
            o_ref[bi, :, ls] = _pair_merge(o, tq).astype(BF16)


def _attn_sample_call(qkv, cache_k, cache_v, bias, layer, bb):
    b, tq, _ = qkv.shape
    _, _, aw, pc = cache_k.shape
    assert b % bb == 0 and pc == N_PAST_CHUNKS * CHUNK and tq <= CHUNK and aw == ATT_WIDTH

    def new(cb):
        return pl.BlockSpec((bb, tq, aw), lambda i: (i, 0, cb))

    cache = pl.BlockSpec((1, bb, aw, pc), lambda i: (layer, i, 0, 0))
    return pl.pallas_call(
        functools.partial(_attn_sample_kernel, bb=bb),
        grid=(b // bb,),
        in_specs=[new(0), new(1), new(2), cache, cache,
                  _const_spec((1, N_PAIRS, 2 * CHUNK, BAND), lambda i: (layer, 0, 0, 0))],
        out_specs=pl.BlockSpec((bb, tq, aw), lambda i: (i, 0, 0)),
        out_shape=jax.ShapeDtypeStruct((b, tq, aw), BF16),
        compiler_params=_params(("parallel",)),
        name="attn_sample",
    )(qkv, qkv, qkv, cache_k, cache_v, bias)


def _transpose_rows(a):
    r = a.shape[0]
    pad = -r % LANES
    if pad:
        a = jnp.concatenate([a, jnp.zeros((pad, a.shape[1]), a.dtype)], axis=0)
    return a.T[:, 0:r]


def _transpose_cols(a):
    c = a.shape[1]
    pad = -c % LANES
    if pad:
        a = jnp.concatenate([a, jnp.zeros((a.shape[0], pad), a.dtype)], axis=1)
    return a.T[0:c, :]


def _block_cumsum(tri, g):
    g1 = g.astype(BF16)
    r1 = g - g1.astype(F32)
    g2 = r1.astype(BF16)
    g3 = (r1 - g2.astype(F32)).astype(BF16)
    return (jnp.dot(tri, g1, preferred_element_type=F32) + jnp.dot(tri, g2, preferred_element_type=F32)
            + jnp.dot(tri, g3, preferred_element_type=F32))


def _mlstm_kernel(qkm_ref, vm_ref, om_ref, gates_ref, gain_ref, tri_ref, c0_ref, n0_ref, m0_ref,
                  ym_ref, c1_ref, n1_ref, m1_ref, cn_scr, m_scr, *, bb, blk, nblk):
    tb = pl.program_id(1)
    hd = M_HEAD_DIM
    ext = cn_scr.shape[2] - hd

    def last_lane(r):
        return jnp.broadcast_to(r[:, blk - 1:blk], r.shape)

    ext_row0 = lax.broadcasted_iota(jnp.int32, (ext, hd), 0) == 0

    @pl.when(tb == 0)
    def _():
        for e in range(bb):
            for h in range(M_HEADS):
                cn_scr[e, h, 0:hd, :] = c0_ref[e, h].T
                cn_scr[e, h, hd:hd + ext, :] = jnp.where(ext_row0, n0_ref[e, h], 0.0)
        m_scr[...] = m0_ref[...]

    srow = lax.broadcasted_iota(jnp.int32, (blk, blk), 0)
    tcol = lax.broadcasted_iota(jnp.int32, (blk, blk), 1)
    causal = srow <= tcol
    ones_ext = jnp.where(lax.broadcasted_iota(jnp.int32, (ext, blk), 0) == 0, 1.0, 0.0)

    gt_all, bt_all, c_all = [], [], []
    for e in range(bb):
        g = gates_ref[e]
        b = _block_cumsum(tri_ref[...], g)
        gt_all.append(_transpose_rows(g))
        bt_all.append(_transpose_rows(b))
        c_all.append(g[:, 0:M_HEADS] - b[:, M_HEADS:2 * M_HEADS])

    units = [(e, h, bi) for e in range(bb) for h in range(M_HEADS) for bi in range(nblk)]
    vt_heads = {(e, h): _transpose_rows(vm_ref[e, :, h * hd:(h + 1) * hd].astype(F32))
                for e in range(bb) for h in range(M_HEADS)}
    scores = {}
    for e, h, bi in units:
        rs = slice(bi * blk, (bi + 1) * blk)
        q = qkm_ref[e, rs, h * hd:(h + 1) * hd]
        k = qkm_ref[e, rs, M_WIDTH + h * hd:M_WIDTH + (h + 1) * hd]
        b_row = bt_all[e][M_HEADS + h:M_HEADS + h + 1, rs]
        dmat = jnp.where(causal, b_row + c_all[e][rs, h:h + 1], NEG)
        m_in = jnp.max(dmat, axis=0, keepdims=True)
        scores[e, h, bi] = (m_in, _nt_dot(k, q) * jnp.exp(dmat - m_in))
    intra = {}
    for e, h, bi in units:
        rs = slice(bi * blk, (bi + 1) * blk)
        k = qkm_ref[e, rs, M_WIDTH + h * hd:M_WIDTH + (h + 1) * hd]
        li_row, b_row = gt_all[e][h:h + 1, rs], bt_all[e][M_HEADS + h:M_HEADS + h + 1, rs]
        m_in, smat = scores[e, h, bi]
        vt = jnp.concatenate([vt_heads[e, h][:, rs], ones_ext], axis=0)
        den_i = jnp.sum(smat, axis=0, keepdims=True)
        num_i = jnp.dot(vt[0:hd].astype(BF16), smat.astype(BF16), preferred_element_type=F32)
        b_last = last_lane(b_row)
        m_u = last_lane(m_in)
        w_u = jnp.exp(b_last - b_row + li_row - m_u)
        upd = jnp.dot((vt * w_u).astype(BF16), k, preferred_element_type=F32)
        intra[e, h, bi] = (den_i, num_i, b_last, m_u, upd)
    for bi in range(nblk):
        rs = slice(bi * blk, (bi + 1) * blk)
        for e, h in [(e, h) for e in range(bb) for h in range(M_HEADS)]:
            ls = slice(h * hd, (h + 1) * hd)
            q = qkm_ref[e, rs, ls]
            b_row = bt_all[e][M_HEADS + h:M_HEADS + h + 1, rs]
            m_in = scores[e, h, bi][0]
            den_i, num_i, b_last, m_u, upd = intra[e, h, bi]
            m_prev = m_scr[e, h][:, 0:blk]
            cn_prev = cn_scr[e, h]
            a = b_row + m_prev
            m_t = jnp.maximum(a, m_in)
            alpha = jnp.exp(m_in - m_t)
            beta = jnp.exp(a - m_t)
            inter = _nt_dot(cn_prev.astype(BF16), q)
            num = alpha * num_i + beta * inter[0:hd]
            den = alpha * den_i + beta * inter[hd:hd + 1]
            hh = num / jnp.maximum(jnp.abs(den), jnp.exp(-m_t))
            hh = hh * lax.rsqrt(jnp.mean(hh * hh, axis=0, keepdims=True) + EPS)
            y = _transpose_cols(hh) * gain_ref[:, ls] * _sigmoid(om_ref[e, rs, ls])
            ym_ref[e, rs, ls] = y.astype(BF16)
            m_new = last_lane(m_t)
            g_state = jnp.exp(b_last + m_prev - m_new)[:, 0:1]
            g_upd = jnp.exp(m_u - m_new)[:, 0:1]
            cn_scr[e, h] = g_state * cn_prev + g_upd * upd
            m_scr[e, h] = jnp.broadcast_to(m_new[:, 0:1], (1, LANES))

    @pl.when(tb == pl.num_programs(1) - 1)
    def _():
        for e in range(bb):
            for h in range(M_HEADS):
                c1_ref[e, h] = cn_scr[e, h, 0:hd, :].T
                n1_ref[e, h] = cn_scr[e, h, hd:hd + 1, :]
        m1_ref[...] = m_scr[...]


def _block_tril(rt, blk):
    r = jnp.arange(rt)
    return ((r[:, None] // blk == r[None, :] // blk) & (r[None, :] <= r[:, None])).astype(BF16)


def _mlstm_call(qkm, vm, om, gates, gain, c0, n0, m0, bb, blk, nblk):
    b, t, _ = qkm.shape
    rt = blk * nblk
    assert t % rt == 0 and b % bb == 0
    mw, hd, nh = M_WIDTH, M_HEAD_DIM, M_HEADS

    def tok(width):
        return pl.BlockSpec((bb, rt, width), lambda i, j: (i, j, 0))

    cspec = pl.BlockSpec((bb, nh, hd, hd), lambda i, j: (i, 0, 0, 0))
    vspec = pl.BlockSpec((bb, nh, 1, hd), lambda i, j: (i, 0, 0, 0))
    return pl.pallas_call(
        functools.partial(_mlstm_kernel, bb=bb, blk=blk, nblk=nblk),
        grid=(b // bb, t // rt),
        in_specs=[tok(2 * mw), tok(mw), tok(mw), tok(LANES),
                  _const_spec((1, mw), lambda i, j: (0, 0)), _const_spec((rt, rt), lambda i, j: (0, 0)),
                  cspec, vspec, vspec],
        out_specs=[tok(mw), cspec, vspec, vspec],
        out_shape=[
            jax.ShapeDtypeStruct((b, t, mw), BF16),
            jax.ShapeDtypeStruct((b, nh, hd, hd), F32),
            jax.ShapeDtypeStruct((b, nh, 1, hd), F32),
            jax.ShapeDtypeStruct((b, nh, 1, hd), F32),
        ],
        scratch_shapes=[pltpu.VMEM((bb, nh, hd + STATE_EXT_ROWS, hd), F32), pltpu.VMEM((bb, nh, 1, hd), F32)],
        compiler_params=_params(("parallel", "arbitrary")),
        name="mlstm",
    )(qkm, vm, om, gates, gain, _block_tril(rt, blk), c0, n0, m0)


def _ffn_kernel(x_ref, ya_ref, ym_ref, gm_ref, gt1_ref, sh2_ref, sc2_ref, gt2_ref, gffn_ref, gfin_ref,
                wba_ref, wbm_ref, wo_ref, wgu_ref, wd_ref, o_ref, act_scr, *, bt, tt, final):
    d = x_ref.shape[-1]
    m = bt * tt
    dff = wd_ref.shape[1]
    x = x_ref[...]
    pa = jnp.dot(ya_ref[...].reshape(m, ATT_WIDTH), wba_ref[0], preferred_element_type=F32)
    pm = jnp.dot(ym_ref[...].reshape(m, M_WIDTH), wbm_ref[0], preferred_element_type=F32)
    ga = gm_ref[:, :, 0:d].reshape(m, d)
    gmm = gm_ref[:, :, d:2 * d].reshape(m, d)
    merged = _sigmoid(ga) * pa + _sigmoid(gmm) * pm
    y1 = jnp.dot(merged.astype(BF16), wo_ref[0], preferred_element_type=F32)
    x1 = x + gt1_ref[0, :, 0] * y1.reshape(bt, tt, d)
    ms = jnp.mean(x1 * x1, axis=-1, keepdims=True)
    h2 = x1 * lax.rsqrt(ms + EPS) * gffn_ref[...]
    h2 = (h2 * (1.0 + sc2_ref[0, :, 0]) + sh2_ref[0, :, 0]).reshape(m, d).astype(BF16)
    cw = 256
    for c in range(0, dff, cw):
        g = jnp.dot(h2, wgu_ref[0, :, c:c + cw], preferred_element_type=F32)
        up = jnp.dot(h2, wgu_ref[0, :, dff + c:dff + c + cw], preferred_element_type=F32)
        act_scr[:, c:c + cw] = (g * _sigmoid(g) * up).astype(BF16)
    y2 = jnp.dot(act_scr[...], wd_ref[0], preferred_element_type=F32)
    x2 = x1 + gt2_ref[0, :, 0] * y2.reshape(bt, tt, d)
    if final:
        ms2 = jnp.mean(x2 * x2, axis=-1, keepdims=True)
        x2 = x2 * lax.rsqrt(ms2 + EPS) * gfin_ref[...]
    o_ref[...] = x2


def _ffn_call(x, ya, ym, gm, mod, layer, row0, g_ffn, g_final, w_br_att, w_br_mlstm, w_out, w_gate_up,
              w_down, bt, tt, final):
    b, t, d = x.shape
    dff = w_down.shape[1]
    assert b % bt == 0 and t % tt == 0 and dff % 256 == 0

    def wspec(shape):
        return _const_spec((1,) + shape, lambda i, j: (layer, 0, 0))

    def tok(width):
        return pl.BlockSpec((bt, tt, width), lambda i, j: (i, j, 0))

    def modspec(k):
        return pl.BlockSpec((1, bt, 1, 1, d), lambda i, j: (layer, row0 // bt + i, k, 0, 0))

    def const2(shape):
        return _const_spec(shape, lambda i, j: (0, 0))

    return pl.pallas_call(
        functools.partial(_ffn_kernel, bt=bt, tt=tt, final=final),
        grid=(b // bt, t // tt),
        in_specs=[tok(d), tok(ATT_WIDTH), tok(M_WIDTH), tok(2 * d),
                  modspec(2), modspec(3), modspec(4), modspec(5), const2((1, d)), const2((1, d)),
                  wspec((ATT_WIDTH, d)), wspec((M_WIDTH, d)), wspec((d, d)), wspec((d, 2 * dff)),
                  wspec((dff, d))],
        out_specs=tok(d),
        out_shape=jax.ShapeDtypeStruct((b, t, d), F32),
        scratch_shapes=[pltpu.VMEM((bt * tt, dff), BF16)],
        compiler_params=_params(("parallel", "parallel")),
        name="merge_ffn",
    )(x, ya, ym, gm, mod, mod, mod, mod, g_ffn, g_final, w_br_att, w_br_mlstm, w_out, w_gate_up, w_down)


def _layer(x, mod, layer, row0, cache, conv_init, c0, n0, m0, wts, bt, tt, blk, nblk, final):
    (g_mix, w_main, w_gates, b_if, conv_w, conv_b, bias, mh_gain, w_br_att, w_br_mlstm, w_out, g_ffn,
     w_gate_up, w_down, g_final) = wts
    b, t, _ = x.shape
    qkv, kt, vt, qkm, vm, om, gm, gates, ctail = _inproj_call(
        x, mod, layer, row0, g_mix, w_main, w_gates, b_if, conv_w, conv_b, conv_init, bt, tt)
    if cache is None:
        ya = _attn_prompt_call(qkv, bias, layer)
    else:
        ya = _attn_sample_call(qkv, cache[0], cache[1], bias, layer, bt)
    mb = next(c for c in (MLSTM_BATCH_TILE, 2, 1) if b % c == 0)
    ym, c1, n1, m1 = _mlstm_call(qkm, vm, om, gates, mh_gain, c0, n0, m0, mb, blk, nblk)
    x = _ffn_call(x, ya, ym, gm, mod, layer, row0, g_ffn, g_final, w_br_att, w_br_mlstm, w_out, w_gate_up,
                  w_down, bt, tt, final)
    keep = kt.shape[1]
    state = (kt.reshape(b, keep, ATT_HEADS, ATT_HEAD_DIM), vt.reshape(b, keep, ATT_HEADS, ATT_HEAD_DIM),
             ctail[:, SUBLANES - (CONV_W - 1):, :], c1, n1[:, :, 0, :], m1[:, :, 0, 0])
    return x, state


def kernel(x_prompt, x_sample, cache_k, cache_v, state_conv, state_C, state_n, state_m, c_prompt, c_sample,
           w_ada, b_ada, g_mix, w_in, b_if, conv_w, conv_b, rel_bias, mh_gain, w_br_att, w_br_mlstm, w_out,
           g_ffn, w_gate_up, w_down, g_final):
    depth = w_ada.shape[0]
    bp, tp, d = x_prompt.shape
    bs, ts, _ = x_sample.shape
    assert tp % PROMPT_ROWS == 0 and tp % CHUNK == 0 and bs % SAMPLE_BATCH_TILE == 0 and ts % CHUNK != 0
    mw = M_WIDTH

    mod = _ada_call(jnp.concatenate([c_sample, c_prompt], axis=0), w_ada, b_ada)
    mod = mod.reshape(depth, bs + bp, 6, 1, d)

    bias = _bias_call(rel_bias)
    pc = cache_k.shape[2]
    caches = tuple(jnp.transpose(c, (0, 1, 3, 4, 2)).reshape(depth, bs, ATT_WIDTH, pc) for c in (cache_k, cache_v))

    pad3 = SUBLANES - (CONV_W - 1)
    zero_conv = jnp.zeros((bp, SUBLANES, 2 * mw), F32)
    zero_c = jnp.zeros((bp, M_HEADS, M_HEAD_DIM, M_HEAD_DIM), F32)
    zero_v = jnp.zeros((bp, M_HEADS, 1, M_HEAD_DIM), F32)

    w_main, w_gates = _wprep_call(w_in)
    wb_att, wb_mlstm, wb_out = w_br_att.astype(BF16), w_br_mlstm.astype(BF16), w_out.astype(BF16)
    wb_gate_up, wb_down = w_gate_up.astype(BF16), w_down.astype(BF16)

    xp, xs = x_prompt, x_sample
    outs_p, outs_s = [], []
    for l in range(depth):
        bif = jnp.pad(b_if[l], (0, LANES - 2 * M_HEADS)).reshape(1, LANES)
        wts = (g_mix[l].reshape(1, d), w_main, w_gates, bif, conv_w[l], conv_b[l].reshape(1, 2 * mw),
               bias, mh_gain[l].reshape(1, mw), wb_att, wb_mlstm, wb_out, g_ffn[l].reshape(1, d), wb_gate_up,
               wb_down, g_final.reshape(1, d))
        final = l == depth - 1
        xp, st_p = _layer(xp, mod, l, bs, None, zero_conv, zero_c, zero_v, zero_v, wts,
                          1, PROMPT_ROWS, MLSTM_BLOCK, MLSTM_TILE_ROWS // MLSTM_BLOCK, final)
        conv_init = jnp.pad(state_conv[l], ((0, 0), (pad3, 0), (0, 0)))
        m0 = jnp.broadcast_to(state_m[l][:, :, None, None], (bs, M_HEADS, 1, M_HEAD_DIM))
        xs, st_s = _layer(xs, mod, l, 0, caches, conv_init, state_C[l],
                          state_n[l][:, :, None, :], m0, wts, SAMPLE_BATCH_TILE, ts, ts, 1, final)
        outs_p.append(st_p)
        outs_s.append(st_s)

    def stk(outs, i):
        return jnp.stack([o[i] for o in outs], axis=0)

    return (xp, xs,
            stk(outs_p, 0), stk(outs_p, 1), stk(outs_p, 2), stk(outs_p, 3), stk(outs_p, 4), stk(outs_p, 5),
            stk(outs_s, 0), stk(outs_s, 1), stk(outs_s, 2), stk(outs_s, 3), stk(outs_s, 4), stk(outs_s, 5))
```

```python
import functools

import jax
import jax.numpy as jnp
from jax import lax
from jax.experimental import pallas as pl
from jax.experimental.pallas import tpu as pltpu

F32 = jnp.float32
BF16 = jnp.bfloat16

CHUNK = 64
N_PAST_CHUNKS = 8
ATT_HEADS = 8
ATT_HEAD_DIM = 64
ATT_WIDTH = ATT_HEADS * ATT_HEAD_DIM
MAX_REL = 256
M_HEADS = 4
M_HEAD_DIM = 128
M_WIDTH = M_HEADS * M_HEAD_DIM
CONV_W = 4
EPS = 1e-6
NEG = -1e30

LANES = 128
SUBLANES = 8
VMEM_LIMIT = 56 * 1024 * 1024
PROMPT_ROWS = 512
INPROJ_PART_ROWS = 128
SAMPLE_BATCH_TILE = 8
MLSTM_BATCH_TILE = 4
BAND = (N_PAST_CHUNKS + 1) * CHUNK
N_PAIRS = ATT_HEADS // 2
ATTN_UNROLL = 8
ATTN_BATCH_TILE = 1
MLSTM_BLOCK = 128
MLSTM_TILE_ROWS = 512
STATE_EXT_ROWS = 16
LOG2E = 1.4426950408889634


def _sigmoid(x):
    return 1.0 / (1.0 + jnp.exp2(x * (-LOG2E)))


def _log_sigmoid(x):
    return jnp.minimum(x, 0.0) - jnp.log1p(jnp.exp(-jnp.abs(x)))


def _const_spec(shape, index_map):
    return pl.BlockSpec(shape, index_map, pipeline_mode=pl.Buffered(1))


def _params(sem):
    return pltpu.CompilerParams(dimension_semantics=sem, vmem_limit_bytes=VMEM_LIMIT)


def _ada_kernel(c_ref, w_ref, b_ref, o_ref):
    c = c_ref[...]
    a = (c * _sigmoid(c)).astype(BF16)
    o_ref[0] = jnp.dot(a, w_ref[0].astype(BF16), preferred_element_type=F32) + b_ref[0]


def _ada_call(c_all, w_ada, b_ada):
    depth, d, n = w_ada.shape
    r = c_all.shape[0]
    tn = 1536
    assert n % tn == 0
    return pl.pallas_call(
        _ada_kernel,
        grid=(depth, n // tn),
        in_specs=[
            pl.BlockSpec((r, d), lambda l, j: (0, 0)),
            pl.BlockSpec((1, d, tn), lambda l, j: (l, 0, j)),
            pl.BlockSpec((1, 1, tn), lambda l, j: (l, 0, j)),
        ],
        out_specs=pl.BlockSpec((1, r, tn), lambda l, j: (l, 0, j)),
        out_shape=jax.ShapeDtypeStruct((depth, r, n), F32),
        compiler_params=_params(("parallel", "parallel")),
        name="ada_mod",
    )(c_all, w_ada, b_ada.reshape(depth, 1, n))


def _wprep_kernel(wt_ref, main_ref, gates_ref):
    n_att = 3 * ATT_WIDTH + 4 * M_WIDTH
    n_g = 2 * M_HEADS
    n_main = main_ref.shape[2]
    step = 512
    for c0 in range(0, n_main, step):
        r0 = c0 if c0 < n_att else c0 + n_g
        main_ref[0, :, c0:c0 + step] = wt_ref[0, r0:r0 + step, :].T.astype(BF16)
    slab = wt_ref[0, n_att:n_att + LANES, :].T
    lane = lax.broadcasted_iota(jnp.int32, slab.shape, 1)
    gates_ref[0] = jnp.where(lane < n_g, slab, 0.0).astype(BF16)


def _wprep_call(w_in):
    depth, d, n = w_in.shape
    n_main = n - 2 * M_HEADS
    assert n_main % 512 == 0 and (3 * ATT_WIDTH + 4 * M_WIDTH) % 512 == 0
    return pl.pallas_call(
        _wprep_kernel,
        grid=(depth,),
        in_specs=[pl.BlockSpec((1, n, d), lambda l: (l, 0, 0), pipeline_mode=pl.Buffered(1))],
        out_specs=[pl.BlockSpec((1, d, n_main), lambda l: (l, 0, 0)),
                   pl.BlockSpec((1, d, LANES), lambda l: (l, 0, 0))],
        out_shape=[jax.ShapeDtypeStruct((depth, d, n_main), BF16), jax.ShapeDtypeStruct((depth, d, LANES), BF16)],
        compiler_params=_params(("parallel",)),
        name="w_in_prep",
    )(jnp.swapaxes(w_in, 1, 2))


def _inproj_kernel(x_ref, sh_ref, sc_ref, g_ref, w_ref, wg_ref, bif_ref, cw_ref, cb_ref, cinit_ref,
                   qkv_ref, kt_ref, vt_ref, qkm_ref, vm_ref, om_ref, gm_ref, gates_ref, ctail_ref,
                   conv_scr, *, bt, tt):
    t = pl.program_id(1)
    d = x_ref.shape[-1]
    aw = ATT_WIDTH
    mw = M_WIDTH

    @pl.when(t == 0)
    def _():
        conv_scr[:, 0:SUBLANES, :] = cinit_ref[...]

    def part(bs, ts):
        nb, ntok = bs.stop - bs.start, ts.stop - ts.start
        m = nb * ntok
        x = x_ref[bs, ts, :]
        ms = jnp.mean(x * x, axis=-1, keepdims=True)
        h = x * lax.rsqrt(ms + EPS) * g_ref[...]
        h = h * (1.0 + sc_ref[0, bs, 0]) + sh_ref[0, bs, 0]
        hb = h.reshape(m, d).astype(BF16)

        def proj(c0, width):
            return jnp.dot(hb, w_ref[0, :, c0:c0 + width], preferred_element_type=F32)

        c0 = 3 * aw
        conv_scr[bs, SUBLANES + ts.start:SUBLANES + ts.stop, :] = proj(c0, 2 * mw).reshape(nb, ntok, 2 * mw)
        lo = SUBLANES - (CONV_W - 1) + ts.start
        acc = conv_scr[bs, lo:lo + ntok, :] * cw_ref[0:1, :]
        for j in range(1, CONV_W):
            acc = acc + conv_scr[bs, lo + j:lo + j + ntok, :] * cw_ref[j:j + 1, :]
        acc = acc + cb_ref[...]
        act = acc * _sigmoid(acc)
        qkm_ref[bs, ts, 0:mw] = act[:, :, 0:mw].astype(BF16)
        qkm_ref[bs, ts, mw:2 * mw] = (act[:, :, mw:2 * mw] * (M_HEAD_DIM ** -0.5)).astype(BF16)
        u = proj(0, 3 * aw)
        qkv_ref[bs, ts, 0:aw] = (u[:, 0:aw] * (LOG2E * ATT_HEAD_DIM ** -0.5)).astype(BF16).reshape(nb, ntok, aw)
        qkv_ref[bs, ts, aw:3 * aw] = u[:, aw:3 * aw].astype(BF16).reshape(nb, ntok, 2 * aw)
        kt_ref[bs, ts, :] = u[:, aw:2 * aw].reshape(nb, ntok, aw)
        vt_ref[bs, ts, :] = u[:, 2 * aw:3 * aw].reshape(nb, ntok, aw)
        c0 += 2 * mw
        u = proj(c0, 2 * mw)
        vm_ref[bs, ts, :] = u[:, 0:mw].astype(BF16).reshape(nb, ntok, mw)
        om_ref[bs, ts, :] = u[:, mw:2 * mw].reshape(nb, ntok, mw)
        c0 += 2 * mw
        for j in range(0, 2 * d, d):
            gm_ref[bs, ts, j:j + d] = proj(c0 + j, d).reshape(nb, ntok, d)
        ug = jnp.dot(hb, wg_ref[0], preferred_element_type=F32) + bif_ref[...]
        col = lax.broadcasted_iota(jnp.int32, ug.shape, 1)
        gates_ref[bs, ts, :] = jnp.where(col >= M_HEADS, _log_sigmoid(ug), ug).reshape(nb, ntok, LANES)

    if bt > 1:
        step = max(1, INPROJ_PART_ROWS // tt)
        for b0 in range(0, bt, step):
            part(slice(b0, min(b0 + step, bt)), slice(0, tt))
    else:
        step = min(tt, INPROJ_PART_ROWS)
        for t0 in range(0, tt, step):
            part(slice(0, bt), slice(t0, t0 + step))
    tail = conv_scr[:, tt:tt + SUBLANES, :]
    ctail_ref[...] = tail
    conv_scr[:, 0:SUBLANES, :] = tail


def _inproj_call(x, mod, layer, row0, g_mix, w_main, w_gates, b_if, conv_w, conv_b, conv_init, bt, tt):
    b, t, d = x.shape
    nb, nt = b // bt, t // tt
    keep = min(N_PAST_CHUNKS * CHUNK, t)
    assert b % bt == 0 and t % tt == 0 and keep % tt == 0
    t_keep0 = nt - keep // tt
    n_main = w_main.shape[2]
    aw, mw = ATT_WIDTH, M_WIDTH

    def wspec(shape):
        return _const_spec((1,) + shape, lambda i, j: (layer, 0, 0))

    def tok(width):
        return pl.BlockSpec((bt, tt, width), lambda i, j: (i, j, 0))

    def modspec(k):
        return pl.BlockSpec((1, bt, 1, 1, d), lambda i, j: (layer, row0 // bt + i, k, 0, 0))

    def const2(shape):
        return _const_spec(shape, lambda i, j: (0, 0))

    keep_spec = pl.BlockSpec((bt, tt, aw), lambda i, j: (i, jnp.maximum(j - t_keep0, 0), 0))
    per_batch8 = pl.BlockSpec((bt, SUBLANES, 2 * mw), lambda i, j: (i, 0, 0))
    outs = pl.pallas_call(
        functools.partial(_inproj_kernel, bt=bt, tt=tt),
        grid=(nb, nt),
        in_specs=[
            tok(d), modspec(0), modspec(1), const2((1, d)),
            wspec((d, n_main)), wspec((d, LANES)), const2((1, LANES)),
            const2((CONV_W, 2 * mw)), const2((1, 2 * mw)), per_batch8,
        ],
        out_specs=[tok(3 * aw), keep_spec, keep_spec, tok(2 * mw), tok(mw), tok(mw), tok(2 * d),
                   tok(LANES), per_batch8],
        out_shape=[
            jax.ShapeDtypeStruct((b, t, 3 * aw), BF16),
            jax.ShapeDtypeStruct((b, keep, aw), F32),
            jax.ShapeDtypeStruct((b, keep, aw), F32),
            jax.ShapeDtypeStruct((b, t, 2 * mw), BF16),
            jax.ShapeDtypeStruct((b, t, mw), BF16),
            jax.ShapeDtypeStruct((b, t, mw), F32),
            jax.ShapeDtypeStruct((b, t, 2 * d), F32),
            jax.ShapeDtypeStruct((b, t, LANES), F32),
            jax.ShapeDtypeStruct((b, SUBLANES, 2 * mw), F32),
        ],
        scratch_shapes=[pltpu.VMEM((bt, SUBLANES + tt, 2 * mw), F32)],
        compiler_params=_params(("parallel", "arbitrary")),
        name="inproj",
    )(x, mod, mod, g_mix, w_main, w_gates, b_if, conv_w, conv_b, conv_init)
    return outs


def _bias_kernel(rev_ref, o_ref):
    w = rev_ref.shape[-1]
    x = jnp.broadcast_to(rev_ref[0, 0], (CHUNK, w))
    r = pltpu.roll(x, w - (CHUNK - 1), 1, stride=1, stride_axis=0)
    o_ref[0, 0] = r[:, 0:BAND] * LOG2E


def _bias_call(rel_bias):
    depth, nh, rel = rel_bias.shape
    n_ext = BAND + CHUNK - 1
    w = -(-n_ext // LANES) * LANES
    ext = jnp.concatenate([rel_bias, jnp.broadcast_to(rel_bias[:, :, rel - 1:], (depth, nh, n_ext - rel))], axis=2)
    rev = jnp.pad(ext[:, :, ::-1], ((0, 0), (0, 0), (0, w - n_ext))).reshape(depth, nh, 1, w)
    out = pl.pallas_call(
        _bias_kernel,
        grid=(depth, nh),
        in_specs=[pl.BlockSpec((1, 1, 1, w), lambda l, h: (l, h, 0, 0))],
        out_specs=pl.BlockSpec((1, 1, CHUNK, BAND), lambda l, h: (l, h, 0, 0)),
        out_shape=jax.ShapeDtypeStruct((depth, nh, CHUNK, BAND), F32),
        compiler_params=_params(("parallel", "parallel")),
        name="rel_bias",
    )(rev)
    return out.reshape(depth, N_PAIRS, 2 * CHUNK, BAND)


def _pair_rows(qp):
    lane = lax.broadcasted_iota(jnp.int32, qp.shape, 1)
    zero = jnp.zeros_like(qp)
    return jnp.concatenate([jnp.where(lane < ATT_HEAD_DIM, qp, zero),
                            jnp.where(lane >= ATT_HEAD_DIM, qp, zero)], axis=0)


def _pair_merge(o, rows):
    lane = lax.broadcasted_iota(jnp.int32, (rows, LANES), 1)
    return jnp.where(lane < ATT_HEAD_DIM, o[0:rows], o[rows:2 * rows])


def _nt_dot(a, b):
    return lax.dot_general(a, b, (((1,), (1,)), ((), ())), preferred_element_type=F32)


def _attn_prompt_kernel(q_ref, kp_ref, kc_ref, vp_ref, vc_ref, bias_ref, o_ref, kband, vband, *, unroll):
    j = pl.program_id(1)
    ab, rows = kc_ref.shape[0], kc_ref.shape[1]
    past = N_PAST_CHUNKS * CHUNK
    assert rows == past

    @pl.when(j == 0)
    def _():
        kband[:, 0:past, :] = jnp.zeros((ab, past, ATT_WIDTH), BF16)
        vband[:, 0:past, :] = jnp.zeros((ab, past, ATT_WIDTH), BF16)

    @pl.when(j > 0)
    def _():
        kband[:, 0:past, :] = kp_ref[...]
        vband[:, 0:past, :] = vp_ref[...]

    kband[:, past:past + rows, :] = kc_ref[...]
    vband[:, past:past + rows, :] = vc_ref[...]

    def chunks(i, masked):
        units = [(e, pl.multiple_of((i * unroll + u) * CHUNK, CHUNK), i * unroll + u, p)
                 for e in range(ab) for u in range(unroll) for p in range(N_PAIRS)]
        scores = []
        for e, r0, ci, p in units:
            ls = slice(p * LANES, (p + 1) * LANES)
            qs = _pair_rows(q_ref[e, pl.ds(r0, CHUNK), ls])
            scores.append(_nt_dot(qs, kband[e, pl.ds(r0, BAND), ls]) + bias_ref[0, p])
        for (e, r0, ci, p), s in zip(units, scores):
            ls = slice(p * LANES, (p + 1) * LANES)
            if masked:
                col = lax.broadcasted_iota(jnp.int32, s.shape, 1)
                s = jnp.where(col >= (N_PAST_CHUNKS - ci) * CHUNK, s, NEG)
            mx = jnp.max(s, axis=-1, keepdims=True)
            ex = jnp.exp2(s - mx)
            den = jnp.sum(ex, axis=-1, keepdims=True)
            o = jnp.dot(ex.astype(BF16), vband[e, pl.ds(r0, BAND), ls], preferred_element_type=F32) / den
            o_ref[e, pl.ds(r0, CHUNK), ls] = _pair_merge(o, CHUNK).astype(BF16)

    n_iter = rows // (CHUNK * unroll)

    @pl.when(j == 0)
    def _():
        lax.fori_loop(0, n_iter, lambda i, c: (chunks(i, True), c)[1], 0)

    @pl.when(j > 0)
    def _():
        lax.fori_loop(0, n_iter, lambda i, c: (chunks(i, False), c)[1], 0)


def _attn_prompt_call(qkv, bias, layer):
    b, t, _ = qkv.shape
    rows = N_PAST_CHUNKS * CHUNK
    assert t % rows == 0
    aw = ATT_WIDTH
    ab = next(c for c in (ATTN_BATCH_TILE, 1) if b % c == 0)

    def cur(cb):
        return pl.BlockSpec((ab, rows, aw), lambda i, j: (i, j, cb))

    def prev(cb):
        return pl.BlockSpec((ab, rows, aw), lambda i, j: (i, jnp.maximum(j - 1, 0), cb))

    return pl.pallas_call(
        functools.partial(_attn_prompt_kernel, unroll=ATTN_UNROLL),
        grid=(b // ab, t // rows),
        in_specs=[cur(0), prev(1), cur(1), prev(2), cur(2),
                  _const_spec((1, N_PAIRS, 2 * CHUNK, BAND), lambda i, j: (layer, 0, 0, 0))],
        out_specs=pl.BlockSpec((ab, rows, aw), lambda i, j: (i, j, 0)),
        out_shape=jax.ShapeDtypeStruct((b, t, aw), BF16),
        scratch_shapes=[pltpu.VMEM((ab, 2 * rows, aw), BF16), pltpu.VMEM((ab, 2 * rows, aw), BF16)],
        compiler_params=_params(("parallel", "parallel")),
        name="attn_prompt",
    )(qkv, qkv, qkv, qkv, qkv, bias)


def _attn_sample_kernel(q_ref, kn_ref, vn_ref, kc_ref, vc_ref, bias_ref, o_ref, *, bb):
    tq = q_ref.shape[1]
    pc = kc_ref.shape[3]
    for bi in range(bb):
        scores = []
        for p in range(N_PAIRS):
            ls = slice(p * LANES, (p + 1) * LANES)
            qs = _pair_rows(q_ref[bi, :, ls])
            bias = jnp.concatenate([bias_ref[0, p, 0:tq, :], bias_ref[0, p, CHUNK:CHUNK + tq, :]], axis=0)
            s1 = jnp.dot(qs, kc_ref[0, bi, ls, :].astype(BF16), preferred_element_type=F32)
            scores.append((s1 + bias[:, 0:pc], _nt_dot(qs, kn_ref[bi, :, ls]) + bias[:, pc:pc + tq]))
        for p, (s1, s2) in enumerate(scores):
            ls = slice(p * LANES, (p + 1) * LANES)
            mx = jnp.maximum(jnp.max(s1, axis=-1, keepdims=True), jnp.max(s2, axis=-1, keepdims=True))
            e1 = jnp.exp2(s1 - mx)
            e2 = jnp.exp2(s2 - mx)
            den = jnp.sum(e1, axis=-1, keepdims=True) + jnp.sum(e2, axis=-1, keepdims=True)
            o = (_nt_dot(e1.astype(BF16), vc_ref[0, bi, ls, :].astype(BF16))
                 + jnp.dot(e2.astype(BF16), vn_ref[bi, :, ls], preferred_element_type=F32)) / den
            o_ref[bi, :, ls] = _pair_merge(o, tq).astype(BF16)


def _attn_sample_call(qkv, cache_k, cache_v, bias, layer, bb):
    b, tq, _ = qkv.shape
    _, _, aw, pc = cache_k.shape
    assert b % bb == 0 and pc == N_PAST_CHUNKS * CHUNK and tq <= CHUNK and aw == ATT_WIDTH

    def new(cb):
        return pl.BlockSpec((bb, tq, aw), lambda i: (i, 0, cb))

    cache = pl.BlockSpec((1, bb, aw, pc), lambda i: (layer, i, 0, 0))
    return pl.pallas_call(
        functools.partial(_attn_sample_kernel, bb=bb),
        grid=(b // bb,),
        in_specs=[new(0), new(1), new(2), cache, cache,
                  _const_spec((1, N_PAIRS, 2 * CHUNK, BAND), lambda i: (layer, 0, 0, 0))],
        out_specs=pl.BlockSpec((bb, tq, aw), lambda i: (i, 0, 0)),
        out_shape=jax.ShapeDtypeStruct((b, tq, aw), BF16),
        compiler_params=_params(("parallel",)),
        name="attn_sample",
    )(qkv, qkv, qkv, cache_k, cache_v, bias)


def _transpose_rows(a):
    r = a.shape[0]
    pad = -r % LANES
    if pad:
        a = jnp.concatenate([a, jnp.zeros((pad, a.shape[1]), a.dtype)], axis=0)
    return a.T[:, 0:r]


def _transpose_cols(a):
    c = a.shape[1]
    pad = -c % LANES
    if pad:
        a = jnp.concatenate([a, jnp.zeros((a.shape[0], pad), a.dtype)], axis=1)
    return a.T[0:c, :]


def _block_cumsum(tri, g):
    g1 = g.astype(BF16)
    r1 = g - g1.astype(F32)
    g2 = r1.astype(BF16)
    g3 = (r1 - g2.astype(F32)).astype(BF16)
    return (jnp.dot(tri, g1, preferred_element_type=F32) + jnp.dot(tri, g2, preferred_element_type=F32)
            + jnp.dot(tri, g3, preferred_element_type=F32))


def _mlstm_kernel(qkm_ref, vm_ref, om_ref, gates_ref, gain_ref, tri_ref, c0_ref, n0_ref, m0_ref,
                  ym_ref, c1_ref, n1_ref, m1_ref, cn_scr, m_scr, *, bb, blk, nblk):
    tb = pl.program_id(1)
    hd = M_HEAD_DIM
    ext = cn_scr.shape[2] - hd

    def last_lane(r):
        return jnp.broadcast_to(r[:, blk - 1:blk], r.shape)

    ext_row0 = lax.broadcasted_iota(jnp.int32, (ext, hd), 0) == 0

    @pl.when(tb == 0)
    def _():
        for e in range(bb):
            for h in range(M_HEADS):
                cn_scr[e, h, 0:hd, :] = c0_ref[e, h].T
                cn_scr[e, h, hd:hd + ext, :] = jnp.where(ext_row0, n0_ref[e, h], 0.0)
        m_scr[...] = m0_ref[...]

    srow = lax.broadcasted_iota(jnp.int32, (blk, blk), 0)
    tcol = lax.broadcasted_iota(jnp.int32, (blk, blk), 1)
    causal = srow <= tcol
    ones_ext = jnp.where(lax.broadcasted_iota(jnp.int32, (ext, blk), 0) == 0, 1.0, 0.0)

    gt_all, bt_all, c_all = [], [], []
    for e in range(bb):
        g = gates_ref[e]
        b = _block_cumsum(tri_ref[...], g)
        gt_all.append(_transpose_rows(g))
        bt_all.append(_transpose_rows(b))
        c_all.append(g[:, 0:M_HEADS] - b[:, M_HEADS:2 * M_HEADS])

    units = [(e, h, bi) for e in range(bb) for h in range(M_HEADS) for bi in range(nblk)]
    vt_heads = {(e, h): _transpose_rows(vm_ref[e, :, h * hd:(h + 1) * hd].astype(F32))
                for e in range(bb) for h in range(M_HEADS)}
    scores = {}
    for e, h, bi in units:
        rs = slice(bi * blk, (bi + 1) * blk)
        q = qkm_ref[e, rs, h * hd:(h + 1) * hd]
        k = qkm_ref[e, rs, M_WIDTH + h * hd:M_WIDTH + (h + 1) * hd]
        b_row = bt_all[e][M_HEADS + h:M_HEADS + h + 1, rs]
        dmat = jnp.where(causal, b_row + c_all[e][rs, h:h + 1], NEG)
        m_in = jnp.max(dmat, axis=0, keepdims=True)
        scores[e, h, bi] = (m_in, _nt_dot(k, q) * jnp.exp(dmat - m_in))
    intra = {}
    for e, h, bi in units:
        rs = slice(bi * blk, (bi + 1) * blk)
        k = qkm_ref[e, rs, M_WIDTH + h * hd:M_WIDTH + (h + 1) * hd]
        li_row, b_row = gt_all[e][h:h + 1, rs], bt_all[e][M_HEADS + h:M_HEADS + h + 1, rs]
        m_in, smat = scores[e, h, bi]
        vt = jnp.concatenate([vt_heads[e, h][:, rs], ones_ext], axis=0)
        den_i = jnp.sum(smat, axis=0, keepdims=True)
        num_i = jnp.dot(vt[0:hd].astype(BF16), smat.astype(BF16), preferred_element_type=F32)
        b_last = last_lane(b_row)
        m_u = last_lane(m_in)
        w_u = jnp.exp(b_last - b_row + li_row - m_u)
        upd = jnp.dot((vt * w_u).astype(BF16), k, preferred_element_type=F32)
        intra[e, h, bi] = (den_i, num_i, b_last, m_u, upd)
    for bi in range(nblk):
        rs = slice(bi * blk, (bi + 1) * blk)
        for e, h in [(e, h) for e in range(bb) for h in range(M_HEADS)]:
            ls = slice(h * hd, (h + 1) * hd)
            q = qkm_ref[e, rs, ls]
            b_row = bt_all[e][M_HEADS + h:M_HEADS + h + 1, rs]
            m_in = scores[e, h, bi][0]
            den_i, num_i, b_last, m_u, upd = intra[e, h, bi]
            m_prev = m_scr[e, h][:, 0:blk]
            cn_prev = cn_scr[e, h]
            a = b_row + m_prev
            m_t = jnp.maximum(a, m_in)
            alpha = jnp.exp(m_in - m_t)
            beta = jnp.exp(a - m_t)
            inter = _nt_dot(cn_prev.astype(BF16), q)
            num = alpha * num_i + beta * inter[0:hd]
            den = alpha * den_i + beta * inter[hd:hd + 1]
            hh = num / jnp.maximum(jnp.abs(den), jnp.exp(-m_t))
            hh = hh * lax.rsqrt(jnp.mean(hh * hh, axis=0, keepdims=True) + EPS)
            y = _transpose_cols(hh) * gain_ref[:, ls] * _sigmoid(om_ref[e, rs, ls])
            ym_ref[e, rs, ls] = y.astype(BF16)
            m_new = last_lane(m_t)
            g_state = jnp.exp(b_last + m_prev - m_new)[:, 0:1]
            g_upd = jnp.exp(m_u - m_new)[:, 0:1]
            cn_scr[e, h] = g_state * cn_prev + g_upd * upd
            m_scr[e, h] = jnp.broadcast_to(m_new[:, 0:1], (1, LANES))

    @pl.when(tb == pl.num_programs(1) - 1)
    def _():
        for e in range(bb):
            for h in range(M_HEADS):
                c1_ref[e, h] = cn_scr[e, h, 0:hd, :].T
                n1_ref[e, h] = cn_scr[e, h, hd:hd + 1, :]
        m1_ref[...] = m_scr[...]


def _block_tril(rt, blk):
    r = jnp.arange(rt)
    return ((r[:, None] // blk == r[None, :] // blk) & (r[None, :] <= r[:, None])).astype(BF16)


def _mlstm_call(qkm, vm, om, gates, gain, c0, n0, m0, bb, blk, nblk):
    b, t, _ = qkm.shape
    rt = blk * nblk
    assert t % rt == 0 and b % bb == 0
    mw, hd, nh = M_WIDTH, M_HEAD_DIM, M_HEADS

    def tok(width):
        return pl.BlockSpec((bb, rt, width), lambda i, j: (i, j, 0))

    cspec = pl.BlockSpec((bb, nh, hd, hd), lambda i, j: (i, 0, 0, 0))
    vspec = pl.BlockSpec((bb, nh, 1, hd), lambda i, j: (i, 0, 0, 0))
    return pl.pallas_call(
        functools.partial(_mlstm_kernel, bb=bb, blk=blk, nblk=nblk),
        grid=(b // bb, t // rt),
        in_specs=[tok(2 * mw), tok(mw), tok(mw), tok(LANES),
                  _const_spec((1, mw), lambda i, j: (0, 0)), _const_spec((rt, rt), lambda i, j: (0, 0)),
                  cspec, vspec, vspec],
        out_specs=[tok(mw), cspec, vspec, vspec],
        out_shape=[
            jax.ShapeDtypeStruct((b, t, mw), BF16),
            jax.ShapeDtypeStruct((b, nh, hd, hd), F32),
            jax.ShapeDtypeStruct((b, nh, 1, hd), F32),
            jax.ShapeDtypeStruct((b, nh, 1, hd), F32),
        ],
        scratch_shapes=[pltpu.VMEM((bb, nh, hd + STATE_EXT_ROWS, hd), F32), pltpu.VMEM((bb, nh, 1, hd), F32)],
        compiler_params=_params(("parallel", "arbitrary")),
        name="mlstm",
    )(qkm, vm, om, gates, gain, _block_tril(rt, blk), c0, n0, m0)


def _ffn_kernel(x_ref, ya_ref, ym_ref, gm_ref, gt1_ref, sh2_ref, sc2_ref, gt2_ref, gffn_ref, gfin_ref,
                wba_ref, wbm_ref, wo_ref, wgu_ref, wd_ref, o_ref, act_scr, *, bt, tt, final):
    d = x_ref.shape[-1]
    m = bt * tt
    dff = wd_ref.shape[1]
    x = x_ref[...]
    pa = jnp.dot(ya_ref[...].reshape(m, ATT_WIDTH), wba_ref[0], preferred_element_type=F32)
    pm = jnp.dot(ym_ref[...].reshape(m, M_WIDTH), wbm_ref[0], preferred_element_type=F32)
    ga = gm_ref[:, :, 0:d].reshape(m, d)
    gmm = gm_ref[:, :, d:2 * d].reshape(m, d)
    merged = _sigmoid(ga) * pa + _sigmoid(gmm) * pm
    y1 = jnp.dot(merged.astype(BF16), wo_ref[0], preferred_element_type=F32)
    x1 = x + gt1_ref[0, :, 0] * y1.reshape(bt, tt, d)
    ms = jnp.mean(x1 * x1, axis=-1, keepdims=True)
    h2 = x1 * lax.rsqrt(ms + EPS) * gffn_ref[...]
    h2 = (h2 * (1.0 + sc2_ref[0, :, 0]) + sh2_ref[0, :, 0]).reshape(m, d).astype(BF16)
    cw = 256
    for c in range(0, dff, cw):
        g = jnp.dot(h2, wgu_ref[0, :, c:c + cw], preferred_element_type=F32)
        up = jnp.dot(h2, wgu_ref[0, :, dff + c:dff + c + cw], preferred_element_type=F32)
        act_scr[:, c:c + cw] = (g * _sigmoid(g) * up).astype(BF16)
    y2 = jnp.dot(act_scr[...], wd_ref[0], preferred_element_type=F32)
    x2 = x1 + gt2_ref[0, :, 0] * y2.reshape(bt, tt, d)
    if final:
        ms2 = jnp.mean(x2 * x2, axis=-1, keepdims=True)
        x2 = x2 * lax.rsqrt(ms2 + EPS) * gfin_ref[...]
    o_ref[...] = x2


def _ffn_call(x, ya, ym, gm, mod, layer, row0, g_ffn, g_final, w_br_att, w_br_mlstm, w_out, w_gate_up,
              w_down, bt, tt, final):
    b, t, d = x.shape
    dff = w_down.shape[1]
    assert b % bt == 0 and t % tt == 0 and dff % 256 == 0

    def wspec(shape):
        return _const_spec((1,) + shape, lambda i, j: (layer, 0, 0))

    def tok(width):
        return pl.BlockSpec((bt, tt, width), lambda i, j: (i, j, 0))

    def modspec(k):
        return pl.BlockSpec((1, bt, 1, 1, d), lambda i, j: (layer, row0 // bt + i, k, 0, 0))

    def const2(shape):
        return _const_spec(shape, lambda i, j: (0, 0))

    return pl.pallas_call(
        functools.partial(_ffn_kernel, bt=bt, tt=tt, final=final),
        grid=(b // bt, t // tt),
        in_specs=[tok(d), tok(ATT_WIDTH), tok(M_WIDTH), tok(2 * d),
                  modspec(2), modspec(3), modspec(4), modspec(5), const2((1, d)), const2((1, d)),
                  wspec((ATT_WIDTH, d)), wspec((M_WIDTH, d)), wspec((d, d)), wspec((d, 2 * dff)),
                  wspec((dff, d))],
        out_specs=tok(d),
        out_shape=jax.ShapeDtypeStruct((b, t, d), F32),
        scratch_shapes=[pltpu.VMEM((bt * tt, dff), BF16)],
        compiler_params=_params(("parallel", "parallel")),
        name="merge_ffn",
    )(x, ya, ym, gm, mod, mod, mod, mod, g_ffn, g_final, w_br_att, w_br_mlstm, w_out, w_gate_up, w_down)


def _layer(x, mod, layer, row0, cache, conv_init, c0, n0, m0, wts, bt, tt, blk, nblk, final):
    (g_mix, w_main, w_gates, b_if, conv_w, conv_b, bias, mh_gain, w_br_att, w_br_mlstm, w_out, g_ffn,
     w_gate_up, w_down, g_final) = wts
    b, t, _ = x.shape
    qkv, kt, vt, qkm, vm, om, gm, gates, ctail = _inproj_call(
        x, mod, layer, row0, g_mix, w_main, w_gates, b_if, conv_w, conv_b, conv_init, bt, tt)
    if cache is None:
        ya = _attn_prompt_call(qkv, bias, layer)
    else:
        ya = _attn_sample_call(qkv, cache[0], cache[1], bias, layer, bt)
    mb = next(c for c in (MLSTM_BATCH_TILE, 2, 1) if b % c == 0)
    ym, c1, n1, m1 = _mlstm_call(qkm, vm, om, gates, mh_gain, c0, n0, m0, mb, blk, nblk)
    x = _ffn_call(x, ya, ym, gm, mod, layer, row0, g_ffn, g_final, w_br_att, w_br_mlstm, w_out, w_gate_up,
                  w_down, bt, tt, final)
    keep = kt.shape[1]
    state = (kt.reshape(b, keep, ATT_HEADS, ATT_HEAD_DIM), vt.reshape(b, keep, ATT_HEADS, ATT_HEAD_DIM),
             ctail[:, SUBLANES - (CONV_W - 1):, :], c1, n1[:, :, 0, :], m1[:, :, 0, 0])
    return x, state


def kernel(x_prompt, x_sample, cache_k, cache_v, state_conv, state_C, state_n, state_m, c_prompt, c_sample,
           w_ada, b_ada, g_mix, w_in, b_if, conv_w, conv_b, rel_bias, mh_gain, w_br_att, w_br_mlstm, w_out,
           g_ffn, w_gate_up, w_down, g_final):
    depth = w_ada.shape[0]
    bp, tp, d = x_prompt.shape
    bs, ts, _ = x_sample.shape
    assert tp % PROMPT_ROWS == 0 and tp % CHUNK == 0 and bs % SAMPLE_BATCH_TILE == 0 and ts % CHUNK != 0
    mw = M_WIDTH

    mod = _ada_call(jnp.concatenate([c_sample, c_prompt], axis=0), w_ada, b_ada)
    mod = mod.reshape(depth, bs + bp, 6, 1, d)

    bias = _bias_call(rel_bias)
    pc = cache_k.shape[2]
    caches = tuple(jnp.transpose(c, (0, 1, 3, 4, 2)).reshape(depth, bs, ATT_WIDTH, pc) for c in (cache_k, cache_v))

    pad3 = SUBLANES - (CONV_W - 1)
    zero_conv = jnp.zeros((bp, SUBLANES, 2 * mw), F32)
    zero_c = jnp.zeros((bp, M_HEADS, M_HEAD_DIM, M_HEAD_DIM), F32)
    zero_v = jnp.zeros((bp, M_HEADS, 1, M_HEAD_DIM), F32)

    w_main, w_gates = _wprep_call(w_in)
    wb_att, wb_mlstm, wb_out = w_br_att.astype(BF16), w_br_mlstm.astype(BF16), w_out.astype(BF16)
    wb_gate_up, wb_down = w_gate_up.astype(BF16), w_down.astype(BF16)

    xp, xs = x_prompt, x_sample
    outs_p, outs_s = [], []
    for l in range(depth):
        bif = jnp.pad(b_if[l], (0, LANES - 2 * M_HEADS)).reshape(1, LANES)
        wts = (g_mix[l].reshape(1, d), w_main, w_gates, bif, conv_w[l], conv_b[l].reshape(1, 2 * mw),
               bias, mh_gain[l].reshape(1, mw), wb_att, wb_mlstm, wb_out, g_ffn[l].reshape(1, d), wb_gate_up,
               wb_down, g_final.reshape(1, d))
        final = l == depth - 1
        xp, st_p = _layer(xp, mod, l, bs, None, zero_conv, zero_c, zero_v, zero_v, wts,
                          1, PROMPT_ROWS, MLSTM_BLOCK, MLSTM_TILE_ROWS // MLSTM_BLOCK, final)
        conv_init = jnp.pad(state_conv[l], ((0, 0), (pad3, 0), (0, 0)))
        m0 = jnp.broadcast_to(state_m[l][:, :, None, None], (bs, M_HEADS, 1, M_HEAD_DIM))
        xs, st_s = _layer(xs, mod, l, 0, caches, conv_init, state_C[l],
                          state_n[l][:, :, None, :], m0, wts, SAMPLE_BATCH_TILE, ts, ts, 1, final)
        outs_p.append(st_p)
        outs_s.append(st_s)

    def stk(outs, i):
        return jnp.stack([o[i] for o in outs], axis=0)

    return (xp, xs,
            stk(outs_p, 0), stk(outs_p, 1), stk(outs_p, 2), stk(outs_p, 3), stk(outs_p, 4), stk(outs_p, 5),
            stk(outs_s, 0), stk(outs_s, 1), stk(outs_s, 2), stk(outs_s, 3), stk(outs_s, 4), stk(outs_s, 5))
```

```python
import functools

import jax
import jax.numpy as jnp
from jax import lax
from jax.experimental import pallas as pl
from jax.experimental.pallas import tpu as pltpu

F32 = jnp.float32
BF16 = jnp.bfloat16

CHUNK = 64
N_PAST_CHUNKS = 8
ATT_HEADS = 8
ATT_HEAD_DIM = 64
ATT_WIDTH = ATT_HEADS * ATT_HEAD_DIM
MAX_REL = 256
M_HEADS = 4
M_HEAD_DIM = 128
M_WIDTH = M_HEADS * M_HEAD_DIM
CONV_W = 4
EPS = 1e-6
NEG = -1e30

LANES = 128
SUBLANES = 8
VMEM_LIMIT = 56 * 1024 * 1024
PROMPT_ROWS = 512
SAMPLE_BATCH_TILE = 8
MLSTM_BATCH_TILE = 4
BAND = (N_PAST_CHUNKS + 1) * CHUNK
N_PAIRS = ATT_HEADS // 2
ATTN_UNROLL = 8
ATTN_BATCH_TILE = 2
MLSTM_BLOCK = 128
MLSTM_TILE_ROWS = 512
STATE_EXT_ROWS = 16
LOG2E = 1.4426950408889634


def _sigmoid(x):
    return 1.0 / (1.0 + jnp.exp2(x * (-LOG2E)))


def _log_sigmoid(x):
    return jnp.minimum(x, 0.0) - jnp.log1p(jnp.exp(-jnp.abs(x)))


def _const_spec(shape, index_map):
    return pl.BlockSpec(shape, index_map, pipeline_mode=pl.Buffered(1))


def _params(sem):
    return pltpu.CompilerParams(dimension_semantics=sem, vmem_limit_bytes=VMEM_LIMIT)


def _ada_kernel(c_ref, w_ref, b_ref, o_ref):
    c = c_ref[...]
    a = (c * _sigmoid(c)).astype(BF16)
    o_ref[0] = jnp.dot(a, w_ref[0].astype(BF16), preferred_element_type=F32) + b_ref[0]


def _ada_call(c_all, w_ada, b_ada):
    depth, d, n = w_ada.shape
    r = c_all.shape[0]
    tn = 1536
    assert n % tn == 0
    return pl.pallas_call(
        _ada_kernel,
        grid=(depth, n // tn),
        in_specs=[
            pl.BlockSpec((r, d), lambda l, j: (0, 0)),
            pl.BlockSpec((1, d, tn), lambda l, j: (l, 0, j)),
            pl.BlockSpec((1, 1, tn), lambda l, j: (l, 0, j)),
        ],
        out_specs=pl.BlockSpec((1, r, tn), lambda l, j: (l, 0, j)),
        out_shape=jax.ShapeDtypeStruct((depth, r, n), F32),
        compiler_params=_params(("parallel", "parallel")),
        name="ada_mod",
    )(c_all, w_ada, b_ada.reshape(depth, 1, n))


def _wprep_kernel(wt_ref, main_ref, gates_ref):
    n_att = 3 * ATT_WIDTH + 4 * M_WIDTH
    n_g = 2 * M_HEADS
    n_main = main_ref.shape[2]
    step = 512
    for c0 in range(0, n_main, step):
        r0 = c0 if c0 < n_att else c0 + n_g
        main_ref[0, :, c0:c0 + step] = wt_ref[0, r0:r0 + step, :].T.astype(BF16)
    slab = wt_ref[0, n_att:n_att + LANES, :].T
    lane = lax.broadcasted_iota(jnp.int32, slab.shape, 1)
    gates_ref[0] = jnp.where(lane < n_g, slab, 0.0).astype(BF16)


def _wprep_call(w_in):
    depth, d, n = w_in.shape
    n_main = n - 2 * M_HEADS
    assert n_main % 512 == 0 and (3 * ATT_WIDTH + 4 * M_WIDTH) % 512 == 0
    return pl.pallas_call(
        _wprep_kernel,
        grid=(depth,),
        in_specs=[pl.BlockSpec((1, n, d), lambda l: (l, 0, 0), pipeline_mode=pl.Buffered(1))],
        out_specs=[pl.BlockSpec((1, d, n_main), lambda l: (l, 0, 0)),
                   pl.BlockSpec((1, d, LANES), lambda l: (l, 0, 0))],
        out_shape=[jax.ShapeDtypeStruct((depth, d, n_main), BF16), jax.ShapeDtypeStruct((depth, d, LANES), BF16)],
        compiler_params=_params(("parallel",)),
        name="w_in_prep",
    )(jnp.swapaxes(w_in, 1, 2))


def _inproj_kernel(x_ref, sh_ref, sc_ref, g_ref, w_ref, wg_ref, bif_ref, cw_ref, cb_ref, cinit_ref,
                   qkv_ref, kt_ref, vt_ref, qkm_ref, vm_ref, om_ref, gm_ref, gates_ref, ctail_ref,
                   conv_scr, hb_scr, *, bt, tt):
    t = pl.program_id(1)
    d = x_ref.shape[-1]
    m = bt * tt
    x = x_ref[...]
    ms = jnp.mean(x * x, axis=-1, keepdims=True)
    h = x * lax.rsqrt(ms + EPS) * g_ref[...]
    h = h * (1.0 + sc_ref[0, :, 0]) + sh_ref[0, :, 0]
    hb_scr[...] = h.reshape(m, d).astype(BF16)

    def proj(c0, width):
        return jnp.dot(hb_scr[...], w_ref[0, :, c0:c0 + width], preferred_element_type=F32)

    aw = ATT_WIDTH
    mw = M_WIDTH
    c0 = 3 * aw

    @pl.when(t == 0)
    def _():
        conv_scr[:, 0:SUBLANES, :] = cinit_ref[...]

    conv_scr[:, SUBLANES:SUBLANES + tt, :] = proj(c0, 2 * mw).reshape(bt, tt, 2 * mw)
    lo = SUBLANES - (CONV_W - 1)
    acc = conv_scr[:, lo:lo + tt, :] * cw_ref[0:1, :]
    for j in range(1, CONV_W):
        acc = acc + conv_scr[:, lo + j:lo + j + tt, :] * cw_ref[j:j + 1, :]
    acc = acc + cb_ref[...]
    act = acc * _sigmoid(acc)
    qkm_ref[:, :, 0:mw] = act[:, :, 0:mw].astype(BF16)
    qkm_ref[:, :, mw:2 * mw] = (act[:, :, mw:2 * mw] * (M_HEAD_DIM ** -0.5)).astype(BF16)
    tail = conv_scr[:, tt:tt + SUBLANES, :]
    ctail_ref[...] = tail
    conv_scr[:, 0:SUBLANES, :] = tail
    u = proj(0, 3 * aw)
    qkv_ref[:, :, 0:aw] = (u[:, 0:aw] * (LOG2E * ATT_HEAD_DIM ** -0.5)).astype(BF16).reshape(bt, tt, aw)
    qkv_ref[:, :, aw:3 * aw] = u[:, aw:3 * aw].astype(BF16).reshape(bt, tt, 2 * aw)
    kt_ref[...] = u[:, aw:2 * aw].reshape(bt, tt, aw)
    vt_ref[...] = u[:, 2 * aw:3 * aw].reshape(bt, tt, aw)
    c0 += 2 * mw
    u = proj(c0, 2 * mw)
    vm_ref[...] = u[:, 0:mw].astype(BF16).reshape(bt, tt, mw)
    om_ref[...] = u[:, mw:2 * mw].reshape(bt, tt, mw)
    c0 += 2 * mw
    for j in range(0, 2 * d, d):
        gm_ref[:, :, j:j + d] = proj(c0 + j, d).reshape(bt, tt, d)
    ug = jnp.dot(hb_scr[...], wg_ref[0], preferred_element_type=F32) + bif_ref[...]
    col = lax.broadcasted_iota(jnp.int32, ug.shape, 1)
    gates_ref[...] = jnp.where(col >= M_HEADS, _log_sigmoid(ug), ug).reshape(bt, tt, LANES)


def _inproj_call(x, mod, layer, row0, g_mix, w_main, w_gates, b_if, conv_w, conv_b, conv_init, bt, tt):
    b, t, d = x.shape
    nb, nt = b // bt, t // tt
    keep = min(N_PAST_CHUNKS * CHUNK, t)
    assert b % bt == 0 and t % tt == 0 and keep % tt == 0
    t_keep0 = nt - keep // tt
    n_main = w_main.shape[2]
    aw, mw = ATT_WIDTH, M_WIDTH

    def wspec(shape):
        return _const_spec((1,) + shape, lambda i, j: (layer, 0, 0))

    def tok(width):
        return pl.BlockSpec((bt, tt, width), lambda i, j: (i, j, 0))

    def modspec(k):
        return pl.BlockSpec((1, bt, 1, 1, d), lambda i, j: (layer, row0 // bt + i, k, 0, 0))

    def const2(shape):
        return _const_spec(shape, lambda i, j: (0, 0))

    keep_spec = pl.BlockSpec((bt, tt, aw), lambda i, j: (i, jnp.maximum(j - t_keep0, 0), 0))
    per_batch8 = pl.BlockSpec((bt, SUBLANES, 2 * mw), lambda i, j: (i, 0, 0))
    outs = pl.pallas_call(
        functools.partial(_inproj_kernel, bt=bt, tt=tt),
        grid=(nb, nt),
        in_specs=[
            tok(d), modspec(0), modspec(1), const2((1, d)),
            wspec((d, n_main)), wspec((d, LANES)), const2((1, LANES)),
            const2((CONV_W, 2 * mw)), const2((1, 2 * mw)), per_batch8,
        ],
        out_specs=[tok(3 * aw), keep_spec, keep_spec, tok(2 * mw), tok(mw), tok(mw), tok(2 * d),
                   tok(LANES), per_batch8],
        out_shape=[
            jax.ShapeDtypeStruct((b, t, 3 * aw), BF16),
            jax.ShapeDtypeStruct((b, keep, aw), F32),
            jax.ShapeDtypeStruct((b, keep, aw), F32),
            jax.ShapeDtypeStruct((b, t, 2 * mw), BF16),
            jax.ShapeDtypeStruct((b, t, mw), BF16),
            jax.ShapeDtypeStruct((b, t, mw), F32),
            jax.ShapeDtypeStruct((b, t, 2 * d), F32),
            jax.ShapeDtypeStruct((b, t, LANES), F32),
            jax.ShapeDtypeStruct((b, SUBLANES, 2 * mw), F32),
        ],
        scratch_shapes=[pltpu.VMEM((bt, SUBLANES + tt, 2 * mw), F32), pltpu.VMEM((bt * tt, d), BF16)],
        compiler_params=_params(("parallel", "arbitrary")),
        name="inproj",
    )(x, mod, mod, g_mix, w_main, w_gates, b_if, conv_w, conv_b, conv_init)
    return outs


def _bias_kernel(rev_ref, o_ref):
    w = rev_ref.shape[-1]
    x = jnp.broadcast_to(rev_ref[0, 0], (CHUNK, w))
    r = pltpu.roll(x, w - (CHUNK - 1), 1, stride=1, stride_axis=0)
    o_ref[0, 0] = r[:, 0:BAND] * LOG2E


def _bias_call(rel_bias):
    depth, nh, rel = rel_bias.shape
    assert rel == MAX_REL + CHUNK and nh == ATT_HEADS
    n_ext = BAND + CHUNK - 1
    w = -(-n_ext // LANES) * LANES
    ext = jnp.concatenate([rel_bias, jnp.broadcast_to(rel_bias[:, :, rel - 1:], (depth, nh, n_ext - rel))], axis=2)
    rev = jnp.pad(ext[:, :, ::-1], ((0, 0), (0, 0), (0, w - n_ext))).reshape(depth, nh, 1, w)
    out = pl.pallas_call(
        _bias_kernel,
        grid=(depth, nh),
        in_specs=[pl.BlockSpec((1, 1, 1, w), lambda l, h: (l, h, 0, 0))],
        out_specs=pl.BlockSpec((1, 1, CHUNK, BAND), lambda l, h: (l, h, 0, 0)),
        out_shape=jax.ShapeDtypeStruct((depth, nh, CHUNK, BAND), F32),
        compiler_params=_params(("parallel", "parallel")),
        name="rel_bias",
    )(rev)
    return out.reshape(depth, N_PAIRS, 2 * CHUNK, BAND)


def _pair_rows(qp):
    lane = lax.broadcasted_iota(jnp.int32, qp.shape, 1)
    zero = jnp.zeros_like(qp)
    return jnp.concatenate([jnp.where(lane < ATT_HEAD_DIM, qp, zero),
                            jnp.where(lane >= ATT_HEAD_DIM, qp, zero)], axis=0)


def _pair_merge(o, rows):
    lane = lax.broadcasted_iota(jnp.int32, (rows, LANES), 1)
    return jnp.where(lane < ATT_HEAD_DIM, o[0:rows], o[rows:2 * rows])


def _nt_dot(a, b):
    return lax.dot_general(a, b, (((1,), (1,)), ((), ())), preferred_element_type=F32)


def _attn_prompt_kernel(q_ref, kp_ref, kc_ref, vp_ref, vc_ref, bias_ref, o_ref, kband, vband, *, unroll):
    j = pl.program_id(1)
    ab, rows = kc_ref.shape[0], kc_ref.shape[1]
    past = N_PAST_CHUNKS * CHUNK
    assert rows == past

    @pl.when(j == 0)
    def _():
        kband[:, 0:past, :] = jnp.zeros((ab, past, ATT_WIDTH), BF16)
        vband[:, 0:past, :] = jnp.zeros((ab, past, ATT_WIDTH), BF16)

    @pl.when(j > 0)
    def _():
        kband[:, 0:past, :] = kp_ref[...]
        vband[:, 0:past, :] = vp_ref[...]

    kband[:, past:past + rows, :] = kc_ref[...]
    vband[:, past:past + rows, :] = vc_ref[...]

    def chunks(i, masked):
        units = [(e, pl.multiple_of((i * unroll + u) * CHUNK, CHUNK), i * unroll + u, p)
                 for e in range(ab) for u in range(unroll) for p in range(N_PAIRS)]
        scores = []
        for e, r0, ci, p in units:
            ls = slice(p * LANES, (p + 1) * LANES)
            qs = _pair_rows(q_ref[e, pl.ds(r0, CHUNK), ls])
            scores.append(_nt_dot(qs, kband[e, pl.ds(r0, BAND), ls]) + bias_ref[0, p])
        for (e, r0, ci, p), s in zip(units, scores):
            ls = slice(p * LANES, (p + 1) * LANES)
            if masked:
                col = lax.broadcasted_iota(jnp.int32, s.shape, 1)
                s = jnp.where(col >= (N_PAST_CHUNKS - ci) * CHUNK, s, NEG)
            mx = jnp.max(s, axis=-1, keepdims=True)
            ex = jnp.exp2(s - mx)
            den = jnp.sum(ex, axis=-1, keepdims=True)
            o = jnp.dot(ex.astype(BF16), vband[e, pl.ds(r0, BAND), ls], preferred_element_type=F32) / den
            o_ref[e, pl.ds(r0, CHUNK), ls] = _pair_merge(o, CHUNK).astype(BF16)

    n_iter = rows // (CHUNK * unroll)

    @pl.when(j == 0)
    def _():
        lax.fori_loop(0, n_iter, lambda i, c: (chunks(i, True), c)[1], 0)

    @pl.when(j > 0)
    def _():
        lax.fori_loop(0, n_iter, lambda i, c: (chunks(i, False), c)[1], 0)


def _attn_prompt_call(qkv, bias, layer):
    b, t, _ = qkv.shape
    rows = N_PAST_CHUNKS * CHUNK
    assert t % rows == 0
    aw = ATT_WIDTH
    ab = next(c for c in (ATTN_BATCH_TILE, 1) if b % c == 0)

    def cur(cb):
        return pl.BlockSpec((ab, rows, aw), lambda i, j: (i, j, cb))

    def prev(cb):
        return pl.BlockSpec((ab, rows, aw), lambda i, j: (i, jnp.maximum(j - 1, 0), cb))

    return pl.pallas_call(
        functools.partial(_attn_prompt_kernel, unroll=ATTN_UNROLL),
        grid=(b // ab, t // rows),
        in_specs=[cur(0), prev(1), cur(1), prev(2), cur(2),
                  _const_spec((1, N_PAIRS, 2 * CHUNK, BAND), lambda i, j: (layer, 0, 0, 0))],
        out_specs=pl.BlockSpec((ab, rows, aw), lambda i, j: (i, j, 0)),
        out_shape=jax.ShapeDtypeStruct((b, t, aw), BF16),
        scratch_shapes=[pltpu.VMEM((ab, 2 * rows, aw), BF16), pltpu.VMEM((ab, 2 * rows, aw), BF16)],
        compiler_params=_params(("parallel", "parallel")),
        name="attn_prompt",
    )(qkv, qkv, qkv, qkv, qkv, bias)


def _attn_sample_kernel(q_ref, kn_ref, vn_ref, kc_ref, vc_ref, bias_ref, o_ref, *, bb):
    tq = q_ref.shape[1]
    pc = kc_ref.shape[3]
    for bi in range(bb):
        scores = []
        for p in range(N_PAIRS):
            ls = slice(p * LANES, (p + 1) * LANES)
            qs = _pair_rows(q_ref[bi, :, ls])
            bias = jnp.concatenate([bias_ref[0, p, 0:tq, :], bias_ref[0, p, CHUNK:CHUNK + tq, :]], axis=0)
            s1 = jnp.dot(qs, kc_ref[0, bi, ls, :].astype(BF16), preferred_element_type=F32)
            scores.append((s1 + bias[:, 0:pc], _nt_dot(qs, kn_ref[bi, :, ls]) + bias[:, pc:pc + tq]))
        for p, (s1, s2) in enumerate(scores):
            ls = slice(p * LANES, (p + 1) * LANES)
            mx = jnp.maximum(jnp.max(s1, axis=-1, keepdims=True), jnp.max(s2, axis=-1, keepdims=True))
            e1 = jnp.exp2(s1 - mx)
            e2 = jnp.exp2(s2 - mx)
            den = jnp.sum(e1, axis=-1, keepdims=True) + jnp.sum(e2, axis=-1, keepdims=True)
            o = (_nt_dot(e1.astype(BF16), vc_ref[0, bi, ls, :].astype(BF16))
                 + jnp.dot(e2.astype(BF16), vn_ref[bi, :, ls], preferred_element_type=F32)) / den
            o_ref[bi, :, ls] = _pair_merge(o, tq).astype(BF16)


def _attn_sample_call(qkv, cache_k, cache_v, bias, layer, bb):
    b, tq, _ = qkv.shape
    _, _, aw, pc = cache_k.shape
    assert b % bb == 0 and pc == N_PAST_CHUNKS * CHUNK and tq <= CHUNK and aw == ATT_WIDTH

    def new(cb):
        return pl.BlockSpec((bb, tq, aw), lambda i: (i, 0, cb))

    cache = pl.BlockSpec((1, bb, aw, pc), lambda i: (layer, i, 0, 0))
    return pl.pallas_call(
        functools.partial(_attn_sample_kernel, bb=bb),
        grid=(b // bb,),
        in_specs=[new(0), new(1), new(2), cache, cache,
                  _const_spec((1, N_PAIRS, 2 * CHUNK, BAND), lambda i: (layer, 0, 0, 0))],
        out_specs=pl.BlockSpec((bb, tq, aw), lambda i: (i, 0, 0)),
        out_shape=jax.ShapeDtypeStruct((b, tq, aw), BF16),
        compiler_params=_params(("parallel",)),
        name="attn_sample",
    )(qkv, qkv, qkv, cache_k, cache_v, bias)


def _transpose_rows(a):
    r = a.shape[0]
    pad = -r % LANES
    if pad:
        a = jnp.concatenate([a, jnp.zeros((pad, a.shape[1]), a.dtype)], axis=0)
    return a.T[:, 0:r]


def _transpose_cols(a):
    c = a.shape[1]
    pad = -c % LANES
    if pad:
        a = jnp.concatenate([a, jnp.zeros((a.shape[0], pad), a.dtype)], axis=1)
    return a.T[0:c, :]


def _block_cumsum(tri, g):
    g1 = g.astype(BF16)
    r1 = g - g1.astype(F32)
    g2 = r1.astype(BF16)
    g3 = (r1 - g2.astype(F32)).astype(BF16)
    return (jnp.dot(tri, g1, preferred_element_type=F32) + jnp.dot(tri, g2, preferred_element_type=F32)
            + jnp.dot(tri, g3, preferred_element_type=F32))


def _mlstm_kernel(qkm_ref, vm_ref, om_ref, gates_ref, gain_ref, tri_ref, c0_ref, n0_ref, m0_ref,
                  ym_ref, c1_ref, n1_ref, m1_ref, cn_scr, m_scr, *, bb, blk, nblk):
    tb = pl.program_id(1)
    hd = M_HEAD_DIM
    ext = cn_scr.shape[2] - hd

    def last_lane(r):
        return jnp.broadcast_to(r[:, blk - 1:blk], r.shape)

    ext_row0 = lax.broadcasted_iota(jnp.int32, (ext, hd), 0) == 0

    @pl.when(tb == 0)
    def _():
        for e in range(bb):
            for h in range(M_HEADS):
                cn_scr[e, h, 0:hd, :] = c0_ref[e, h].T
                cn_scr[e, h, hd:hd + ext, :] = jnp.where(ext_row0, n0_ref[e, h], 0.0)
        m_scr[...] = m0_ref[...]

    srow = lax.broadcasted_iota(jnp.int32, (blk, blk), 0)
    tcol = lax.broadcasted_iota(jnp.int32, (blk, blk), 1)
    causal = srow <= tcol
    ones_ext = jnp.where(lax.broadcasted_iota(jnp.int32, (ext, blk), 0) == 0, 1.0, 0.0)

    gt_all, bt_all, c_all = [], [], []
    for e in range(bb):
        g = gates_ref[e]
        b = _block_cumsum(tri_ref[...], g)
        gt_all.append(_transpose_rows(g))
        bt_all.append(_transpose_rows(b))
        c_all.append(g[:, 0:M_HEADS] - b[:, M_HEADS:2 * M_HEADS])

    units = [(e, h, bi) for e in range(bb) for h in range(M_HEADS) for bi in range(nblk)]
    vt_heads = {(e, h): _transpose_rows(vm_ref[e, :, h * hd:(h + 1) * hd].astype(F32))
                for e in range(bb) for h in range(M_HEADS)}
    scores = {}
    for e, h, bi in units:
        rs = slice(bi * blk, (bi + 1) * blk)
        q = qkm_ref[e, rs, h * hd:(h + 1) * hd]
        k = qkm_ref[e, rs, M_WIDTH + h * hd:M_WIDTH + (h + 1) * hd]
        b_row = bt_all[e][M_HEADS + h:M_HEADS + h + 1, rs]
        dmat = jnp.where(causal, b_row + c_all[e][rs, h:h + 1], NEG)
        m_in = jnp.max(dmat, axis=0, keepdims=True)
        scores[e, h, bi] = (m_in, _nt_dot(k, q) * jnp.exp(dmat - m_in))
    intra = {}
    for e, h, bi in units:
        rs = slice(bi * blk, (bi + 1) * blk)
        k = qkm_ref[e, rs, M_WIDTH + h * hd:M_WIDTH + (h + 1) * hd]
        li_row, b_row = gt_all[e][h:h + 1, rs], bt_all[e][M_HEADS + h:M_HEADS + h + 1, rs]
        m_in, smat = scores[e, h, bi]
        vt = jnp.concatenate([vt_heads[e, h][:, rs], ones_ext], axis=0)
        den_i = jnp.sum(smat, axis=0, keepdims=True)
        num_i = jnp.dot(vt[0:hd].astype(BF16), smat.astype(BF16), preferred_element_type=F32)
        b_last = last_lane(b_row)
        m_u = last_lane(m_in)
        w_u = jnp.exp(b_last - b_row + li_row - m_u)
        upd = jnp.dot((vt * w_u).astype(BF16), k, preferred_element_type=F32)
        intra[e, h, bi] = (den_i, num_i, b_last, m_u, upd)
    for bi in range(nblk):
        rs = slice(bi * blk, (bi + 1) * blk)
        for e, h in [(e, h) for e in range(bb) for h in range(M_HEADS)]:
            ls = slice(h * hd, (h + 1) * hd)
            q = qkm_ref[e, rs, ls]
            b_row = bt_all[e][M_HEADS + h:M_HEADS + h + 1, rs]
            m_in = scores[e, h, bi][0]
            den_i, num_i, b_last, m_u, upd = intra[e, h, bi]
            m_prev = m_scr[e, h][:, 0:blk]
            cn_prev = cn_scr[e, h]
            a = b_row + m_prev
            m_t = jnp.maximum(a, m_in)
            alpha = jnp.exp(m_in - m_t)
            beta = jnp.exp(a - m_t)
            inter = _nt_dot(cn_prev.astype(BF16), q)
            num = alpha * num_i + beta * inter[0:hd]
            den = alpha * den_i + beta * inter[hd:hd + 1]
            hh = num / jnp.maximum(jnp.abs(den), jnp.exp(-m_t))
            hh = hh * lax.rsqrt(jnp.mean(hh * hh, axis=0, keepdims=True) + EPS)
            y = _transpose_cols(hh) * gain_ref[:, ls] * _sigmoid(om_ref[e, rs, ls])
            ym_ref[e, rs, ls] = y.astype(BF16)
            m_new = last_lane(m_t)
            g_state = jnp.exp(b_last + m_prev - m_new)[:, 0:1]
            g_upd = jnp.exp(m_u - m_new)[:, 0:1]
            cn_scr[e, h] = g_state * cn_prev + g_upd * upd
            m_scr[e, h] = jnp.broadcast_to(m_new[:, 0:1], (1, LANES))

    @pl.when(tb == pl.num_programs(1) - 1)
    def _():
        for e in range(bb):
            for h in range(M_HEADS):
                c1_ref[e, h] = cn_scr[e, h, 0:hd, :].T
                n1_ref[e, h] = cn_scr[e, h, hd:hd + 1, :]
        m1_ref[...] = m_scr[...]


def _block_tril(rt, blk):
    r = jnp.arange(rt)
    return ((r[:, None] // blk == r[None, :] // blk) & (r[None, :] <= r[:, None])).astype(BF16)


def _mlstm_call(qkm, vm, om, gates, gain, c0, n0, m0, bb, blk, nblk):
    b, t, _ = qkm.shape
    rt = blk * nblk
    assert t % rt == 0 and b % bb == 0
    mw, hd, nh = M_WIDTH, M_HEAD_DIM, M_HEADS

    def tok(width):
        return pl.BlockSpec((bb, rt, width), lambda i, j: (i, j, 0))

    cspec = pl.BlockSpec((bb, nh, hd, hd), lambda i, j: (i, 0, 0, 0))
    vspec = pl.BlockSpec((bb, nh, 1, hd), lambda i, j: (i, 0, 0, 0))
    return pl.pallas_call(
        functools.partial(_mlstm_kernel, bb=bb, blk=blk, nblk=nblk),
        grid=(b // bb, t // rt),
        in_specs=[tok(2 * mw), tok(mw), tok(mw), tok(LANES),
                  _const_spec((1, mw), lambda i, j: (0, 0)), _const_spec((rt, rt), lambda i, j: (0, 0)),
                  cspec, vspec, vspec],
        out_specs=[tok(mw), cspec, vspec, vspec],
        out_shape=[
            jax.ShapeDtypeStruct((b, t, mw), BF16),
            jax.ShapeDtypeStruct((b, nh, hd, hd), F32),
            jax.ShapeDtypeStruct((b, nh, 1, hd), F32),
            jax.ShapeDtypeStruct((b, nh, 1, hd), F32),
        ],
        scratch_shapes=[pltpu.VMEM((bb, nh, hd + STATE_EXT_ROWS, hd), F32), pltpu.VMEM((bb, nh, 1, hd), F32)],
        compiler_params=_params(("parallel", "arbitrary")),
        name="mlstm",
    )(qkm, vm, om, gates, gain, _block_tril(rt, blk), c0, n0, m0)


def _ffn_kernel(x_ref, ya_ref, ym_ref, gm_ref, gt1_ref, sh2_ref, sc2_ref, gt2_ref, gffn_ref, gfin_ref,
                wba_ref, wbm_ref, wo_ref, wgu_ref, wd_ref, o_ref, act_scr, *, bt, tt, final):
    d = x_ref.shape[-1]
    m = bt * tt
    dff = wd_ref.shape[1]
    x = x_ref[...]
    pa = jnp.dot(ya_ref[...].reshape(m, ATT_WIDTH), wba_ref[0], preferred_element_type=F32)
    pm = jnp.dot(ym_ref[...].reshape(m, M_WIDTH), wbm_ref[0], preferred_element_type=F32)
    ga = gm_ref[:, :, 0:d].reshape(m, d)
    gmm = gm_ref[:, :, d:2 * d].reshape(m, d)
    merged = _sigmoid(ga) * pa + _sigmoid(gmm) * pm
    y1 = jnp.dot(merged.astype(BF16), wo_ref[0], preferred_element_type=F32)
    x1 = x + gt1_ref[0, :, 0] * y1.reshape(bt, tt, d)
    ms = jnp.mean(x1 * x1, axis=-1, keepdims=True)
    h2 = x1 * lax.rsqrt(ms + EPS) * gffn_ref[...]
    h2 = (h2 * (1.0 + sc2_ref[0, :, 0]) + sh2_ref[0, :, 0]).reshape(m, d).astype(BF16)
    cw = 256
    for c in range(0, dff, cw):
        g = jnp.dot(h2, wgu_ref[0, :, c:c + cw], preferred_element_type=F32)
        up = jnp.dot(h2, wgu_ref[0, :, dff + c:dff + c + cw], preferred_element_type=F32)
        act_scr[:, c:c + cw] = (g * _sigmoid(g) * up).astype(BF16)
    y2 = jnp.dot(act_scr[...], wd_ref[0], preferred_element_type=F32)
    x2 = x1 + gt2_ref[0, :, 0] * y2.reshape(bt, tt, d)
    if final:
        ms2 = jnp.mean(x2 * x2, axis=-1, keepdims=True)
        x2 = x2 * lax.rsqrt(ms2 + EPS) * gfin_ref[...]
    o_ref[...] = x2


def _ffn_call(x, ya, ym, gm, mod, layer, row0, g_ffn, g_final, w_br_att, w_br_mlstm, w_out, w_gate_up,
              w_down, bt, tt, final):
    b, t, d = x.shape
    dff = w_down.shape[1]
    assert b % bt == 0 and t % tt == 0 and dff % 256 == 0

    def wspec(shape):
        return _const_spec((1,) + shape, lambda i, j: (layer, 0, 0))

    def tok(width):
        return pl.BlockSpec((bt, tt, width), lambda i, j: (i, j, 0))

    def modspec(k):
        return pl.BlockSpec((1, bt, 1, 1, d), lambda i, j: (layer, row0 // bt + i, k, 0, 0))

    def const2(shape):
        return _const_spec(shape, lambda i, j: (0, 0))

    return pl.pallas_call(
        functools.partial(_ffn_kernel, bt=bt, tt=tt, final=final),
        grid=(b // bt, t // tt),
        in_specs=[tok(d), tok(ATT_WIDTH), tok(M_WIDTH), tok(2 * d),
                  modspec(2), modspec(3), modspec(4), modspec(5), const2((1, d)), const2((1, d)),
                  wspec((ATT_WIDTH, d)), wspec((M_WIDTH, d)), wspec((d, d)), wspec((d, 2 * dff)),
                  wspec((dff, d))],
        out_specs=tok(d),
        out_shape=jax.ShapeDtypeStruct((b, t, d), F32),
        scratch_shapes=[pltpu.VMEM((bt * tt, dff), BF16)],
        compiler_params=_params(("parallel", "parallel")),
        name="merge_ffn",
    )(x, ya, ym, gm, mod, mod, mod, mod, g_ffn, g_final, w_br_att, w_br_mlstm, w_out, w_gate_up, w_down)


def _layer(x, mod, layer, row0, cache, conv_init, c0, n0, m0, wts, bt, tt, blk, nblk, final):
    (g_mix, w_main, w_gates, b_if, conv_w, conv_b, bias, mh_gain, w_br_att, w_br_mlstm, w_out, g_ffn,
     w_gate_up, w_down, g_final) = wts
    b, t, _ = x.shape
    qkv, kt, vt, qkm, vm, om, gm, gates, ctail = _inproj_call(
        x, mod, layer, row0, g_mix, w_main, w_gates, b_if, conv_w, conv_b, conv_init, bt, tt)
    if cache is None:
        ya = _attn_prompt_call(qkv, bias, layer)
    else:
        ya = _attn_sample_call(qkv, cache[0], cache[1], bias, layer, bt)
    mb = next(c for c in (MLSTM_BATCH_TILE, 2, 1) if b % c == 0)
    ym, c1, n1, m1 = _mlstm_call(qkm, vm, om, gates, mh_gain, c0, n0, m0, mb, blk, nblk)
    x = _ffn_call(x, ya, ym, gm, mod, layer, row0, g_ffn, g_final, w_br_att, w_br_mlstm, w_out, w_gate_up,
                  w_down, bt, tt, final)
    keep = kt.shape[1]
    state = (kt.reshape(b, keep, ATT_HEADS, ATT_HEAD_DIM), vt.reshape(b, keep, ATT_HEADS, ATT_HEAD_DIM),
             ctail[:, SUBLANES - (CONV_W - 1):, :], c1, n1[:, :, 0, :], m1[:, :, 0, 0])
    return x, state


def kernel(x_prompt, x_sample, cache_k, cache_v, state_conv, state_C, state_n, state_m, c_prompt, c_sample,
           w_ada, b_ada, g_mix, w_in, b_if, conv_w, conv_b, rel_bias, mh_gain, w_br_att, w_br_mlstm, w_out,
           g_ffn, w_gate_up, w_down, g_final):
    depth = w_ada.shape[0]
    bp, tp, d = x_prompt.shape
    bs, ts, _ = x_sample.shape
    assert tp % PROMPT_ROWS == 0 and tp % CHUNK == 0 and bs % SAMPLE_BATCH_TILE == 0 and ts % CHUNK != 0
    mw = M_WIDTH

    mod = _ada_call(jnp.concatenate([c_sample, c_prompt], axis=0), w_ada, b_ada)
    mod = mod.reshape(depth, bs + bp, 6, 1, d)

    bias = _bias_call(rel_bias)
    pc = cache_k.shape[2]
    caches = tuple(jnp.transpose(c, (0, 1, 3, 4, 2)).reshape(depth, bs, ATT_WIDTH, pc) for c in (cache_k, cache_v))

    pad3 = SUBLANES - (CONV_W - 1)
    zero_conv = jnp.zeros((bp, SUBLANES, 2 * mw), F32)
    zero_c = jnp.zeros((bp, M_HEADS, M_HEAD_DIM, M_HEAD_DIM), F32)
    zero_v = jnp.zeros((bp, M_HEADS, 1, M_HEAD_DIM), F32)

    w_main, w_gates = _wprep_call(w_in)
    wb_att, wb_mlstm, wb_out = w_br_att.astype(BF16), w_br_mlstm.astype(BF16), w_out.astype(BF16)
    wb_gate_up, wb_down = w_gate_up.astype(BF16), w_down.astype(BF16)

    xp, xs = x_prompt, x_sample
    outs_p, outs_s = [], []
    for l in range(depth):
        bif = jnp.pad(b_if[l], (0, LANES - 2 * M_HEADS)).reshape(1, LANES)
        wts = (g_mix[l].reshape(1, d), w_main, w_gates, bif, conv_w[l], conv_b[l].reshape(1, 2 * mw),
               bias, mh_gain[l].reshape(1, mw), wb_att, wb_mlstm, wb_out, g_ffn[l].reshape(1, d), wb_gate_up,
               wb_down, g_final.reshape(1, d))
        final = l == depth - 1
        xp, st_p = _layer(xp, mod, l, bs, None, zero_conv, zero_c, zero_v, zero_v, wts,
                          1, PROMPT_ROWS, MLSTM_BLOCK, MLSTM_TILE_ROWS // MLSTM_BLOCK, final)
        conv_init = jnp.pad(state_conv[l], ((0, 0), (pad3, 0), (0, 0)))
        m0 = jnp.broadcast_to(state_m[l][:, :, None, None], (bs, M_HEADS, 1, M_HEAD_DIM))
        xs, st_s = _layer(xs, mod, l, 0, caches, conv_init, state_C[l],
                          state_n[l][:, :, None, :], m0, wts, SAMPLE_BATCH_TILE, ts, ts, 1, final)
        outs_p.append(st_p)
        outs_s.append(st_s)

    def stk(outs, i):
        return jnp.stack([o[i] for o in outs], axis=0)

    return (xp, xs,
            stk(outs_p, 0), stk(outs_p, 1), stk(outs_p, 2), stk(outs_p, 3), stk(outs_p, 4), stk(outs_p, 5),
            stk(outs_s, 0), stk(outs_s, 1), stk(outs_s, 2), stk(outs_s, 3), stk(outs_s, 4), stk(outs_s, 5))
```

```python
import functools

import jax
import jax.numpy as jnp
from jax import lax
from jax.experimental import pallas as pl
from jax.experimental.pallas import tpu as pltpu

F32 = jnp.float32
BF16 = jnp.bfloat16

CHUNK = 64
N_PAST_CHUNKS = 8
ATT_HEADS = 8
ATT_HEAD_DIM = 64
ATT_WIDTH = ATT_HEADS * ATT_HEAD_DIM
MAX_REL = 256
M_HEADS = 4
M_HEAD_DIM = 128
M_WIDTH = M_HEADS * M_HEAD_DIM
CONV_W = 4
EPS = 1e-6
NEG = -1e30

LANES = 128
SUBLANES = 8
VMEM_LIMIT = 56 * 1024 * 1024
PROMPT_ROWS = 512
SAMPLE_BATCH_TILE = 8
MLSTM_BATCH_TILE = 4
BAND = (N_PAST_CHUNKS + 1) * CHUNK
N_PAIRS = ATT_HEADS // 2
ATTN_UNROLL = 8
ATTN_BATCH_TILE = 2
MLSTM_BLOCK = 128
MLSTM_TILE_ROWS = 512
STATE_EXT_ROWS = 16
LOG2E = 1.4426950408889634


def _sigmoid(x):
    return 1.0 / (1.0 + jnp.exp2(x * (-LOG2E)))


def _log_sigmoid(x):
    return jnp.minimum(x, 0.0) - jnp.log1p(jnp.exp(-jnp.abs(x)))


def _const_spec(shape, index_map):
    return pl.BlockSpec(shape, index_map, pipeline_mode=pl.Buffered(1))


def _params(sem):
    return pltpu.CompilerParams(dimension_semantics=sem, vmem_limit_bytes=VMEM_LIMIT)


def _ada_kernel(c_ref, w_ref, b_ref, o_ref):
    c = c_ref[...]
    a = (c * _sigmoid(c)).astype(BF16)
    o_ref[0] = jnp.dot(a, w_ref[0].astype(BF16), preferred_element_type=F32) + b_ref[0]


def _ada_call(c_all, w_ada, b_ada):
    depth, d, n = w_ada.shape
    r = c_all.shape[0]
    tn = 1536
    assert n % tn == 0
    return pl.pallas_call(
        _ada_kernel,
        grid=(depth, n // tn),
        in_specs=[
            pl.BlockSpec((r, d), lambda l, j: (0, 0)),
            pl.BlockSpec((1, d, tn), lambda l, j: (l, 0, j)),
            pl.BlockSpec((1, 1, tn), lambda l, j: (l, 0, j)),
        ],
        out_specs=pl.BlockSpec((1, r, tn), lambda l, j: (l, 0, j)),
        out_shape=jax.ShapeDtypeStruct((depth, r, n), F32),
        compiler_params=_params(("parallel", "parallel")),
        name="ada_mod",
    )(c_all, w_ada, b_ada.reshape(depth, 1, n))


def _wprep_kernel(wt_ref, main_ref, gates_ref):
    n_att = 3 * ATT_WIDTH + 4 * M_WIDTH
    n_g = 2 * M_HEADS
    n_main = main_ref.shape[2]
    step = 512
    for c0 in range(0, n_main, step):
        r0 = c0 if c0 < n_att else c0 + n_g
        main_ref[0, :, c0:c0 + step] = wt_ref[0, r0:r0 + step, :].T.astype(BF16)
    slab = wt_ref[0, n_att:n_att + LANES, :].T
    lane = lax.broadcasted_iota(jnp.int32, slab.shape, 1)
    gates_ref[0] = jnp.where(lane < n_g, slab, 0.0).astype(BF16)


def _wprep_call(w_in):
    depth, d, n = w_in.shape
    n_main = n - 2 * M_HEADS
    assert n_main % 512 == 0 and (3 * ATT_WIDTH + 4 * M_WIDTH) % 512 == 0
    return pl.pallas_call(
        _wprep_kernel,
        grid=(depth,),
        in_specs=[pl.BlockSpec((1, n, d), lambda l: (l, 0, 0), pipeline_mode=pl.Buffered(1))],
        out_specs=[pl.BlockSpec((1, d, n_main), lambda l: (l, 0, 0)),
                   pl.BlockSpec((1, d, LANES), lambda l: (l, 0, 0))],
        out_shape=[jax.ShapeDtypeStruct((depth, d, n_main), BF16), jax.ShapeDtypeStruct((depth, d, LANES), BF16)],
        compiler_params=_params(("parallel",)),
        name="w_in_prep",
    )(jnp.swapaxes(w_in, 1, 2))


def _inproj_kernel(x_ref, sh_ref, sc_ref, g_ref, w_ref, wg_ref, bif_ref, cw_ref, cb_ref, cinit_ref,
                   qkv_ref, kt_ref, vt_ref, qkm_ref, vm_ref, om_ref, gm_ref, gates_ref, ctail_ref,
                   conv_scr, hb_scr, *, bt, tt):
    t = pl.program_id(1)
    d = x_ref.shape[-1]
    m = bt * tt
    x = x_ref[...]
    ms = jnp.mean(x * x, axis=-1, keepdims=True)
    h = x * lax.rsqrt(ms + EPS) * g_ref[...]
    h = h * (1.0 + sc_ref[0, :, 0]) + sh_ref[0, :, 0]
    hb_scr[...] = h.reshape(m, d).astype(BF16)

    def proj(c0, width):
        return jnp.dot(hb_scr[...], w_ref[0, :, c0:c0 + width], preferred_element_type=F32)

    aw = ATT_WIDTH
    mw = M_WIDTH
    c0 = 3 * aw

    @pl.when(t == 0)
    def _():
        conv_scr[:, 0:SUBLANES, :] = cinit_ref[...]

    conv_scr[:, SUBLANES:SUBLANES + tt, :] = proj(c0, 2 * mw).reshape(bt, tt, 2 * mw)
    lo = SUBLANES - (CONV_W - 1)
    acc = conv_scr[:, lo:lo + tt, :] * cw_ref[0:1, :]
    for j in range(1, CONV_W):
        acc = acc + conv_scr[:, lo + j:lo + j + tt, :] * cw_ref[j:j + 1, :]
    acc = acc + cb_ref[...]
    act = acc * _sigmoid(acc)
    qkm_ref[:, :, 0:mw] = act[:, :, 0:mw].astype(BF16)
    qkm_ref[:, :, mw:2 * mw] = (act[:, :, mw:2 * mw] * (M_HEAD_DIM ** -0.5)).astype(BF16)
    tail = conv_scr[:, tt:tt + SUBLANES, :]
    ctail_ref[...] = tail
    conv_scr[:, 0:SUBLANES, :] = tail
    u = proj(0, 3 * aw)
    qkv_ref[:, :, 0:aw] = (u[:, 0:aw] * (LOG2E * ATT_HEAD_DIM ** -0.5)).astype(BF16).reshape(bt, tt, aw)
    qkv_ref[:, :, aw:3 * aw] = u[:, aw:3 * aw].astype(BF16).reshape(bt, tt, 2 * aw)
    kt_ref[...] = u[:, aw:2 * aw].reshape(bt, tt, aw)
    vt_ref[...] = u[:, 2 * aw:3 * aw].reshape(bt, tt, aw)
    c0 += 2 * mw
    u = proj(c0, 2 * mw)
    vm_ref[...] = u[:, 0:mw].astype(BF16).reshape(bt, tt, mw)
    om_ref[...] = u[:, mw:2 * mw].reshape(bt, tt, mw)
    c0 += 2 * mw
    for j in range(0, 2 * d, d):
        gm_ref[:, :, j:j + d] = proj(c0 + j, d).reshape(bt, tt, d)
    ug = jnp.dot(hb_scr[...], wg_ref[0], preferred_element_type=F32) + bif_ref[...]
    col = lax.broadcasted_iota(jnp.int32, ug.shape, 1)
    gates_ref[...] = jnp.where(col >= M_HEADS, _log_sigmoid(ug), ug).reshape(bt, tt, LANES)


def _inproj_call(x, mod, layer, row0, g_mix, w_main, w_gates, b_if, conv_w, conv_b, conv_init, bt, tt):
    b, t, d = x.shape
    nb, nt = b // bt, t // tt
    keep = min(N_PAST_CHUNKS * CHUNK, t)
    assert b % bt == 0 and t % tt == 0 and keep % tt == 0
    t_keep0 = nt - keep // tt
    n_main = w_main.shape[2]
    aw, mw = ATT_WIDTH, M_WIDTH

    def wspec(shape):
        return _const_spec((1,) + shape, lambda i, j: (layer, 0, 0))

    def tok(width):
        return pl.BlockSpec((bt, tt, width), lambda i, j: (i, j, 0))

    def modspec(k):
        return pl.BlockSpec((1, bt, 1, 1, d), lambda i, j: (layer, row0 // bt + i, k, 0, 0))

    def const2(shape):
        return _const_spec(shape, lambda i, j: (0, 0))

    keep_spec = pl.BlockSpec((bt, tt, aw), lambda i, j: (i, jnp.maximum(j - t_keep0, 0), 0))
    per_batch8 = pl.BlockSpec((bt, SUBLANES, 2 * mw), lambda i, j: (i, 0, 0))
    outs = pl.pallas_call(
        functools.partial(_inproj_kernel, bt=bt, tt=tt),
        grid=(nb, nt),
        in_specs=[
            tok(d), modspec(0), modspec(1), const2((1, d)),
            wspec((d, n_main)), wspec((d, LANES)), const2((1, LANES)),
            const2((CONV_W, 2 * mw)), const2((1, 2 * mw)), per_batch8,
        ],
        out_specs=[tok(3 * aw), keep_spec, keep_spec, tok(2 * mw), tok(mw), tok(mw), tok(2 * d),
                   tok(LANES), per_batch8],
        out_shape=[
            jax.ShapeDtypeStruct((b, t, 3 * aw), BF16),
            jax.ShapeDtypeStruct((b, keep, aw), F32),
            jax.ShapeDtypeStruct((b, keep, aw), F32),
            jax.ShapeDtypeStruct((b, t, 2 * mw), BF16),
            jax.ShapeDtypeStruct((b, t, mw), BF16),
            jax.ShapeDtypeStruct((b, t, mw), F32),
            jax.ShapeDtypeStruct((b, t, 2 * d), F32),
            jax.ShapeDtypeStruct((b, t, LANES), F32),
            jax.ShapeDtypeStruct((b, SUBLANES, 2 * mw), F32),
        ],
        scratch_shapes=[pltpu.VMEM((bt, SUBLANES + tt, 2 * mw), F32), pltpu.VMEM((bt * tt, d), BF16)],
        compiler_params=_params(("parallel", "arbitrary")),
        name="inproj",
    )(x, mod, mod, g_mix, w_main, w_gates, b_if, conv_w, conv_b, conv_init)
    return outs


def _bias_kernel(rev_ref, o_ref):
    w = rev_ref.shape[-1]
    x = jnp.broadcast_to(rev_ref[0, 0], (CHUNK, w))
    r = pltpu.roll(x, w - (CHUNK - 1), 1, stride=1, stride_axis=0)
    o_ref[0, 0] = r[:, 0:BAND] * LOG2E


def _bias_call(rel_bias):
    depth, nh, rel = rel_bias.shape
    assert rel == MAX_REL + CHUNK and nh == ATT_HEADS
    n_ext = BAND + CHUNK - 1
    w = -(-n_ext // LANES) * LANES
    ext = jnp.concatenate([rel_bias, jnp.broadcast_to(rel_bias[:, :, rel - 1:], (depth, nh, n_ext - rel))], axis=2)
    rev = jnp.pad(ext[:, :, ::-1], ((0, 0), (0, 0), (0, w - n_ext))).reshape(depth, nh, 1, w)
    out = pl.pallas_call(
        _bias_kernel,
        grid=(depth, nh),
        in_specs=[pl.BlockSpec((1, 1, 1, w), lambda l, h: (l, h, 0, 0))],
        out_specs=pl.BlockSpec((1, 1, CHUNK, BAND), lambda l, h: (l, h, 0, 0)),
        out_shape=jax.ShapeDtypeStruct((depth, nh, CHUNK, BAND), F32),
        compiler_params=_params(("parallel", "parallel")),
        name="rel_bias",
    )(rev)
    return out.reshape(depth, N_PAIRS, 2 * CHUNK, BAND)


def _pair_rows(qp):
    lane = lax.broadcasted_iota(jnp.int32, qp.shape, 1)
    zero = jnp.zeros_like(qp)
    return jnp.concatenate([jnp.where(lane < ATT_HEAD_DIM, qp, zero),
                            jnp.where(lane >= ATT_HEAD_DIM, qp, zero)], axis=0)


def _pair_merge(o, rows):
    lane = lax.broadcasted_iota(jnp.int32, (rows, LANES), 1)
    return jnp.where(lane < ATT_HEAD_DIM, o[0:rows], o[rows:2 * rows])


def _nt_dot(a, b):
    return lax.dot_general(a, b, (((1,), (1,)), ((), ())), preferred_element_type=F32)


def _attn_prompt_kernel(q_ref, kw_ref, vw_ref, bias_ref, o_ref, kscr, vscr, *, unroll):
    j = pl.program_id(1)
    ab, rows = q_ref.shape[0], q_ref.shape[1]
    past = N_PAST_CHUNKS * CHUNK
    assert rows == past and kw_ref.shape[1] == past + rows

    def chunks(i, masked):
        kband, vband = (kscr, vscr) if masked else (kw_ref, vw_ref)
        units = [(e, pl.multiple_of((i * unroll + u) * CHUNK, CHUNK), i * unroll + u, p)
                 for e in range(ab) for u in range(unroll) for p in range(N_PAIRS)]
        scores = []
        for e, r0, ci, p in units:
            ls = slice(p * LANES, (p + 1) * LANES)
            qs = _pair_rows(q_ref[e, pl.ds(r0, CHUNK), ls])
            scores.append(_nt_dot(qs, kband[e, pl.ds(r0, BAND), ls]) + bias_ref[0, p])
        for (e, r0, ci, p), s in zip(units, scores):
            ls = slice(p * LANES, (p + 1) * LANES)
            if masked:
                col = lax.broadcasted_iota(jnp.int32, s.shape, 1)
                s = jnp.where(col >= (N_PAST_CHUNKS - ci) * CHUNK, s, NEG)
            mx = jnp.max(s, axis=-1, keepdims=True)
            ex = jnp.exp2(s - mx)
            den = jnp.sum(ex, axis=-1, keepdims=True)
            o = jnp.dot(ex.astype(BF16), vband[e, pl.ds(r0, BAND), ls], preferred_element_type=F32) / den
            o_ref[e, pl.ds(r0, CHUNK), ls] = _pair_merge(o, CHUNK).astype(BF16)

    n_iter = rows // (CHUNK * unroll)

    @pl.when(j == 0)
    def _():
        kscr[:, 0:past, :] = jnp.zeros((ab, past, ATT_WIDTH), BF16)
        vscr[:, 0:past, :] = jnp.zeros((ab, past, ATT_WIDTH), BF16)
        kscr[:, past:past + rows, :] = kw_ref[:, 0:rows, :]
        vscr[:, past:past + rows, :] = vw_ref[:, 0:rows, :]
        lax.fori_loop(0, n_iter, lambda i, c: (chunks(i, True), c)[1], 0)

    @pl.when(j > 0)
    def _():
        lax.fori_loop(0, n_iter, lambda i, c: (chunks(i, False), c)[1], 0)


def _attn_prompt_call(qkv, bias, layer):
    b, t, _ = qkv.shape
    rows = N_PAST_CHUNKS * CHUNK
    assert t % rows == 0 and t >= 2 * rows
    aw = ATT_WIDTH
    ab = next(c for c in (ATTN_BATCH_TILE, 1) if b % c == 0)

    def cur(cb):
        return pl.BlockSpec((ab, rows, aw), lambda i, j: (i, j, cb))

    def window(cb):
        return pl.BlockSpec((pl.Element(ab), pl.Element(2 * rows), pl.Element(aw)),
                            lambda i, j: (i * ab, jnp.maximum(j - 1, 0) * rows, cb * aw))

    return pl.pallas_call(
        functools.partial(_attn_prompt_kernel, unroll=ATTN_UNROLL),
        grid=(b // ab, t // rows),
        in_specs=[cur(0), window(1), window(2),
                  _const_spec((1, N_PAIRS, 2 * CHUNK, BAND), lambda i, j: (layer, 0, 0, 0))],
        out_specs=pl.BlockSpec((ab, rows, aw), lambda i, j: (i, j, 0)),
        out_shape=jax.ShapeDtypeStruct((b, t, aw), BF16),
        scratch_shapes=[pltpu.VMEM((ab, 2 * rows, aw), BF16), pltpu.VMEM((ab, 2 * rows, aw), BF16)],
        compiler_params=_params(("parallel", "parallel")),
        name="attn_prompt",
    )(qkv, qkv, qkv, bias)


def _attn_sample_kernel(q_ref, kn_ref, vn_ref, kc_ref, vc_ref, bias_ref, o_ref, *, bb):
    tq = q_ref.shape[1]
    pc = kc_ref.shape[3]
    for bi in range(bb):
        scores = []
        for p in range(N_PAIRS):
            ls = slice(p * LANES, (p + 1) * LANES)
            qs = _pair_rows(q_ref[bi, :, ls])
            bias = jnp.concatenate([bias_ref[0, p, 0:tq, :], bias_ref[0, p, CHUNK:CHUNK + tq, :]], axis=0)
            s1 = jnp.dot(qs, kc_ref[0, bi, ls, :].astype(BF16), preferred_element_type=F32)
            scores.append((s1 + bias[:, 0:pc], _nt_dot(qs, kn_ref[bi, :, ls]) + bias[:, pc:pc + tq]))
        for p, (s1, s2) in enumerate(scores):
            ls = slice(p * LANES, (p + 1) * LANES)
            mx = jnp.maximum(jnp.max(s1, axis=-1, keepdims=True), jnp.max(s2, axis=-1, keepdims=True))
            e1 = jnp.exp2(s1 - mx)
            e2 = jnp.exp2(s2 - mx)
            den = jnp.sum(e1, axis=-1, keepdims=True) + jnp.sum(e2, axis=-1, keepdims=True)
            o = (_nt_dot(e1.astype(BF16), vc_ref[0, bi, ls, :].astype(BF16))
                 + jnp.dot(e2.astype(BF16), vn_ref[bi, :, ls], preferred_element_type=F32)) / den
            o_ref[bi, :, ls] = _pair_merge(o, tq).astype(BF16)


def _attn_sample_call(qkv, cache_k, cache_v, bias, layer, bb):
    b, tq, _ = qkv.shape
    _, _, aw, pc = cache_k.shape
    assert b % bb == 0 and pc == N_PAST_CHUNKS * CHUNK and tq <= CHUNK and aw == ATT_WIDTH

    def new(cb):
        return pl.BlockSpec((bb, tq, aw), lambda i: (i, 0, cb))

    cache = pl.BlockSpec((1, bb, aw, pc), lambda i: (layer, i, 0, 0))
    return pl.pallas_call(
        functools.partial(_attn_sample_kernel, bb=bb),
        grid=(b // bb,),
        in_specs=[new(0), new(1), new(2), cache, cache,
                  _const_spec((1, N_PAIRS, 2 * CHUNK, BAND), lambda i: (layer, 0, 0, 0))],
        out_specs=pl.BlockSpec((bb, tq, aw), lambda i: (i, 0, 0)),
        out_shape=jax.ShapeDtypeStruct((b, tq, aw), BF16),
        compiler_params=_params(("parallel",)),
        name="attn_sample",
    )(qkv, qkv, qkv, cache_k, cache_v, bias)


def _transpose_rows(a):
    r = a.shape[0]
    pad = -r % LANES
    if pad:
        a = jnp.concatenate([a, jnp.zeros((pad, a.shape[1]), a.dtype)], axis=0)
    return a.T[:, 0:r]


def _transpose_cols(a):
    c = a.shape[1]
    pad = -c % LANES
    if pad:
        a = jnp.concatenate([a, jnp.zeros((a.shape[0], pad), a.dtype)], axis=1)
    return a.T[0:c, :]


def _block_cumsum(tri, g):
    g1 = g.astype(BF16)
    r1 = g - g1.astype(F32)
    g2 = r1.astype(BF16)
    g3 = (r1 - g2.astype(F32)).astype(BF16)
    return (jnp.dot(tri, g1, preferred_element_type=F32) + jnp.dot(tri, g2, preferred_element_type=F32)
            + jnp.dot(tri, g3, preferred_element_type=F32))


def _mlstm_kernel(qkm_ref, vm_ref, om_ref, gates_ref, gain_ref, tri_ref, c0_ref, n0_ref, m0_ref,
                  ym_ref, c1_ref, n1_ref, m1_ref, cn_scr, m_scr, *, bb, blk, nblk):
    tb = pl.program_id(1)
    hd = M_HEAD_DIM
    ext = cn_scr.shape[2] - hd

    def last_lane(r):
        return jnp.broadcast_to(r[:, blk - 1:blk], r.shape)

    ext_row0 = lax.broadcasted_iota(jnp.int32, (ext, hd), 0) == 0

    @pl.when(tb == 0)
    def _():
        for e in range(bb):
            for h in range(M_HEADS):
                cn_scr[e, h, 0:hd, :] = c0_ref[e, h].T
                cn_scr[e, h, hd:hd + ext, :] = jnp.where(ext_row0, n0_ref[e, h], 0.0)
        m_scr[...] = m0_ref[...]

    srow = lax.broadcasted_iota(jnp.int32, (blk, blk), 0)
    tcol = lax.broadcasted_iota(jnp.int32, (blk, blk), 1)
    causal = srow <= tcol
    ones_ext = jnp.where(lax.broadcasted_iota(jnp.int32, (ext, blk), 0) == 0, 1.0, 0.0)

    gt_all, bt_all, c_all = [], [], []
    for e in range(bb):
        g = gates_ref[e]
        b = _block_cumsum(tri_ref[...], g)
        gt_all.append(_transpose_rows(g))
        bt_all.append(_transpose_rows(b))
        c_all.append(g[:, 0:M_HEADS] - b[:, M_HEADS:2 * M_HEADS])

    units = [(e, h, bi) for e in range(bb) for h in range(M_HEADS) for bi in range(nblk)]
    vt_heads = {(e, h): _transpose_rows(vm_ref[e, :, h * hd:(h + 1) * hd].astype(F32))
                for e in range(bb) for h in range(M_HEADS)}
    scores = {}
    for e, h, bi in units:
        rs = slice(bi * blk, (bi + 1) * blk)
        q = qkm_ref[e, rs, h * hd:(h + 1) * hd]
        k = qkm_ref[e, rs, M_WIDTH + h * hd:M_WIDTH + (h + 1) * hd]
        b_row = bt_all[e][M_HEADS + h:M_HEADS + h + 1, rs]
        dmat = jnp.where(causal, b_row + c_all[e][rs, h:h + 1], NEG)
        m_in = jnp.max(dmat, axis=0, keepdims=True)
        scores[e, h, bi] = (m_in, _nt_dot(k, q) * jnp.exp(dmat - m_in))
    intra = {}
    for e, h, bi in units:
        rs = slice(bi * blk, (bi + 1) * blk)
        k = qkm_ref[e, rs, M_WIDTH + h * hd:M_WIDTH + (h + 1) * hd]
        li_row, b_row = gt_all[e][h:h + 1, rs], bt_all[e][M_HEADS + h:M_HEADS + h + 1, rs]
        m_in, smat = scores[e, h, bi]
        vt = jnp.concatenate([vt_heads[e, h][:, rs], ones_ext], axis=0)
        den_i = jnp.sum(smat, axis=0, keepdims=True)
        num_i = jnp.dot(vt[0:hd].astype(BF16), smat.astype(BF16), preferred_element_type=F32)
        b_last = last_lane(b_row)
        m_u = last_lane(m_in)
        w_u = jnp.exp(b_last - b_row + li_row - m_u)
        upd = jnp.dot((vt * w_u).astype(BF16), k, preferred_element_type=F32)
        intra[e, h, bi] = (den_i, num_i, b_last, m_u, upd)
    for bi in range(nblk):
        rs = slice(bi * blk, (bi + 1) * blk)
        for e, h in [(e, h) for e in range(bb) for h in range(M_HEADS)]:
            ls = slice(h * hd, (h + 1) * hd)
            q = qkm_ref[e, rs, ls]
            b_row = bt_all[e][M_HEADS + h:M_HEADS + h + 1, rs]
            m_in = scores[e, h, bi][0]
            den_i, num_i, b_last, m_u, upd = intra[e, h, bi]
            m_prev = m_scr[e, h][:, 0:blk]
            cn_prev = cn_scr[e, h]
            a = b_row + m_prev
            m_t = jnp.maximum(a, m_in)
            alpha = jnp.exp(m_in - m_t)
            beta = jnp.exp(a - m_t)
            inter = _nt_dot(cn_prev.astype(BF16), q)
            num = alpha * num_i + beta * inter[0:hd]
            den = alpha * den_i + beta * inter[hd:hd + 1]
            hh = num / jnp.maximum(jnp.abs(den), jnp.exp(-m_t))
            hh = hh * lax.rsqrt(jnp.mean(hh * hh, axis=0, keepdims=True) + EPS)
            y = _transpose_cols(hh) * gain_ref[:, ls] * _sigmoid(om_ref[e, rs, ls])
            ym_ref[e, rs, ls] = y.astype(BF16)
            m_new = last_lane(m_t)
            g_state = jnp.exp(b_last + m_prev - m_new)[:, 0:1]
            g_upd = jnp.exp(m_u - m_new)[:, 0:1]
            cn_scr[e, h] = g_state * cn_prev + g_upd * upd
            m_scr[e, h] = jnp.broadcast_to(m_new[:, 0:1], (1, LANES))

    @pl.when(tb == pl.num_programs(1) - 1)
    def _():
        for e in range(bb):
            for h in range(M_HEADS):
                c1_ref[e, h] = cn_scr[e, h, 0:hd, :].T
                n1_ref[e, h] = cn_scr[e, h, hd:hd + 1, :]
        m1_ref[...] = m_scr[...]


def _block_tril(rt, blk):
    r = jnp.arange(rt)
    return ((r[:, None] // blk == r[None, :] // blk) & (r[None, :] <= r[:, None])).astype(BF16)


def _mlstm_call(qkm, vm, om, gates, gain, c0, n0, m0, bb, blk, nblk):
    b, t, _ = qkm.shape
    rt = blk * nblk
    assert t % rt == 0 and b % bb == 0
    mw, hd, nh = M_WIDTH, M_HEAD_DIM, M_HEADS

    def tok(width):
        return pl.BlockSpec((bb, rt, width), lambda i, j: (i, j, 0))

    cspec = pl.BlockSpec((bb, nh, hd, hd), lambda i, j: (i, 0, 0, 0))
    vspec = pl.BlockSpec((bb, nh, 1, hd), lambda i, j: (i, 0, 0, 0))
    return pl.pallas_call(
        functools.partial(_mlstm_kernel, bb=bb, blk=blk, nblk=nblk),
        grid=(b // bb, t // rt),
        in_specs=[tok(2 * mw), tok(mw), tok(mw), tok(LANES),
                  _const_spec((1, mw), lambda i, j: (0, 0)), _const_spec((rt, rt), lambda i, j: (0, 0)),
                  cspec, vspec, vspec],
        out_specs=[tok(mw), cspec, vspec, vspec],
        out_shape=[
            jax.ShapeDtypeStruct((b, t, mw), BF16),
            jax.ShapeDtypeStruct((b, nh, hd, hd), F32),
            jax.ShapeDtypeStruct((b, nh, 1, hd), F32),
            jax.ShapeDtypeStruct((b, nh, 1, hd), F32),
        ],
        scratch_shapes=[pltpu.VMEM((bb, nh, hd + STATE_EXT_ROWS, hd), F32), pltpu.VMEM((bb, nh, 1, hd), F32)],
        compiler_params=_params(("parallel", "arbitrary")),
        name="mlstm",
    )(qkm, vm, om, gates, gain, _block_tril(rt, blk), c0, n0, m0)


def _ffn_kernel(x_ref, ya_ref, ym_ref, gm_ref, gt1_ref, sh2_ref, sc2_ref, gt2_ref, gffn_ref, gfin_ref,
                wba_ref, wbm_ref, wo_ref, wgu_ref, wd_ref, o_ref, act_scr, *, bt, tt, final):
    d = x_ref.shape[-1]
    m = bt * tt
    dff = wd_ref.shape[1]
    x = x_ref[...]
    pa = jnp.dot(ya_ref[...].reshape(m, ATT_WIDTH), wba_ref[0], preferred_element_type=F32)
    pm = jnp.dot(ym_ref[...].reshape(m, M_WIDTH), wbm_ref[0], preferred_element_type=F32)
    ga = gm_ref[:, :, 0:d].reshape(m, d)
    gmm = gm_ref[:, :, d:2 * d].reshape(m, d)
    merged = _sigmoid(ga) * pa + _sigmoid(gmm) * pm
    y1 = jnp.dot(merged.astype(BF16), wo_ref[0], preferred_element_type=F32)
    x1 = x + gt1_ref[0, :, 0] * y1.reshape(bt, tt, d)
    ms = jnp.mean(x1 * x1, axis=-1, keepdims=True)
    h2 = x1 * lax.rsqrt(ms + EPS) * gffn_ref[...]
    h2 = (h2 * (1.0 + sc2_ref[0, :, 0]) + sh2_ref[0, :, 0]).reshape(m, d).astype(BF16)
    cw = 256
    for c in range(0, dff, cw):
        g = jnp.dot(h2, wgu_ref[0, :, c:c + cw], preferred_element_type=F32)
        up = jnp.dot(h2, wgu_ref[0, :, dff + c:dff + c + cw], preferred_element_type=F32)
        act_scr[:, c:c + cw] = (g * _sigmoid(g) * up).astype(BF16)
    y2 = jnp.dot(act_scr[...], wd_ref[0], preferred_element_type=F32)
    x2 = x1 + gt2_ref[0, :, 0] * y2.reshape(bt, tt, d)
    if final:
        ms2 = jnp.mean(x2 * x2, axis=-1, keepdims=True)
        x2 = x2 * lax.rsqrt(ms2 + EPS) * gfin_ref[...]
    o_ref[...] = x2


def _ffn_call(x, ya, ym, gm, mod, layer, row0, g_ffn, g_final, w_br_att, w_br_mlstm, w_out, w_gate_up,
              w_down, bt, tt, final):
    b, t, d = x.shape
    dff = w_down.shape[1]
    assert b % bt == 0 and t % tt == 0 and dff % 256 == 0

    def wspec(shape):
        return _const_spec((1,) + shape, lambda i, j: (layer, 0, 0))

    def tok(width):
        return pl.BlockSpec((bt, tt, width), lambda i, j: (i, j, 0))

    def modspec(k):
        return pl.BlockSpec((1, bt, 1, 1, d), lambda i, j: (layer, row0 // bt + i, k, 0, 0))

    def const2(shape):
        return _const_spec(shape, lambda i, j: (0, 0))

    return pl.pallas_call(
        functools.partial(_ffn_kernel, bt=bt, tt=tt, final=final),
        grid=(b // bt, t // tt),
        in_specs=[tok(d), tok(ATT_WIDTH), tok(M_WIDTH), tok(2 * d),
                  modspec(2), modspec(3), modspec(4), modspec(5), const2((1, d)), const2((1, d)),
                  wspec((ATT_WIDTH, d)), wspec((M_WIDTH, d)), wspec((d, d)), wspec((d, 2 * dff)),
                  wspec((dff, d))],
        out_specs=tok(d),
        out_shape=jax.ShapeDtypeStruct((b, t, d), F32),
        scratch_shapes=[pltpu.VMEM((bt * tt, dff), BF16)],
        compiler_params=_params(("parallel", "parallel")),
        name="merge_ffn",
    )(x, ya, ym, gm, mod, mod, mod, mod, g_ffn, g_final, w_br_att, w_br_mlstm, w_out, w_gate_up, w_down)


def _layer(x, mod, layer, row0, cache, conv_init, c0, n0, m0, wts, bt, tt, blk, nblk, final):
    (g_mix, w_main, w_gates, b_if, conv_w, conv_b, bias, mh_gain, w_br_att, w_br_mlstm, w_out, g_ffn,
     w_gate_up, w_down, g_final) = wts
    b, t, _ = x.shape
    qkv, kt, vt, qkm, vm, om, gm, gates, ctail = _inproj_call(
        x, mod, layer, row0, g_mix, w_main, w_gates, b_if, conv_w, conv_b, conv_init, bt, tt)
    if cache is None:
        ya = _attn_prompt_call(qkv, bias, layer)
    else:
        ya = _attn_sample_call(qkv, cache[0], cache[1], bias, layer, bt)
    mb = next(c for c in (MLSTM_BATCH_TILE, 2, 1) if b % c == 0)
    ym, c1, n1, m1 = _mlstm_call(qkm, vm, om, gates, mh_gain, c0, n0, m0, mb, blk, nblk)
    x = _ffn_call(x, ya, ym, gm, mod, layer, row0, g_ffn, g_final, w_br_att, w_br_mlstm, w_out, w_gate_up,
                  w_down, bt, tt, final)
    keep = kt.shape[1]
    state = (kt.reshape(b, keep, ATT_HEADS, ATT_HEAD_DIM), vt.reshape(b, keep, ATT_HEADS, ATT_HEAD_DIM),
             ctail[:, SUBLANES - (CONV_W - 1):, :], c1, n1[:, :, 0, :], m1[:, :, 0, 0])
    return x, state


def kernel(x_prompt, x_sample, cache_k, cache_v, state_conv, state_C, state_n, state_m, c_prompt, c_sample,
           w_ada, b_ada, g_mix, w_in, b_if, conv_w, conv_b, rel_bias, mh_gain, w_br_att, w_br_mlstm, w_out,
           g_ffn, w_gate_up, w_down, g_final):
    depth = w_ada.shape[0]
    bp, tp, d = x_prompt.shape
    bs, ts, _ = x_sample.shape
    assert tp % PROMPT_ROWS == 0 and tp % CHUNK == 0 and bs % SAMPLE_BATCH_TILE == 0 and ts % CHUNK != 0
    mw = M_WIDTH

    mod = _ada_call(jnp.concatenate([c_sample, c_prompt], axis=0), w_ada, b_ada)
    mod = mod.reshape(depth, bs + bp, 6, 1, d)

    bias = _bias_call(rel_bias)
    pc = cache_k.shape[2]
    caches = tuple(jnp.transpose(c, (0, 1, 3, 4, 2)).reshape(depth, bs, ATT_WIDTH, pc) for c in (cache_k, cache_v))

    pad3 = SUBLANES - (CONV_W - 1)
    zero_conv = jnp.zeros((bp, SUBLANES, 2 * mw), F32)
    zero_c = jnp.zeros((bp, M_HEADS, M_HEAD_DIM, M_HEAD_DIM), F32)
    zero_v = jnp.zeros((bp, M_HEADS, 1, M_HEAD_DIM), F32)

    w_main, w_gates = _wprep_call(w_in)
    wb_att, wb_mlstm, wb_out = w_br_att.astype(BF16), w_br_mlstm.astype(BF16), w_out.astype(BF16)
    wb_gate_up, wb_down = w_gate_up.astype(BF16), w_down.astype(BF16)

    xp, xs = x_prompt, x_sample
    outs_p, outs_s = [], []
    for l in range(depth):
        bif = jnp.pad(b_if[l], (0, LANES - 2 * M_HEADS)).reshape(1, LANES)
        wts = (g_mix[l].reshape(1, d), w_main, w_gates, bif, conv_w[l], conv_b[l].reshape(1, 2 * mw),
               bias, mh_gain[l].reshape(1, mw), wb_att, wb_mlstm, wb_out, g_ffn[l].reshape(1, d), wb_gate_up,
               wb_down, g_final.reshape(1, d))
        final = l == depth - 1
        xp, st_p = _layer(xp, mod, l, bs, None, zero_conv, zero_c, zero_v, zero_v, wts,
                          1, PROMPT_ROWS, MLSTM_BLOCK, MLSTM_TILE_ROWS // MLSTM_BLOCK, final)
        conv_init = jnp.pad(state_conv[l], ((0, 0), (pad3, 0), (0, 0)))
        m0 = jnp.broadcast_to(state_m[l][:, :, None, None], (bs, M_HEADS, 1, M_HEAD_DIM))
        xs, st_s = _layer(xs, mod, l, 0, caches, conv_init, state_C[l],
                          state_n[l][:, :, None, :], m0, wts, SAMPLE_BATCH_TILE, ts, ts, 1, final)
        outs_p.append(st_p)
        outs_s.append(st_s)

    def stk(outs, i):
        return jnp.stack([o[i] for o in outs], axis=0)

    return (xp, xs,
            stk(outs_p, 0), stk(outs_p, 1), stk(outs_p, 2), stk(outs_p, 3), stk(outs_p, 4), stk(outs_p, 5),
            stk(outs_s, 0), stk(outs_s, 1), stk(outs_s, 2), stk(outs_s, 3), stk(outs_s, 4), stk(outs_s, 5))
```

```python
import functools

import jax
import jax.numpy as jnp
from jax import lax
from jax.experimental import pallas as pl
from jax.experimental.pallas import tpu as pltpu

F32 = jnp.float32
BF16 = jnp.bfloat16

CHUNK = 64
N_PAST_CHUNKS = 8
ATT_HEADS = 8
ATT_HEAD_DIM = 64
ATT_WIDTH = ATT_HEADS * ATT_HEAD_DIM
MAX_REL = 256
M_HEADS = 4
M_HEAD_DIM = 128
M_WIDTH = M_HEADS * M_HEAD_DIM
CONV_W = 4
EPS = 1e-6
NEG = -1e30

LANES = 128
SUBLANES = 8
VMEM_LIMIT = 56 * 1024 * 1024
PROMPT_ROWS = 512
SAMPLE_BATCH_TILE = 16
SAMPLE_ATTN_BATCH_TILE = 8
MLSTM_BATCH_TILE = 4
BAND = (N_PAST_CHUNKS + 1) * CHUNK
N_PAIRS = ATT_HEADS // 2
ATTN_UNROLL = 8
ATTN_BATCH_TILE = 2
MLSTM_BLOCK = 128
MLSTM_TILE_ROWS = 512
STATE_EXT_ROWS = 16
LOG2E = 1.4426950408889634


def _sigmoid(x):
    return 1.0 / (1.0 + jnp.exp2(x * (-LOG2E)))


def _log_sigmoid(x):
    return jnp.minimum(x, 0.0) - jnp.log1p(jnp.exp(-jnp.abs(x)))


def _const_spec(shape, index_map):
    return pl.BlockSpec(shape, index_map, pipeline_mode=pl.Buffered(1))


def _params(sem):
    return pltpu.CompilerParams(dimension_semantics=sem, vmem_limit_bytes=VMEM_LIMIT)


def _ada_kernel(c_ref, w_ref, b_ref, o_ref):
    c = c_ref[...]
    a = (c * _sigmoid(c)).astype(BF16)
    o_ref[0] = jnp.dot(a, w_ref[0].astype(BF16), preferred_element_type=F32) + b_ref[0]


def _ada_call(c_all, w_ada, b_ada):
    depth, d, n = w_ada.shape
    r = c_all.shape[0]
    tn = 1536
    assert n % tn == 0
    return pl.pallas_call(
        _ada_kernel,
        grid=(depth, n // tn),
        in_specs=[
            pl.BlockSpec((r, d), lambda l, j: (0, 0)),
            pl.BlockSpec((1, d, tn), lambda l, j: (l, 0, j)),
            pl.BlockSpec((1, 1, tn), lambda l, j: (l, 0, j)),
        ],
        out_specs=pl.BlockSpec((1, r, tn), lambda l, j: (l, 0, j)),
        out_shape=jax.ShapeDtypeStruct((depth, r, n), F32),
        compiler_params=_params(("parallel", "parallel")),
        name="ada_mod",
    )(c_all, w_ada, b_ada.reshape(depth, 1, n))


def _wprep_kernel(wt_ref, main_ref, gates_ref):
    n_att = 3 * ATT_WIDTH + 4 * M_WIDTH
    n_g = 2 * M_HEADS
    n_main = main_ref.shape[2]
    step = 512
    for c0 in range(0, n_main, step):
        r0 = c0 if c0 < n_att else c0 + n_g
        main_ref[0, :, c0:c0 + step] = wt_ref[0, r0:r0 + step, :].T.astype(BF16)
    slab = wt_ref[0, n_att:n_att + LANES, :].T
    lane = lax.broadcasted_iota(jnp.int32, slab.shape, 1)
    gates_ref[0] = jnp.where(lane < n_g, slab, 0.0).astype(BF16)


def _wprep_call(w_in):
    depth, d, n = w_in.shape
    n_main = n - 2 * M_HEADS
    assert n_main % 512 == 0 and (3 * ATT_WIDTH + 4 * M_WIDTH) % 512 == 0
    return pl.pallas_call(
        _wprep_kernel,
        grid=(depth,),
        in_specs=[pl.BlockSpec((1, n, d), lambda l: (l, 0, 0), pipeline_mode=pl.Buffered(1))],
        out_specs=[pl.BlockSpec((1, d, n_main), lambda l: (l, 0, 0)),
                   pl.BlockSpec((1, d, LANES), lambda l: (l, 0, 0))],
        out_shape=[jax.ShapeDtypeStruct((depth, d, n_main), BF16), jax.ShapeDtypeStruct((depth, d, LANES), BF16)],
        compiler_params=_params(("parallel",)),
        name="w_in_prep",
    )(jnp.swapaxes(w_in, 1, 2))


def _inproj_kernel(x_ref, sh_ref, sc_ref, g_ref, w_ref, wg_ref, bif_ref, cw_ref, cb_ref, cinit_ref,
                   qkv_ref, kt_ref, vt_ref, qkm_ref, vm_ref, om_ref, gm_ref, gates_ref, ctail_ref,
                   conv_scr, hb_scr, *, bt, tt):
    t = pl.program_id(1)
    d = x_ref.shape[-1]
    m = bt * tt
    x = x_ref[...]
    ms = jnp.mean(x * x, axis=-1, keepdims=True)
    h = x * lax.rsqrt(ms + EPS) * g_ref[...]
    h = h * (1.0 + sc_ref[0, :, 0]) + sh_ref[0, :, 0]
    hb_scr[...] = h.reshape(m, d).astype(BF16)

    def proj(c0, width):
        return jnp.dot(hb_scr[...], w_ref[0, :, c0:c0 + width], preferred_element_type=F32)

    aw = ATT_WIDTH
    mw = M_WIDTH
    c0 = 3 * aw

    @pl.when(t == 0)
    def _():
        conv_scr[:, 0:SUBLANES, :] = cinit_ref[...]

    conv_scr[:, SUBLANES:SUBLANES + tt, :] = proj(c0, 2 * mw).reshape(bt, tt, 2 * mw)
    lo = SUBLANES - (CONV_W - 1)
    acc = conv_scr[:, lo:lo + tt, :] * cw_ref[0:1, :]
    for j in range(1, CONV_W):
        acc = acc + conv_scr[:, lo + j:lo + j + tt, :] * cw_ref[j:j + 1, :]
    acc = acc + cb_ref[...]
    act = acc * _sigmoid(acc)
    qkm_ref[:, :, 0:mw] = act[:, :, 0:mw].astype(BF16)
    qkm_ref[:, :, mw:2 * mw] = (act[:, :, mw:2 * mw] * (M_HEAD_DIM ** -0.5)).astype(BF16)
    tail = conv_scr[:, tt:tt + SUBLANES, :]
    ctail_ref[...] = tail
    conv_scr[:, 0:SUBLANES, :] = tail
    u = proj(0, 3 * aw)
    qkv_ref[:, :, 0:aw] = (u[:, 0:aw] * (LOG2E * ATT_HEAD_DIM ** -0.5)).astype(BF16).reshape(bt, tt, aw)
    qkv_ref[:, :, aw:3 * aw] = u[:, aw:3 * aw].astype(BF16).reshape(bt, tt, 2 * aw)
    kt_ref[...] = u[:, aw:2 * aw].reshape(bt, tt, aw)
    vt_ref[...] = u[:, 2 * aw:3 * aw].reshape(bt, tt, aw)
    c0 += 2 * mw
    u = proj(c0, 2 * mw)
    vm_ref[...] = u[:, 0:mw].astype(BF16).reshape(bt, tt, mw)
    om_ref[...] = u[:, mw:2 * mw].reshape(bt, tt, mw)
    c0 += 2 * mw
    for j in range(0, 2 * d, d):
        gm_ref[:, :, j:j + d] = proj(c0 + j, d).reshape(bt, tt, d)
    ug = jnp.dot(hb_scr[...], wg_ref[0], preferred_element_type=F32) + bif_ref[...]
    col = lax.broadcasted_iota(jnp.int32, ug.shape, 1)
    gates_ref[...] = jnp.where(col >= M_HEADS, _log_sigmoid(ug), ug).reshape(bt, tt, LANES)


def _inproj_call(x, mod, layer, row0, g_mix, w_main, w_gates, b_if, conv_w, conv_b, conv_init, bt, tt):
    b, t, d = x.shape
    nb, nt = b // bt, t // tt
    keep = min(N_PAST_CHUNKS * CHUNK, t)
    assert b % bt == 0 and t % tt == 0 and keep % tt == 0
    t_keep0 = nt - keep // tt
    n_main = w_main.shape[2]
    aw, mw = ATT_WIDTH, M_WIDTH

    def wspec(shape):
        return _const_spec((1,) + shape, lambda i, j: (layer, 0, 0))

    def tok(width):
        return pl.BlockSpec((bt, tt, width), lambda i, j: (i, j, 0))

    def modspec(k):
        return pl.BlockSpec((1, bt, 1, 1, d), lambda i, j: (layer, row0 // bt + i, k, 0, 0))

    def const2(shape):
        return _const_spec(shape, lambda i, j: (0, 0))

    keep_spec = pl.BlockSpec((bt, tt, aw), lambda i, j: (i, jnp.maximum(j - t_keep0, 0), 0))
    per_batch8 = pl.BlockSpec((bt, SUBLANES, 2 * mw), lambda i, j: (i, 0, 0))
    outs = pl.pallas_call(
        functools.partial(_inproj_kernel, bt=bt, tt=tt),
        grid=(nb, nt),
        in_specs=[
            tok(d), modspec(0), modspec(1), const2((1, d)),
            wspec((d, n_main)), wspec((d, LANES)), const2((1, LANES)),
            const2((CONV_W, 2 * mw)), const2((1, 2 * mw)), per_batch8,
        ],
        out_specs=[tok(3 * aw), keep_spec, keep_spec, tok(2 * mw), tok(mw), tok(mw), tok(2 * d),
                   tok(LANES), per_batch8],
        out_shape=[
            jax.ShapeDtypeStruct((b, t, 3 * aw), BF16),
            jax.ShapeDtypeStruct((b, keep, aw), F32),
            jax.ShapeDtypeStruct((b, keep, aw), F32),
            jax.ShapeDtypeStruct((b, t, 2 * mw), BF16),
            jax.ShapeDtypeStruct((b, t, mw), BF16),
            jax.ShapeDtypeStruct((b, t, mw), F32),
            jax.ShapeDtypeStruct((b, t, 2 * d), F32),
            jax.ShapeDtypeStruct((b, t, LANES), F32),
            jax.ShapeDtypeStruct((b, SUBLANES, 2 * mw), F32),
        ],
        scratch_shapes=[pltpu.VMEM((bt, SUBLANES + tt, 2 * mw), F32), pltpu.VMEM((bt * tt, d), BF16)],
        compiler_params=_params(("parallel", "arbitrary")),
        name="inproj",
    )(x, mod, mod, g_mix, w_main, w_gates, b_if, conv_w, conv_b, conv_init)
    return outs


def _bias_kernel(rev_ref, o_ref):
    w = rev_ref.shape[-1]
    x = jnp.broadcast_to(rev_ref[0, 0], (CHUNK, w))
    r = pltpu.roll(x, w - (CHUNK - 1), 1, stride=1, stride_axis=0)
    o_ref[0, 0] = r[:, 0:BAND] * LOG2E


def _bias_call(rel_bias):
    depth, nh, rel = rel_bias.shape
    assert rel == MAX_REL + CHUNK and nh == ATT_HEADS
    n_ext = BAND + CHUNK - 1
    w = -(-n_ext // LANES) * LANES
    ext = jnp.concatenate([rel_bias, jnp.broadcast_to(rel_bias[:, :, rel - 1:], (depth, nh, n_ext - rel))], axis=2)
    rev = jnp.pad(ext[:, :, ::-1], ((0, 0), (0, 0), (0, w - n_ext))).reshape(depth, nh, 1, w)
    out = pl.pallas_call(
        _bias_kernel,
        grid=(depth, nh),
        in_specs=[pl.BlockSpec((1, 1, 1, w), lambda l, h: (l, h, 0, 0))],
        out_specs=pl.BlockSpec((1, 1, CHUNK, BAND), lambda l, h: (l, h, 0, 0)),
        out_shape=jax.ShapeDtypeStruct((depth, nh, CHUNK, BAND), F32),
        compiler_params=_params(("parallel", "parallel")),
        name="rel_bias",
    )(rev)
    return out.reshape(depth, N_PAIRS, 2 * CHUNK, BAND)


def _pair_rows(qp):
    lane = lax.broadcasted_iota(jnp.int32, qp.shape, 1)
    zero = jnp.zeros_like(qp)
    return jnp.concatenate([jnp.where(lane < ATT_HEAD_DIM, qp, zero),
                            jnp.where(lane >= ATT_HEAD_DIM, qp, zero)], axis=0)


def _pair_merge(o, rows):
    lane = lax.broadcasted_iota(jnp.int32, (rows, LANES), 1)
    return jnp.where(lane < ATT_HEAD_DIM, o[0:rows], o[rows:2 * rows])


def _nt_dot(a, b):
    return lax.dot_general(a, b, (((1,), (1,)), ((), ())), preferred_element_type=F32)


def _attn_prompt_kernel(q_ref, kw_ref, vw_ref, bias_ref, o_ref, kscr, vscr, *, unroll):
    j = pl.program_id(1)
    ab, rows = q_ref.shape[0], q_ref.shape[1]
    past = N_PAST_CHUNKS * CHUNK
    assert rows == past and kw_ref.shape[1] == past + rows

    def chunks(i, masked):
        kband, vband = (kscr, vscr) if masked else (kw_ref, vw_ref)
        units = [(e, pl.multiple_of((i * unroll + u) * CHUNK, CHUNK), i * unroll + u, p)
                 for e in range(ab) for u in range(unroll) for p in range(N_PAIRS)]
        scores = []
        for e, r0, ci, p in units:
            ls = slice(p * LANES, (p + 1) * LANES)
            qs = _pair_rows(q_ref[e, pl.ds(r0, CHUNK), ls])
            scores.append(_nt_dot(qs, kband[e, pl.ds(r0, BAND), ls]) + bias_ref[0, p])
        for (e, r0, ci, p), s in zip(units, scores):
            ls = slice(p * LANES, (p + 1) * LANES)
            if masked:
                col = lax.broadcasted_iota(jnp.int32, s.shape, 1)
                s = jnp.where(col >= (N_PAST_CHUNKS - ci) * CHUNK, s, NEG)
            mx = jnp.max(s, axis=-1, keepdims=True)
            ex = jnp.exp2(s - mx)
            den = jnp.sum(ex, axis=-1, keepdims=True)
            o = jnp.dot(ex.astype(BF16), vband[e, pl.ds(r0, BAND), ls], preferred_element_type=F32) / den
            o_ref[e, pl.ds(r0, CHUNK), ls] = _pair_merge(o, CHUNK).astype(BF16)

    n_iter = rows // (CHUNK * unroll)

    @pl.when(j == 0)
    def _():
        kscr[:, 0:past, :] = jnp.zeros((ab, past, ATT_WIDTH), BF16)
        vscr[:, 0:past, :] = jnp.zeros((ab, past, ATT_WIDTH), BF16)
        kscr[:, past:past + rows, :] = kw_ref[:, 0:rows, :]
        vscr[:, past:past + rows, :] = vw_ref[:, 0:rows, :]
        lax.fori_loop(0, n_iter, lambda i, c: (chunks(i, True), c)[1], 0)

    @pl.when(j > 0)
    def _():
        lax.fori_loop(0, n_iter, lambda i, c: (chunks(i, False), c)[1], 0)


def _attn_prompt_call(qkv, bias, layer):
    b, t, _ = qkv.shape
    rows = N_PAST_CHUNKS * CHUNK
    assert t % rows == 0 and t >= 2 * rows
    aw = ATT_WIDTH
    ab = next(c for c in (ATTN_BATCH_TILE, 1) if b % c == 0)

    def cur(cb):
        return pl.BlockSpec((ab, rows, aw), lambda i, j: (i, j, cb))

    def window(cb):
        return pl.BlockSpec((pl.Element(ab), pl.Element(2 * rows), pl.Element(aw)),
                            lambda i, j: (i * ab, jnp.maximum(j - 1, 0) * rows, cb * aw))

    return pl.pallas_call(
        functools.partial(_attn_prompt_kernel, unroll=ATTN_UNROLL),
        grid=(b // ab, t // rows),
        in_specs=[cur(0), window(1), window(2),
                  _const_spec((1, N_PAIRS, 2 * CHUNK, BAND), lambda i, j: (layer, 0, 0, 0))],
        out_specs=pl.BlockSpec((ab, rows, aw), lambda i, j: (i, j, 0)),
        out_shape=jax.ShapeDtypeStruct((b, t, aw), BF16),
        scratch_shapes=[pltpu.VMEM((ab, 2 * rows, aw), BF16), pltpu.VMEM((ab, 2 * rows, aw), BF16)],
        compiler_params=_params(("parallel", "parallel")),
        name="attn_prompt",
    )(qkv, qkv, qkv, bias)


def _attn_sample_kernel(q_ref, kn_ref, vn_ref, kc_ref, vc_ref, bias_ref, o_ref, *, bb):
    tq = q_ref.shape[1]
    pc = kc_ref.shape[3]
    for bi in range(bb):
        scores = []
        for p in range(N_PAIRS):
            ls = slice(p * LANES, (p + 1) * LANES)
            qs = _pair_rows(q_ref[bi, :, ls])
            bias = jnp.concatenate([bias_ref[0, p, 0:tq, :], bias_ref[0, p, CHUNK:CHUNK + tq, :]], axis=0)
            s1 = jnp.dot(qs, kc_ref[0, bi, ls, :].astype(BF16), preferred_element_type=F32)
            scores.append((s1 + bias[:, 0:pc], _nt_dot(qs, kn_ref[bi, :, ls]) + bias[:, pc:pc + tq]))
        for p, (s1, s2) in enumerate(scores):
            ls = slice(p * LANES, (p + 1) * LANES)
            mx = jnp.maximum(jnp.max(s1, axis=-1, keepdims=True), jnp.max(s2, axis=-1, keepdims=True))
            e1 = jnp.exp2(s1 - mx)
            e2 = jnp.exp2(s2 - mx)
            den = jnp.sum(e1, axis=-1, keepdims=True) + jnp.sum(e2, axis=-1, keepdims=True)
            o = (_nt_dot(e1.astype(BF16), vc_ref[0, bi, ls, :].astype(BF16))
                 + jnp.dot(e2.astype(BF16), vn_ref[bi, :, ls], preferred_element_type=F32)) / den
            o_ref[bi, :, ls] = _pair_merge(o, tq).astype(BF16)


def _attn_sample_call(qkv, cache_k, cache_v, bias, layer, bb):
    b, tq, _ = qkv.shape
    _, _, aw, pc = cache_k.shape
    assert b % bb == 0 and pc == N_PAST_CHUNKS * CHUNK and tq <= CHUNK and aw == ATT_WIDTH

    def new(cb):
        return pl.BlockSpec((bb, tq, aw), lambda i: (i, 0, cb))

    cache = pl.BlockSpec((1, bb, aw, pc), lambda i: (layer, i, 0, 0))
    return pl.pallas_call(
        functools.partial(_attn_sample_kernel, bb=bb),
        grid=(b // bb,),
        in_specs=[new(0), new(1), new(2), cache, cache,
                  _const_spec((1, N_PAIRS, 2 * CHUNK, BAND), lambda i: (layer, 0, 0, 0))],
        out_specs=pl.BlockSpec((bb, tq, aw), lambda i: (i, 0, 0)),
        out_shape=jax.ShapeDtypeStruct((b, tq, aw), BF16),
        compiler_params=_params(("parallel",)),
        name="attn_sample",
    )(qkv, qkv, qkv, cache_k, cache_v, bias)


def _transpose_rows(a):
    r = a.shape[0]
    pad = -r % LANES
    if pad:
        a = jnp.concatenate([a, jnp.zeros((pad, a.shape[1]), a.dtype)], axis=0)
    return a.T[:, 0:r]


def _transpose_cols(a):
    c = a.shape[1]
    pad = -c % LANES
    if pad:
        a = jnp.concatenate([a, jnp.zeros((a.shape[0], pad), a.dtype)], axis=1)
    return a.T[0:c, :]


def _block_cumsum(tri, g):
    g1 = g.astype(BF16)
    r1 = g - g1.astype(F32)
    g2 = r1.astype(BF16)
    g3 = (r1 - g2.astype(F32)).astype(BF16)
    return (jnp.dot(tri, g1, preferred_element_type=F32) + jnp.dot(tri, g2, preferred_element_type=F32)
            + jnp.dot(tri, g3, preferred_element_type=F32))


def _mlstm_kernel(qkm_ref, vm_ref, om_ref, gates_ref, gain_ref, tri_ref, c0_ref, n0_ref, m0_ref,
                  ym_ref, c1_ref, n1_ref, m1_ref, cn_scr, m_scr, *, bb, blk, nblk):
    tb = pl.program_id(1)
    hd = M_HEAD_DIM
    ext = cn_scr.shape[2] - hd

    def last_lane(r):
        return jnp.broadcast_to(r[:, blk - 1:blk], r.shape)

    ext_row0 = lax.broadcasted_iota(jnp.int32, (ext, hd), 0) == 0

    @pl.when(tb == 0)
    def _():
        for e in range(bb):
            for h in range(M_HEADS):
                cn_scr[e, h, 0:hd, :] = c0_ref[e, h].T
                cn_scr[e, h, hd:hd + ext, :] = jnp.where(ext_row0, n0_ref[e, h], 0.0)
        m_scr[...] = m0_ref[...]

    srow = lax.broadcasted_iota(jnp.int32, (blk, blk), 0)
    tcol = lax.broadcasted_iota(jnp.int32, (blk, blk), 1)
    causal = srow <= tcol
    ones_ext = jnp.where(lax.broadcasted_iota(jnp.int32, (ext, blk), 0) == 0, 1.0, 0.0)

    gt_all, bt_all, c_all = [], [], []
    for e in range(bb):
        g = gates_ref[e]
        b = _block_cumsum(tri_ref[...], g)
        gt_all.append(_transpose_rows(g))
        bt_all.append(_transpose_rows(b))
        c_all.append(g[:, 0:M_HEADS] - b[:, M_HEADS:2 * M_HEADS])

    units = [(e, h, bi) for e in range(bb) for h in range(M_HEADS) for bi in range(nblk)]
    vt_heads = {(e, h): _transpose_rows(vm_ref[e, :, h * hd:(h + 1) * hd].astype(F32))
                for e in range(bb) for h in range(M_HEADS)}
    scores = {}
    for e, h, bi in units:
        rs = slice(bi * blk, (bi + 1) * blk)
        q = qkm_ref[e, rs, h * hd:(h + 1) * hd]
        k = qkm_ref[e, rs, M_WIDTH + h * hd:M_WIDTH + (h + 1) * hd]
        b_row = bt_all[e][M_HEADS + h:M_HEADS + h + 1, rs]
        dmat = jnp.where(causal, b_row + c_all[e][rs, h:h + 1], NEG)
        m_in = jnp.max(dmat, axis=0, keepdims=True)
        scores[e, h, bi] = (m_in, _nt_dot(k, q) * jnp.exp(dmat - m_in))
    intra = {}
    for e, h, bi in units:
        rs = slice(bi * blk, (bi + 1) * blk)
        k = qkm_ref[e, rs, M_WIDTH + h * hd:M_WIDTH + (h + 1) * hd]
        li_row, b_row = gt_all[e][h:h + 1, rs], bt_all[e][M_HEADS + h:M_HEADS + h + 1, rs]
        m_in, smat = scores[e, h, bi]
        vt = jnp.concatenate([vt_heads[e, h][:, rs], ones_ext], axis=0)
        den_i = jnp.sum(smat, axis=0, keepdims=True)
        num_i = jnp.dot(vt[0:hd].astype(BF16), smat.astype(BF16), preferred_element_type=F32)
        b_last = last_lane(b_row)
        m_u = last_lane(m_in)
        w_u = jnp.exp(b_last - b_row + li_row - m_u)
        upd = jnp.dot((vt * w_u).astype(BF16), k, preferred_element_type=F32)
        intra[e, h, bi] = (den_i, num_i, b_last, m_u, upd)
    for bi in range(nblk):
        rs = slice(bi * blk, (bi + 1) * blk)
        for e, h in [(e, h) for e in range(bb) for h in range(M_HEADS)]:
            ls = slice(h * hd, (h + 1) * hd)
            q = qkm_ref[e, rs, ls]
            b_row = bt_all[e][M_HEADS + h:M_HEADS + h + 1, rs]
            m_in = scores[e, h, bi][0]
            den_i, num_i, b_last, m_u, upd = intra[e, h, bi]
            m_prev = m_scr[e, h][:, 0:blk]
            cn_prev = cn_scr[e, h]
            a = b_row + m_prev
            m_t = jnp.maximum(a, m_in)
            alpha = jnp.exp(m_in - m_t)
            beta = jnp.exp(a - m_t)
            inter = _nt_dot(cn_prev.astype(BF16), q)
            num = alpha * num_i + beta * inter[0:hd]
            den = alpha * den_i + beta * inter[hd:hd + 1]
            hh = num / jnp.maximum(jnp.abs(den), jnp.exp(-m_t))
            hh = hh * lax.rsqrt(jnp.mean(hh * hh, axis=0, keepdims=True) + EPS)
            y = _transpose_cols(hh) * gain_ref[:, ls] * _sigmoid(om_ref[e, rs, ls])
            ym_ref[e, rs, ls] = y.astype(BF16)
            m_new = last_lane(m_t)
            g_state = jnp.exp(b_last + m_prev - m_new)[:, 0:1]
            g_upd = jnp.exp(m_u - m_new)[:, 0:1]
            cn_scr[e, h] = g_state * cn_prev + g_upd * upd
            m_scr[e, h] = jnp.broadcast_to(m_new[:, 0:1], (1, LANES))

    @pl.when(tb == pl.num_programs(1) - 1)
    def _():
        for e in range(bb):
            for h in range(M_HEADS):
                c1_ref[e, h] = cn_scr[e, h, 0:hd, :].T
                n1_ref[e, h] = cn_scr[e, h, hd:hd + 1, :]
        m1_ref[...] = m_scr[...]


def _block_tril(rt, blk):
    r = jnp.arange(rt)
    return ((r[:, None] // blk == r[None, :] // blk) & (r[None, :] <= r[:, None])).astype(BF16)


def _mlstm_call(qkm, vm, om, gates, gain, c0, n0, m0, bb, blk, nblk):
    b, t, _ = qkm.shape
    rt = blk * nblk
    assert t % rt == 0 and b % bb == 0
    mw, hd, nh = M_WIDTH, M_HEAD_DIM, M_HEADS

    def tok(width):
        return pl.BlockSpec((bb, rt, width), lambda i, j: (i, j, 0))

    cspec = pl.BlockSpec((bb, nh, hd, hd), lambda i, j: (i, 0, 0, 0))
    vspec = pl.BlockSpec((bb, nh, 1, hd), lambda i, j: (i, 0, 0, 0))
    return pl.pallas_call(
        functools.partial(_mlstm_kernel, bb=bb, blk=blk, nblk=nblk),
        grid=(b // bb, t // rt),
        in_specs=[tok(2 * mw), tok(mw), tok(mw), tok(LANES),
                  _const_spec((1, mw), lambda i, j: (0, 0)), _const_spec((rt, rt), lambda i, j: (0, 0)),
                  cspec, vspec, vspec],
        out_specs=[tok(mw), cspec, vspec, vspec],
        out_shape=[
            jax.ShapeDtypeStruct((b, t, mw), BF16),
            jax.ShapeDtypeStruct((b, nh, hd, hd), F32),
            jax.ShapeDtypeStruct((b, nh, 1, hd), F32),
            jax.ShapeDtypeStruct((b, nh, 1, hd), F32),
        ],
        scratch_shapes=[pltpu.VMEM((bb, nh, hd + STATE_EXT_ROWS, hd), F32), pltpu.VMEM((bb, nh, 1, hd), F32)],
        compiler_params=_params(("parallel", "arbitrary")),
        name="mlstm",
    )(qkm, vm, om, gates, gain, _block_tril(rt, blk), c0, n0, m0)


def _ffn_kernel(x_ref, ya_ref, ym_ref, gm_ref, gt1_ref, sh2_ref, sc2_ref, gt2_ref, gffn_ref, gfin_ref,
                wba_ref, wbm_ref, wo_ref, wgu_ref, wd_ref, o_ref, act_scr, *, bt, tt, final):
    d = x_ref.shape[-1]
    m = bt * tt
    dff = wd_ref.shape[1]
    x = x_ref[...]
    pa = jnp.dot(ya_ref[...].reshape(m, ATT_WIDTH), wba_ref[0], preferred_element_type=F32)
    pm = jnp.dot(ym_ref[...].reshape(m, M_WIDTH), wbm_ref[0], preferred_element_type=F32)
    ga = gm_ref[:, :, 0:d].reshape(m, d)
    gmm = gm_ref[:, :, d:2 * d].reshape(m, d)
    merged = _sigmoid(ga) * pa + _sigmoid(gmm) * pm
    y1 = jnp.dot(merged.astype(BF16), wo_ref[0], preferred_element_type=F32)
    x1 = x + gt1_ref[0, :, 0] * y1.reshape(bt, tt, d)
    ms = jnp.mean(x1 * x1, axis=-1, keepdims=True)
    h2 = x1 * lax.rsqrt(ms + EPS) * gffn_ref[...]
    h2 = (h2 * (1.0 + sc2_ref[0, :, 0]) + sh2_ref[0, :, 0]).reshape(m, d).astype(BF16)
    cw = 256
    for c in range(0, dff, cw):
        g = jnp.dot(h2, wgu_ref[0, :, c:c + cw], preferred_element_type=F32)
        up = jnp.dot(h2, wgu_ref[0, :, dff + c:dff + c + cw], preferred_element_type=F32)
        act_scr[:, c:c + cw] = (g * _sigmoid(g) * up).astype(BF16)
    y2 = jnp.dot(act_scr[...], wd_ref[0], preferred_element_type=F32)
    x2 = x1 + gt2_ref[0, :, 0] * y2.reshape(bt, tt, d)
    if final:
        ms2 = jnp.mean(x2 * x2, axis=-1, keepdims=True)
        x2 = x2 * lax.rsqrt(ms2 + EPS) * gfin_ref[...]
    o_ref[...] = x2


def _ffn_call(x, ya, ym, gm, mod, layer, row0, g_ffn, g_final, w_br_att, w_br_mlstm, w_out, w_gate_up,
              w_down, bt, tt, final):
    b, t, d = x.shape
    dff = w_down.shape[1]
    assert b % bt == 0 and t % tt == 0 and dff % 256 == 0

    def wspec(shape):
        return _const_spec((1,) + shape, lambda i, j: (layer, 0, 0))

    def tok(width):
        return pl.BlockSpec((bt, tt, width), lambda i, j: (i, j, 0))

    def modspec(k):
        return pl.BlockSpec((1, bt, 1, 1, d), lambda i, j: (layer, row0 // bt + i, k, 0, 0))

    def const2(shape):
        return _const_spec(shape, lambda i, j: (0, 0))

    return pl.pallas_call(
        functools.partial(_ffn_kernel, bt=bt, tt=tt, final=final),
        grid=(b // bt, t // tt),
        in_specs=[tok(d), tok(ATT_WIDTH), tok(M_WIDTH), tok(2 * d),
                  modspec(2), modspec(3), modspec(4), modspec(5), const2((1, d)), const2((1, d)),
                  wspec((ATT_WIDTH, d)), wspec((M_WIDTH, d)), wspec((d, d)), wspec((d, 2 * dff)),
                  wspec((dff, d))],
        out_specs=tok(d),
        out_shape=jax.ShapeDtypeStruct((b, t, d), F32),
        scratch_shapes=[pltpu.VMEM((bt * tt, dff), BF16)],
        compiler_params=_params(("parallel", "parallel")),
        name="merge_ffn",
    )(x, ya, ym, gm, mod, mod, mod, mod, g_ffn, g_final, w_br_att, w_br_mlstm, w_out, w_gate_up, w_down)


def _layer(x, mod, layer, row0, cache, conv_init, c0, n0, m0, wts, bt, tt, blk, nblk, final):
    (g_mix, w_main, w_gates, b_if, conv_w, conv_b, bias, mh_gain, w_br_att, w_br_mlstm, w_out, g_ffn,
     w_gate_up, w_down, g_final) = wts
    b, t, _ = x.shape
    qkv, kt, vt, qkm, vm, om, gm, gates, ctail = _inproj_call(
        x, mod, layer, row0, g_mix, w_main, w_gates, b_if, conv_w, conv_b, conv_init, bt, tt)
    if cache is None:
        ya = _attn_prompt_call(qkv, bias, layer)
    else:
        ya = _attn_sample_call(qkv, cache[0], cache[1], bias, layer, min(bt, SAMPLE_ATTN_BATCH_TILE))
    mb = next(c for c in (MLSTM_BATCH_TILE, 2, 1) if b % c == 0)
    ym, c1, n1, m1 = _mlstm_call(qkm, vm, om, gates, mh_gain, c0, n0, m0, mb, blk, nblk)
    x = _ffn_call(x, ya, ym, gm, mod, layer, row0, g_ffn, g_final, w_br_att, w_br_mlstm, w_out, w_gate_up,
                  w_down, bt, tt, final)
    keep = kt.shape[1]
    state = (kt.reshape(b, keep, ATT_HEADS, ATT_HEAD_DIM), vt.reshape(b, keep, ATT_HEADS, ATT_HEAD_DIM),
             ctail[:, SUBLANES - (CONV_W - 1):, :], c1, n1[:, :, 0, :], m1[:, :, 0, 0])
    return x, state


def kernel(x_prompt, x_sample, cache_k, cache_v, state_conv, state_C, state_n, state_m, c_prompt, c_sample,
           w_ada, b_ada, g_mix, w_in, b_if, conv_w, conv_b, rel_bias, mh_gain, w_br_att, w_br_mlstm, w_out,
           g_ffn, w_gate_up, w_down, g_final):
    depth = w_ada.shape[0]
    bp, tp, d = x_prompt.shape
    bs, ts, _ = x_sample.shape
    assert tp % PROMPT_ROWS == 0 and tp % CHUNK == 0 and bs % SAMPLE_BATCH_TILE == 0 and ts % CHUNK != 0
    mw = M_WIDTH

    mod = _ada_call(jnp.concatenate([c_sample, c_prompt], axis=0), w_ada, b_ada)
    mod = mod.reshape(depth, bs + bp, 6, 1, d)

    bias = _bias_call(rel_bias)
    pc = cache_k.shape[2]
    caches = tuple(jnp.transpose(c, (0, 1, 3, 4, 2)).reshape(depth, bs, ATT_WIDTH, pc) for c in (cache_k, cache_v))

    pad3 = SUBLANES - (CONV_W - 1)
    zero_conv = jnp.zeros((bp, SUBLANES, 2 * mw), F32)
    zero_c = jnp.zeros((bp, M_HEADS, M_HEAD_DIM, M_HEAD_DIM), F32)
    zero_v = jnp.zeros((bp, M_HEADS, 1, M_HEAD_DIM), F32)

    w_main, w_gates = _wprep_call(w_in)
    wb_att, wb_mlstm, wb_out = w_br_att.astype(BF16), w_br_mlstm.astype(BF16), w_out.astype(BF16)
    wb_gate_up, wb_down = w_gate_up.astype(BF16), w_down.astype(BF16)

    xp, xs = x_prompt, x_sample
    outs_p, outs_s = [], []
    for l in range(depth):
        bif = jnp.pad(b_if[l], (0, LANES - 2 * M_HEADS)).reshape(1, LANES)
        wts = (g_mix[l].reshape(1, d), w_main, w_gates, bif, conv_w[l], conv_b[l].reshape(1, 2 * mw),
               bias, mh_gain[l].reshape(1, mw), wb_att, wb_mlstm, wb_out, g_ffn[l].reshape(1, d), wb_gate_up,
               wb_down, g_final.reshape(1, d))
        final = l == depth - 1
        xp, st_p = _layer(xp, mod, l, bs, None, zero_conv, zero_c, zero_v, zero_v, wts,
                          1, PROMPT_ROWS, MLSTM_BLOCK, MLSTM_TILE_ROWS // MLSTM_BLOCK, final)
        conv_init = jnp.pad(state_conv[l], ((0, 0), (pad3, 0), (0, 0)))
        m0 = jnp.broadcast_to(state_m[l][:, :, None, None], (bs, M_HEADS, 1, M_HEAD_DIM))
        xs, st_s = _layer(xs, mod, l, 0, caches, conv_init, state_C[l],
                          state_n[l][:, :, None, :], m0, wts, SAMPLE_BATCH_TILE, ts, ts, 1, final)
        outs_p.append(st_p)
        outs_s.append(st_s)

    def stk(outs, i):
        return jnp.stack([o[i] for o in outs], axis=0)

    return (xp, xs,
            stk(outs_p, 0), stk(outs_p, 1), stk(outs_p, 2), stk(outs_p, 3), stk(outs_p, 4), stk(outs_p, 5),
            stk(outs_s, 0), stk(outs_s, 1), stk(outs_s, 2), stk(outs_s, 3), stk(outs_s, 4), stk(outs_s, 5))
```

```python
import functools

import jax
import jax.numpy as jnp
from jax import lax
from jax.experimental import pallas as pl
from jax.experimental.pallas import tpu as pltpu

F32 = jnp.float32
BF16 = jnp.bfloat16

CHUNK = 64
N_PAST_CHUNKS = 8
ATT_HEADS = 8
ATT_HEAD_DIM = 64
ATT_WIDTH = ATT_HEADS * ATT_HEAD_DIM
MAX_REL = 256
M_HEADS = 4
M_HEAD_DIM = 128
M_WIDTH = M_HEADS * M_HEAD_DIM
CONV_W = 4
EPS = 1e-6
NEG = -1e30

LANES = 128
SUBLANES = 8
VMEM_LIMIT = 56 * 1024 * 1024
PROMPT_ROWS = 512
CONV_PIECE_ROWS = 64
SAMPLE_BATCH_TILE = 8
MLSTM_BATCH_TILE = 4
BAND = (N_PAST_CHUNKS + 1) * CHUNK
N_PAIRS = ATT_HEADS // 2
ATTN_UNROLL = 8
ATTN_BATCH_TILE = 2
MLSTM_BLOCK = 128
MLSTM_TILE_ROWS = 512
STATE_EXT_ROWS = 16
LOG2E = 1.4426950408889634


def _sigmoid(x):
    return 1.0 / (1.0 + jnp.exp2(x * (-LOG2E)))


def _log_sigmoid(x):
    return jnp.minimum(x, 0.0) - jnp.log1p(jnp.exp(-jnp.abs(x)))


def _const_spec(shape, index_map):
    return pl.BlockSpec(shape, index_map, pipeline_mode=pl.Buffered(1))


def _params(sem):
    return pltpu.CompilerParams(dimension_semantics=sem, vmem_limit_bytes=VMEM_LIMIT)


def _ada_kernel(c_ref, w_ref, b_ref, o_ref):
    c = c_ref[...]
    a = (c * _sigmoid(c)).astype(BF16)
    o_ref[0] = jnp.dot(a, w_ref[0].astype(BF16), preferred_element_type=F32) + b_ref[0]


def _ada_call(c_all, w_ada, b_ada):
    depth, d, n = w_ada.shape
    r = c_all.shape[0]
    tn = 1536
    assert n % tn == 0
    return pl.pallas_call(
        _ada_kernel,
        grid=(depth, n // tn),
        in_specs=[
            pl.BlockSpec((r, d), lambda l, j: (0, 0)),
            pl.BlockSpec((1, d, tn), lambda l, j: (l, 0, j)),
            pl.BlockSpec((1, 1, tn), lambda l, j: (l, 0, j)),
        ],
        out_specs=pl.BlockSpec((1, r, tn), lambda l, j: (l, 0, j)),
        out_shape=jax.ShapeDtypeStruct((depth, r, n), F32),
        compiler_params=_params(("parallel", "parallel")),
        name="ada_mod",
    )(c_all, w_ada, b_ada.reshape(depth, 1, n))


def _wprep_kernel(wt_ref, main_ref, gates_ref):
    n_att = 3 * ATT_WIDTH + 4 * M_WIDTH
    n_g = 2 * M_HEADS
    n_main = main_ref.shape[2]
    step = 512
    for c0 in range(0, n_main, step):
        r0 = c0 if c0 < n_att else c0 + n_g
        main_ref[0, :, c0:c0 + step] = wt_ref[0, r0:r0 + step, :].T.astype(BF16)
    slab = wt_ref[0, n_att:n_att + LANES, :].T
    lane = lax.broadcasted_iota(jnp.int32, slab.shape, 1)
    gates_ref[0] = jnp.where(lane < n_g, slab, 0.0).astype(BF16)


def _wprep_call(w_in):
    depth, d, n = w_in.shape
    n_main = n - 2 * M_HEADS
    assert n_main % 512 == 0 and (3 * ATT_WIDTH + 4 * M_WIDTH) % 512 == 0
    return pl.pallas_call(
        _wprep_kernel,
        grid=(depth,),
        in_specs=[pl.BlockSpec((1, n, d), lambda l: (l, 0, 0), pipeline_mode=pl.Buffered(1))],
        out_specs=[pl.BlockSpec((1, d, n_main), lambda l: (l, 0, 0)),
                   pl.BlockSpec((1, d, LANES), lambda l: (l, 0, 0))],
        out_shape=[jax.ShapeDtypeStruct((depth, d, n_main), BF16), jax.ShapeDtypeStruct((depth, d, LANES), BF16)],
        compiler_params=_params(("parallel",)),
        name="w_in_prep",
    )(jnp.swapaxes(w_in, 1, 2))


def _inproj_kernel(x_ref, sh_ref, sc_ref, g_ref, w_ref, wg_ref, bif_ref, cw_ref, cb_ref, cinit_ref,
                   qkv_ref, kt_ref, vt_ref, qkm_ref, vm_ref, om_ref, gm_ref, gates_ref, ctail_ref,
                   conv_scr, hb_scr, *, bt, tt):
    t = pl.program_id(1)
    d = x_ref.shape[-1]
    m = bt * tt
    x = x_ref[...]
    ms = jnp.mean(x * x, axis=-1, keepdims=True)
    h = x * lax.rsqrt(ms + EPS) * g_ref[...]
    h = h * (1.0 + sc_ref[0, :, 0]) + sh_ref[0, :, 0]
    hb_scr[...] = h.reshape(m, d).astype(BF16)

    def proj(c0, width):
        return jnp.dot(hb_scr[...], w_ref[0, :, c0:c0 + width], preferred_element_type=F32)

    aw = ATT_WIDTH
    mw = M_WIDTH
    c0 = 3 * aw

    @pl.when(t == 0)
    def _():
        conv_scr[:, 0:SUBLANES, :] = cinit_ref[...]

    conv_scr[:, SUBLANES:SUBLANES + tt, :] = proj(c0, 2 * mw).reshape(bt, tt, 2 * mw)
    lo = SUBLANES - (CONV_W - 1)
    rc = min(tt, CONV_PIECE_ROWS)
    for r0 in range(0, tt, rc):
        for k0 in range(0, 2 * mw, mw):
            ks = slice(k0, k0 + mw)
            acc = conv_scr[:, lo + r0:lo + r0 + rc, ks] * cw_ref[0:1, ks]
            for j in range(1, CONV_W):
                acc = acc + conv_scr[:, lo + j + r0:lo + j + r0 + rc, ks] * cw_ref[j:j + 1, ks]
            acc = acc + cb_ref[:, ks]
            act = acc * _sigmoid(acc)
            if k0 >= mw:
                act = act * (M_HEAD_DIM ** -0.5)
            qkm_ref[:, r0:r0 + rc, ks] = act.astype(BF16)
    tail = conv_scr[:, tt:tt + SUBLANES, :]
    ctail_ref[...] = tail
    conv_scr[:, 0:SUBLANES, :] = tail
    u = proj(0, 3 * aw)
    qkv_ref[:, :, 0:aw] = (u[:, 0:aw] * (LOG2E * ATT_HEAD_DIM ** -0.5)).astype(BF16).reshape(bt, tt, aw)
    qkv_ref[:, :, aw:3 * aw] = u[:, aw:3 * aw].astype(BF16).reshape(bt, tt, 2 * aw)
    kt_ref[...] = u[:, aw:2 * aw].reshape(bt, tt, aw)
    vt_ref[...] = u[:, 2 * aw:3 * aw].reshape(bt, tt, aw)
    c0 += 2 * mw
    u = proj(c0, 2 * mw)
    vm_ref[...] = u[:, 0:mw].astype(BF16).reshape(bt, tt, mw)
    om_ref[...] = u[:, mw:2 * mw].reshape(bt, tt, mw)
    c0 += 2 * mw
    for j in range(0, 2 * d, d):
        gm_ref[:, :, j:j + d] = proj(c0 + j, d).reshape(bt, tt, d)
    ug = jnp.dot(hb_scr[...], wg_ref[0], preferred_element_type=F32) + bif_ref[...]
    col = lax.broadcasted_iota(jnp.int32, ug.shape, 1)
    gates_ref[...] = jnp.where(col >= M_HEADS, _log_sigmoid(ug), ug).reshape(bt, tt, LANES)


def _inproj_call(x, mod, layer, row0, g_mix, w_main, w_gates, b_if, conv_w, conv_b, conv_init, bt, tt):
    b, t, d = x.shape
    nb, nt = b // bt, t // tt
    keep = min(N_PAST_CHUNKS * CHUNK, t)
    assert b % bt == 0 and t % tt == 0 and keep % tt == 0
    t_keep0 = nt - keep // tt
    n_main = w_main.shape[2]
    aw, mw = ATT_WIDTH, M_WIDTH

    def wspec(shape):
        return _const_spec((1,) + shape, lambda i, j: (layer, 0, 0))

    def tok(width):
        return pl.BlockSpec((bt, tt, width), lambda i, j: (i, j, 0))

    def modspec(k):
        return pl.BlockSpec((1, bt, 1, 1, d), lambda i, j: (layer, row0 // bt + i, k, 0, 0))

    def const2(shape):
        return _const_spec(shape, lambda i, j: (0, 0))

    keep_spec = pl.BlockSpec((bt, tt, aw), lambda i, j: (i, jnp.maximum(j - t_keep0, 0), 0))
    per_batch8 = pl.BlockSpec((bt, SUBLANES, 2 * mw), lambda i, j: (i, 0, 0))
    outs = pl.pallas_call(
        functools.partial(_inproj_kernel, bt=bt, tt=tt),
        grid=(nb, nt),
        in_specs=[
            tok(d), modspec(0), modspec(1), const2((1, d)),
            wspec((d, n_main)), wspec((d, LANES)), const2((1, LANES)),
            const2((CONV_W, 2 * mw)), const2((1, 2 * mw)), per_batch8,
        ],
        out_specs=[tok(3 * aw), keep_spec, keep_spec, tok(2 * mw), tok(mw), tok(mw), tok(2 * d),
                   tok(LANES), per_batch8],
        out_shape=[
            jax.ShapeDtypeStruct((b, t, 3 * aw), BF16),
            jax.ShapeDtypeStruct((b, keep, aw), F32),
            jax.ShapeDtypeStruct((b, keep, aw), F32),
            jax.ShapeDtypeStruct((b, t, 2 * mw), BF16),
            jax.ShapeDtypeStruct((b, t, mw), BF16),
            jax.ShapeDtypeStruct((b, t, mw), F32),
            jax.ShapeDtypeStruct((b, t, 2 * d), F32),
            jax.ShapeDtypeStruct((b, t, LANES), F32),
            jax.ShapeDtypeStruct((b, SUBLANES, 2 * mw), F32),
        ],
        scratch_shapes=[pltpu.VMEM((bt, SUBLANES + tt, 2 * mw), F32), pltpu.VMEM((bt * tt, d), BF16)],
        compiler_params=_params(("parallel", "arbitrary")),
        name="inproj",
    )(x, mod, mod, g_mix, w_main, w_gates, b_if, conv_w, conv_b, conv_init)
    return outs


def _bias_kernel(rev_ref, o_ref):
    w = rev_ref.shape[-1]
    x = jnp.broadcast_to(rev_ref[0, 0], (CHUNK, w))
    r = pltpu.roll(x, w - (CHUNK - 1), 1, stride=1, stride_axis=0)
    o_ref[0, 0] = r[:, 0:BAND] * LOG2E


def _bias_call(rel_bias):
    depth, nh, rel = rel_bias.shape
    assert rel == MAX_REL + CHUNK and nh == ATT_HEADS
    n_ext = BAND + CHUNK - 1
    w = -(-n_ext // LANES) * LANES
    ext = jnp.concatenate([rel_bias, jnp.broadcast_to(rel_bias[:, :, rel - 1:], (depth, nh, n_ext - rel))], axis=2)
    rev = jnp.pad(ext[:, :, ::-1], ((0, 0), (0, 0), (0, w - n_ext))).reshape(depth, nh, 1, w)
    out = pl.pallas_call(
        _bias_kernel,
        grid=(depth, nh),
        in_specs=[pl.BlockSpec((1, 1, 1, w), lambda l, h: (l, h, 0, 0))],
        out_specs=pl.BlockSpec((1, 1, CHUNK, BAND), lambda l, h: (l, h, 0, 0)),
        out_shape=jax.ShapeDtypeStruct((depth, nh, CHUNK, BAND), F32),
        compiler_params=_params(("parallel", "parallel")),
        name="rel_bias",
    )(rev)
    return out.reshape(depth, N_PAIRS, 2 * CHUNK, BAND)


def _pair_rows(qp):
    lane = lax.broadcasted_iota(jnp.int32, qp.shape, 1)
    zero = jnp.zeros_like(qp)
    return jnp.concatenate([jnp.where(lane < ATT_HEAD_DIM, qp, zero),
                            jnp.where(lane >= ATT_HEAD_DIM, qp, zero)], axis=0)


def _pair_merge(o, rows):
    lane = lax.broadcasted_iota(jnp.int32, (rows, LANES), 1)
    return jnp.where(lane < ATT_HEAD_DIM, o[0:rows], o[rows:2 * rows])


def _nt_dot(a, b):
    return lax.dot_general(a, b, (((1,), (1,)), ((), ())), preferred_element_type=F32)


def _attn_prompt_kernel(q_ref, kw_ref, vw_ref, bias_ref, o_ref, kscr, vscr, *, unroll):
    j = pl.program_id(1)
    ab, rows = q_ref.shape[0], q_ref.shape[1]
    past = N_PAST_CHUNKS * CHUNK
    assert rows == past and kw_ref.shape[1] == past + rows

    def chunks(i, masked):
        kband, vband = (kscr, vscr) if masked else (kw_ref, vw_ref)
        units = [(e, pl.multiple_of((i * unroll + u) * CHUNK, CHUNK), i * unroll + u, p)
                 for e in range(ab) for u in range(unroll) for p in range(N_PAIRS)]
        scores = []
        for e, r0, ci, p in units:
            ls = slice(p * LANES, (p + 1) * LANES)
            qs = _pair_rows(q_ref[e, pl.ds(r0, CHUNK), ls])
            scores.append(_nt_dot(qs, kband[e, pl.ds(r0, BAND), ls]) + bias_ref[0, p])
        for (e, r0, ci, p), s in zip(units, scores):
            ls = slice(p * LANES, (p + 1) * LANES)
            if masked:
                col = lax.broadcasted_iota(jnp.int32, s.shape, 1)
                s = jnp.where(col >= (N_PAST_CHUNKS - ci) * CHUNK, s, NEG)
            mx = jnp.max(s, axis=-1, keepdims=True)
            ex = jnp.exp2(s - mx)
            den = jnp.sum(ex, axis=-1, keepdims=True)
            o = jnp.dot(ex.astype(BF16), vband[e, pl.ds(r0, BAND), ls], preferred_element_type=F32) / den
            o_ref[e, pl.ds(r0, CHUNK), ls] = _pair_merge(o, CHUNK).astype(BF16)

    n_iter = rows // (CHUNK * unroll)

    @pl.when(j == 0)
    def _():
        kscr[:, 0:past, :] = jnp.zeros((ab, past, ATT_WIDTH), BF16)
        vscr[:, 0:past, :] = jnp.zeros((ab, past, ATT_WIDTH), BF16)
        kscr[:, past:past + rows, :] = kw_ref[:, 0:rows, :]
        vscr[:, past:past + rows, :] = vw_ref[:, 0:rows, :]
        lax.fori_loop(0, n_iter, lambda i, c: (chunks(i, True), c)[1], 0)

    @pl.when(j > 0)
    def _():
        lax.fori_loop(0, n_iter, lambda i, c: (chunks(i, False), c)[1], 0)


def _attn_prompt_call(qkv, bias, layer):
    b, t, _ = qkv.shape
    rows = N_PAST_CHUNKS * CHUNK
    assert t % rows == 0 and t >= 2 * rows
    aw = ATT_WIDTH
    ab = next(c for c in (ATTN_BATCH_TILE, 1) if b % c == 0)

    def cur(cb):
        return pl.BlockSpec((ab, rows, aw), lambda i, j: (i, j, cb))

    def window(cb):
        return pl.BlockSpec((pl.Element(ab), pl.Element(2 * rows), pl.Element(aw)),
                            lambda i, j: (i * ab, jnp.maximum(j - 1, 0) * rows, cb * aw))

    return pl.pallas_call(
        functools.partial(_attn_prompt_kernel, unroll=ATTN_UNROLL),
        grid=(b // ab, t // rows),
        in_specs=[cur(0), window(1), window(2),
                  _const_spec((1, N_PAIRS, 2 * CHUNK, BAND), lambda i, j: (layer, 0, 0, 0))],
        out_specs=pl.BlockSpec((ab, rows, aw), lambda i, j: (i, j, 0)),
        out_shape=jax.ShapeDtypeStruct((b, t, aw), BF16),
        scratch_shapes=[pltpu.VMEM((ab, 2 * rows, aw), BF16), pltpu.VMEM((ab, 2 * rows, aw), BF16)],
        compiler_params=_params(("parallel", "parallel")),
        name="attn_prompt",
    )(qkv, qkv, qkv, bias)


def _attn_sample_kernel(q_ref, kn_ref, vn_ref, kc_ref, vc_ref, bias_ref, o_ref, *, bb):
    tq = q_ref.shape[1]
    pc = kc_ref.shape[3]
    for bi in range(bb):
        scores = []
        for p in range(N_PAIRS):
            ls = slice(p * LANES, (p + 1) * LANES)
            qs = _pair_rows(q_ref[bi, :, ls])
            bias = jnp.concatenate([bias_ref[0, p, 0:tq, :], bias_ref[0, p, CHUNK:CHUNK + tq, :]], axis=0)
            s1 = jnp.dot(qs, kc_ref[0, bi, ls, :].astype(BF16), preferred_element_type=F32)
            scores.append((s1 + bias[:, 0:pc], _nt_dot(qs, kn_ref[bi, :, ls]) + bias[:, pc:pc + tq]))
        for p, (s1, s2) in enumerate(scores):
            ls = slice(p * LANES, (p + 1) * LANES)
            mx = jnp.maximum(jnp.max(s1, axis=-1, keepdims=True), jnp.max(s2, axis=-1, keepdims=True))
            e1 = jnp.exp2(s1 - mx)
            e2 = jnp.exp2(s2 - mx)
            den = jnp.sum(e1, axis=-1, keepdims=True) + jnp.sum(e2, axis=-1, keepdims=True)
            o = (_nt_dot(e1.astype(BF16), vc_ref[0, bi, ls, :].astype(BF16))
                 + jnp.dot(e2.astype(BF16), vn_ref[bi, :, ls], preferred_element_type=F32)) / den
            o_ref[bi, :, ls] = _pair_merge(o, tq).astype(BF16)


def _attn_sample_call(qkv, cache_k, cache_v, bias, layer, bb):
    b, tq, _ = qkv.shape
    _, _, aw, pc = cache_k.shape
    assert b % bb == 0 and pc == N_PAST_CHUNKS * CHUNK and tq <= CHUNK and aw == ATT_WIDTH

    def new(cb):
        return pl.BlockSpec((bb, tq, aw), lambda i: (i, 0, cb))

    cache = pl.BlockSpec((1, bb, aw, pc), lambda i: (layer, i, 0, 0))
    return pl.pallas_call(
        functools.partial(_attn_sample_kernel, bb=bb),
        grid=(b // bb,),
        in_specs=[new(0), new(1), new(2), cache, cache,
                  _const_spec((1, N_PAIRS, 2 * CHUNK, BAND), lambda i: (layer, 0, 0, 0))],
        out_specs=pl.BlockSpec((bb, tq, aw), lambda i: (i, 0, 0)),
        out_shape=jax.ShapeDtypeStruct((b, tq, aw), BF16),
        compiler_params=_params(("parallel",)),
        name="attn_sample",
    )(qkv, qkv, qkv, cache_k, cache_v, bias)


def _transpose_rows(a):
    r = a.shape[0]
    pad = -r % LANES
    if pad:
        a = jnp.concatenate([a, jnp.zeros((pad, a.shape[1]), a.dtype)], axis=0)
    return a.T[:, 0:r]


def _transpose_cols(a):
    c = a.shape[1]
    pad = -c % LANES
    if pad:
        a = jnp.concatenate([a, jnp.zeros((a.shape[0], pad), a.dtype)], axis=1)
    return a.T[0:c, :]


def _block_cumsum(tri, g):
    g1 = g.astype(BF16)
    r1 = g - g1.astype(F32)
    g2 = r1.astype(BF16)
    g3 = (r1 - g2.astype(F32)).astype(BF16)
    return (jnp.dot(tri, g1, preferred_element_type=F32) + jnp.dot(tri, g2, preferred_element_type=F32)
            + jnp.dot(tri, g3, preferred_element_type=F32))


def _mlstm_kernel(qkm_ref, vm_ref, om_ref, gates_ref, gain_ref, tri_ref, c0_ref, n0_ref, m0_ref,
                  ym_ref, c1_ref, n1_ref, m1_ref, cn_scr, m_scr, *, bb, blk, nblk):
    tb = pl.program_id(1)
    hd = M_HEAD_DIM
    ext = cn_scr.shape[2] - hd

    def last_lane(r):
        return jnp.broadcast_to(r[:, blk - 1:blk], r.shape)

    ext_row0 = lax.broadcasted_iota(jnp.int32, (ext, hd), 0) == 0

    @pl.when(tb == 0)
    def _():
        for e in range(bb):
            for h in range(M_HEADS):
                cn_scr[e, h, 0:hd, :] = c0_ref[e, h].T
                cn_scr[e, h, hd:hd + ext, :] = jnp.where(ext_row0, n0_ref[e, h], 0.0)
        m_scr[...] = m0_ref[...]

    srow = lax.broadcasted_iota(jnp.int32, (blk, blk), 0)
    tcol = lax.broadcasted_iota(jnp.int32, (blk, blk), 1)
    causal = srow <= tcol
    ones_ext = jnp.where(lax.broadcasted_iota(jnp.int32, (ext, blk), 0) == 0, 1.0, 0.0)

    gt_all, bt_all, c_all = [], [], []
    for e in range(bb):
        g = gates_ref[e]
        b = _block_cumsum(tri_ref[...], g)
        gt_all.append(_transpose_rows(g))
        bt_all.append(_transpose_rows(b))
        c_all.append(g[:, 0:M_HEADS] - b[:, M_HEADS:2 * M_HEADS])

    units = [(e, h, bi) for e in range(bb) for h in range(M_HEADS) for bi in range(nblk)]
    vt_heads = {(e, h): _transpose_rows(vm_ref[e, :, h * hd:(h + 1) * hd].astype(F32))
                for e in range(bb) for h in range(M_HEADS)}
    scores = {}
    for e, h, bi in units:
        rs = slice(bi * blk, (bi + 1) * blk)
        q = qkm_ref[e, rs, h * hd:(h + 1) * hd]
        k = qkm_ref[e, rs, M_WIDTH + h * hd:M_WIDTH + (h + 1) * hd]
        b_row = bt_all[e][M_HEADS + h:M_HEADS + h + 1, rs]
        dmat = jnp.where(causal, b_row + c_all[e][rs, h:h + 1], NEG)
        m_in = jnp.max(dmat, axis=0, keepdims=True)
        scores[e, h, bi] = (m_in, _nt_dot(k, q) * jnp.exp(dmat - m_in))
    intra = {}
    for e, h, bi in units:
        rs = slice(bi * blk, (bi + 1) * blk)
        k = qkm_ref[e, rs, M_WIDTH + h * hd:M_WIDTH + (h + 1) * hd]
        li_row, b_row = gt_all[e][h:h + 1, rs], bt_all[e][M_HEADS + h:M_HEADS + h + 1, rs]
        m_in, smat = scores[e, h, bi]
        vt = jnp.concatenate([vt_heads[e, h][:, rs], ones_ext], axis=0)
        den_i = jnp.sum(smat, axis=0, keepdims=True)
        num_i = jnp.dot(vt[0:hd].astype(BF16), smat.astype(BF16), preferred_element_type=F32)
        b_last = last_lane(b_row)
        m_u = last_lane(m_in)
        w_u = jnp.exp(b_last - b_row + li_row - m_u)
        upd = jnp.dot((vt * w_u).astype(BF16), k, preferred_element_type=F32)
        intra[e, h, bi] = (den_i, num_i, b_last, m_u, upd)
    for bi in range(nblk):
        rs = slice(bi * blk, (bi + 1) * blk)
        for e, h in [(e, h) for e in range(bb) for h in range(M_HEADS)]:
            ls = slice(h * hd, (h + 1) * hd)
            q = qkm_ref[e, rs, ls]
            b_row = bt_all[e][M_HEADS + h:M_HEADS + h + 1, rs]
            m_in = scores[e, h, bi][0]
            den_i, num_i, b_last, m_u, upd = intra[e, h, bi]
            m_prev = m_scr[e, h][:, 0:blk]
            cn_prev = cn_scr[e, h]
            a = b_row + m_prev
            m_t = jnp.maximum(a, m_in)
            alpha = jnp.exp(m_in - m_t)
            beta = jnp.exp(a - m_t)
            inter = _nt_dot(cn_prev.astype(BF16), q)
            num = alpha * num_i + beta * inter[0:hd]
            den = alpha * den_i + beta * inter[hd:hd + 1]
            hh = num / jnp.maximum(jnp.abs(den), jnp.exp(-m_t))
            hh = hh * lax.rsqrt(jnp.mean(hh * hh, axis=0, keepdims=True) + EPS)
            y = _transpose_cols(hh) * gain_ref[:, ls] * _sigmoid(om_ref[e, rs, ls])
            ym_ref[e, rs, ls] = y.astype(BF16)
            m_new = last_lane(m_t)
            g_state = jnp.exp(b_last + m_prev - m_new)[:, 0:1]
            g_upd = jnp.exp(m_u - m_new)[:, 0:1]
            cn_scr[e, h] = g_state * cn_prev + g_upd * upd
            m_scr[e, h] = jnp.broadcast_to(m_new[:, 0:1], (1, LANES))

    @pl.when(tb == pl.num_programs(1) - 1)
    def _():
        for e in range(bb):
            for h in range(M_HEADS):
                c1_ref[e, h] = cn_scr[e, h, 0:hd, :].T
                n1_ref[e, h] = cn_scr[e, h, hd:hd + 1, :]
        m1_ref[...] = m_scr[...]


def _block_tril(rt, blk):
    r = jnp.arange(rt)
    return ((r[:, None] // blk == r[None, :] // blk) & (r[None, :] <= r[:, None])).astype(BF16)


def _mlstm_call(qkm, vm, om, gates, gain, c0, n0, m0, bb, blk, nblk):
    b, t, _ = qkm.shape
    rt = blk * nblk
    assert t % rt == 0 and b % bb == 0
    mw, hd, nh = M_WIDTH, M_HEAD_DIM, M_HEADS

    def tok(width):
        return pl.BlockSpec((bb, rt, width), lambda i, j: (i, j, 0))

    cspec = pl.BlockSpec((bb, nh, hd, hd), lambda i, j: (i, 0, 0, 0))
    vspec = pl.BlockSpec((bb, nh, 1, hd), lambda i, j: (i, 0, 0, 0))
    return pl.pallas_call(
        functools.partial(_mlstm_kernel, bb=bb, blk=blk, nblk=nblk),
        grid=(b // bb, t // rt),
        in_specs=[tok(2 * mw), tok(mw), tok(mw), tok(LANES),
                  _const_spec((1, mw), lambda i, j: (0, 0)), _const_spec((rt, rt), lambda i, j: (0, 0)),
                  cspec, vspec, vspec],
        out_specs=[tok(mw), cspec, vspec, vspec],
        out_shape=[
            jax.ShapeDtypeStruct((b, t, mw), BF16),
            jax.ShapeDtypeStruct((b, nh, hd, hd), F32),
            jax.ShapeDtypeStruct((b, nh, 1, hd), F32),
            jax.ShapeDtypeStruct((b, nh, 1, hd), F32),
        ],
        scratch_shapes=[pltpu.VMEM((bb, nh, hd + STATE_EXT_ROWS, hd), F32), pltpu.VMEM((bb, nh, 1, hd), F32)],
        compiler_params=_params(("parallel", "arbitrary")),
        name="mlstm",
    )(qkm, vm, om, gates, gain, _block_tril(rt, blk), c0, n0, m0)


def _ffn_kernel(x_ref, ya_ref, ym_ref, gm_ref, gt1_ref, sh2_ref, sc2_ref, gt2_ref, gffn_ref, gfin_ref,
                wba_ref, wbm_ref, wo_ref, wgu_ref, wd_ref, o_ref, act_scr, *, bt, tt, final):
    d = x_ref.shape[-1]
    m = bt * tt
    dff = wd_ref.shape[1]
    x = x_ref[...]
    pa = jnp.dot(ya_ref[...].reshape(m, ATT_WIDTH), wba_ref[0], preferred_element_type=F32)
    pm = jnp.dot(ym_ref[...].reshape(m, M_WIDTH), wbm_ref[0], preferred_element_type=F32)
    ga = gm_ref[:, :, 0:d].reshape(m, d)
    gmm = gm_ref[:, :, d:2 * d].reshape(m, d)
    merged = _sigmoid(ga) * pa + _sigmoid(gmm) * pm
    y1 = jnp.dot(merged.astype(BF16), wo_ref[0], preferred_element_type=F32)
    x1 = x + gt1_ref[0, :, 0] * y1.reshape(bt, tt, d)
    ms = jnp.mean(x1 * x1, axis=-1, keepdims=True)
    h2 = x1 * lax.rsqrt(ms + EPS) * gffn_ref[...]
    h2 = (h2 * (1.0 + sc2_ref[0, :, 0]) + sh2_ref[0, :, 0]).reshape(m, d).astype(BF16)
    cw = 256
    for c in range(0, dff, cw):
        g = jnp.dot(h2, wgu_ref[0, :, c:c + cw], preferred_element_type=F32)
        up = jnp.dot(h2, wgu_ref[0, :, dff + c:dff + c + cw], preferred_element_type=F32)
        act_scr[:, c:c + cw] = (g * _sigmoid(g) * up).astype(BF16)
    y2 = jnp.dot(act_scr[...], wd_ref[0], preferred_element_type=F32)
    x2 = x1 + gt2_ref[0, :, 0] * y2.reshape(bt, tt, d)
    if final:
        ms2 = jnp.mean(x2 * x2, axis=-1, keepdims=True)
        x2 = x2 * lax.rsqrt(ms2 + EPS) * gfin_ref[...]
    o_ref[...] = x2


def _ffn_call(x, ya, ym, gm, mod, layer, row0, g_ffn, g_final, w_br_att, w_br_mlstm, w_out, w_gate_up,
              w_down, bt, tt, final):
    b, t, d = x.shape
    dff = w_down.shape[1]
    assert b % bt == 0 and t % tt == 0 and dff % 256 == 0

    def wspec(shape):
        return _const_spec((1,) + shape, lambda i, j: (layer, 0, 0))

    def tok(width):
        return pl.BlockSpec((bt, tt, width), lambda i, j: (i, j, 0))

    def modspec(k):
        return pl.BlockSpec((1, bt, 1, 1, d), lambda i, j: (layer, row0 // bt + i, k, 0, 0))

    def const2(shape):
        return _const_spec(shape, lambda i, j: (0, 0))

    return pl.pallas_call(
        functools.partial(_ffn_kernel, bt=bt, tt=tt, final=final),
        grid=(b // bt, t // tt),
        in_specs=[tok(d), tok(ATT_WIDTH), tok(M_WIDTH), tok(2 * d),
                  modspec(2), modspec(3), modspec(4), modspec(5), const2((1, d)), const2((1, d)),
                  wspec((ATT_WIDTH, d)), wspec((M_WIDTH, d)), wspec((d, d)), wspec((d, 2 * dff)),
                  wspec((dff, d))],
        out_specs=tok(d),
        out_shape=jax.ShapeDtypeStruct((b, t, d), F32),
        scratch_shapes=[pltpu.VMEM((bt * tt, dff), BF16)],
        compiler_params=_params(("parallel", "parallel")),
        name="merge_ffn",
    )(x, ya, ym, gm, mod, mod, mod, mod, g_ffn, g_final, w_br_att, w_br_mlstm, w_out, w_gate_up, w_down)


def _layer(x, mod, layer, row0, cache, conv_init, c0, n0, m0, wts, bt, tt, blk, nblk, final):
    (g_mix, w_main, w_gates, b_if, conv_w, conv_b, bias, mh_gain, w_br_att, w_br_mlstm, w_out, g_ffn,
     w_gate_up, w_down, g_final) = wts
    b, t, _ = x.shape
    qkv, kt, vt, qkm, vm, om, gm, gates, ctail = _inproj_call(
        x, mod, layer, row0, g_mix, w_main, w_gates, b_if, conv_w, conv_b, conv_init, bt, tt)
    if cache is None:
        ya = _attn_prompt_call(qkv, bias, layer)
    else:
        ya = _attn_sample_call(qkv, cache[0], cache[1], bias, layer, bt)
    mb = next(c for c in (MLSTM_BATCH_TILE, 2, 1) if b % c == 0)
    ym, c1, n1, m1 = _mlstm_call(qkm, vm, om, gates, mh_gain, c0, n0, m0, mb, blk, nblk)
    x = _ffn_call(x, ya, ym, gm, mod, layer, row0, g_ffn, g_final, w_br_att, w_br_mlstm, w_out, w_gate_up,
                  w_down, bt, tt, final)
    keep = kt.shape[1]
    state = (kt.reshape(b, keep, ATT_HEADS, ATT_HEAD_DIM), vt.reshape(b, keep, ATT_HEADS, ATT_HEAD_DIM),
             ctail[:, SUBLANES - (CONV_W - 1):, :], c1, n1[:, :, 0, :], m1[:, :, 0, 0])
    return x, state


def kernel(x_prompt, x_sample, cache_k, cache_v, state_conv, state_C, state_n, state_m, c_prompt, c_sample,
           w_ada, b_ada, g_mix, w_in, b_if, conv_w, conv_b, rel_bias, mh_gain, w_br_att, w_br_mlstm, w_out,
           g_ffn, w_gate_up, w_down, g_final):
    depth = w_ada.shape[0]
    bp, tp, d = x_prompt.shape
    bs, ts, _ = x_sample.shape
    assert tp % PROMPT_ROWS == 0 and tp % CHUNK == 0 and bs % SAMPLE_BATCH_TILE == 0 and ts % CHUNK != 0
    mw = M_WIDTH

    mod = _ada_call(jnp.concatenate([c_sample, c_prompt], axis=0), w_ada, b_ada)
    mod = mod.reshape(depth, bs + bp, 6, 1, d)

    bias = _bias_call(rel_bias)
    pc = cache_k.shape[2]
    caches = tuple(jnp.transpose(c, (0, 1, 3, 4, 2)).reshape(depth, bs, ATT_WIDTH, pc) for c in (cache_k, cache_v))

    pad3 = SUBLANES - (CONV_W - 1)
    zero_conv = jnp.zeros((bp, SUBLANES, 2 * mw), F32)
    zero_c = jnp.zeros((bp, M_HEADS, M_HEAD_DIM, M_HEAD_DIM), F32)
    zero_v = jnp.zeros((bp, M_HEADS, 1, M_HEAD_DIM), F32)

    w_main, w_gates = _wprep_call(w_in)
    wb_att, wb_mlstm, wb_out = w_br_att.astype(BF16), w_br_mlstm.astype(BF16), w_out.astype(BF16)
    wb_gate_up, wb_down = w_gate_up.astype(BF16), w_down.astype(BF16)

    xp, xs = x_prompt, x_sample
    outs_p, outs_s = [], []
    for l in range(depth):
        bif = jnp.pad(b_if[l], (0, LANES - 2 * M_HEADS)).reshape(1, LANES)
        wts = (g_mix[l].reshape(1, d), w_main, w_gates, bif, conv_w[l], conv_b[l].reshape(1, 2 * mw),
               bias, mh_gain[l].reshape(1, mw), wb_att, wb_mlstm, wb_out, g_ffn[l].reshape(1, d), wb_gate_up,
               wb_down, g_final.reshape(1, d))
        final = l == depth - 1
        xp, st_p = _layer(xp, mod, l, bs, None, zero_conv, zero_c, zero_v, zero_v, wts,
                          1, PROMPT_ROWS, MLSTM_BLOCK, MLSTM_TILE_ROWS // MLSTM_BLOCK, final)
        conv_init = jnp.pad(state_conv[l], ((0, 0), (pad3, 0), (0, 0)))
        m0 = jnp.broadcast_to(state_m[l][:, :, None, None], (bs, M_HEADS, 1, M_HEAD_DIM))
        xs, st_s = _layer(xs, mod, l, 0, caches, conv_init, state_C[l],
                          state_n[l][:, :, None, :], m0, wts, SAMPLE_BATCH_TILE, ts, ts, 1, final)
        outs_p.append(st_p)
        outs_s.append(st_s)

    def stk(outs, i):
        return jnp.stack([o[i] for o in outs], axis=0)

    return (xp, xs,
            stk(outs_p, 0), stk(outs_p, 1), stk(outs_p, 2), stk(outs_p, 3), stk(outs_p, 4), stk(outs_p, 5),
            stk(outs_s, 0), stk(outs_s, 1), stk(outs_s, 2), stk(outs_s, 3), stk(outs_s, 4), stk(outs_s, 5))
```
